```python
import jax, jax.numpy as jnp
from jax import lax
import numpy as np

D_MODEL = 2048
BATCH = 4
SEQ = 2048
DEPTH = 1
DEC_BATCH = 128
DEC_SEQ = 1
PAST_LEN = 16384
PAGE_SIZE = 128

ATTN_HEADS = 16
ATTN_KV_HEADS = 4
ATTN_HEAD_DIM = 64
WINDOW = 128
MLSTM_HEADS = 4
MLSTM_QK_DIM = 128
MLSTM_V_DIM = 256
MLSTM_CHUNK = 64
N_EXPERTS = 32
TOP_K = 4
D_EXPERT = 2048
SWIGLU_LIMIT = 7.0
SWIGLU_ALPHA = 1.702
MOE_BLOCK = 128
NORM_EPS = 1e-6
N_MOD = 6

ATTN_Q_W = ATTN_HEADS * ATTN_HEAD_DIM
ATTN_KV_W = ATTN_KV_HEADS * ATTN_HEAD_DIM
MLSTM_QK_W = MLSTM_HEADS * MLSTM_QK_DIM
MLSTM_V_W = MLSTM_HEADS * MLSTM_V_DIM
GQA_GROUP = ATTN_HEADS // ATTN_KV_HEADS
IN_SPLITS = (ATTN_Q_W, ATTN_KV_W, ATTN_KV_W, MLSTM_QK_W, MLSTM_QK_W, MLSTM_V_W, MLSTM_V_W,
             MLSTM_HEADS, MLSTM_HEADS, D_MODEL, D_MODEL)
IN_WIDTH = sum(IN_SPLITS)

kernel_name = 'hybrid_mlstm_swa_moe_adaln_step'


def _rmsnorm(x, g):
    xf = x.astype(jnp.float32)
    y = xf * lax.rsqrt(jnp.mean(xf * xf, axis=-1, keepdims=True) + NORM_EPS)
    return (y * g.astype(jnp.float32)).astype(x.dtype)


def _alibi_slopes():
    h = jnp.arange(1, ATTN_HEADS + 1, dtype=jnp.float32)
    return jnp.exp2(-8.0 * h / ATTN_HEADS)


def _window_attention(q, k, v, prefix_k, prefix_v, prefix_valid, sinks):
    B, S = q.shape[:2]
    Lq = WINDOW if S >= WINDOW else S
    S_pad = -(-S // Lq) * Lq
    nb = S_pad // Lq
    pad = ((0, 0), (0, S_pad - S), (0, 0), (0, 0))
    qb = jnp.pad(q, pad).reshape(B, nb, Lq, ATTN_KV_HEADS, GQA_GROUP, ATTN_HEAD_DIM)
    k_full = jnp.concatenate([prefix_k.astype(k.dtype), jnp.pad(k, pad)], axis=1)
    v_full = jnp.concatenate([prefix_v.astype(v.dtype), jnp.pad(v, pad)], axis=1)
    if Lq == WINDOW:
        kb = k_full.reshape(B, nb + 1, WINDOW, ATTN_KV_HEADS, ATTN_HEAD_DIM)
        vb = v_full.reshape(B, nb + 1, WINDOW, ATTN_KV_HEADS, ATTN_HEAD_DIM)
        k_ctx = jnp.concatenate([kb[:, :-1], kb[:, 1:]], axis=2)
        v_ctx = jnp.concatenate([vb[:, :-1], vb[:, 1:]], axis=2)
    else:
        k_ctx, v_ctx = k_full[:, None], v_full[:, None]
    Lk = WINDOW + Lq
    dist = WINDOW + jnp.arange(Lq)[:, None] - jnp.arange(Lk)[None, :]
    full_idx = jnp.arange(nb)[:, None] * Lq + jnp.arange(Lk)[None, :]
    key_ok = (full_idx >= WINDOW) | prefix_valid
    mask = ((dist >= 0) & (dist < WINDOW))[None] & key_ok[:, None, :]
    slopes = _alibi_slopes().reshape(ATTN_KV_HEADS, GQA_GROUP, 1, 1)
    s = jnp.einsum('bnqhgd,bnkhd->bnhgqk', qb, k_ctx,
                   preferred_element_type=jnp.float32) * (ATTN_HEAD_DIM ** -0.5)
    s = s - slopes * dist.astype(jnp.float32)
    s = jnp.where(mask[None, :, None, None], s, -jnp.inf)
    sink = jnp.broadcast_to(
        sinks.astype(jnp.float32).reshape(1, 1, ATTN_KV_HEADS, GQA_GROUP, 1, 1), s.shape[:-1] + (1,))
    p = jax.nn.softmax(jnp.concatenate([s, sink], axis=-1), axis=-1)[..., :-1]
    o = jnp.einsum('bnhgqk,bnkhd->bnqhgd', p.astype(v.dtype), v_ctx)
    o = o.reshape(B, S_pad, ATTN_Q_W)[:, :S]
    return o, k_full[:, S:S + WINDOW], v_full[:, S:S + WINDOW]


def _mlstm(q, k, v, log_i, log_f, C0, n0, m0):
    B, S = q.shape[:2]
    L = MLSTM_CHUNK if S >= MLSTM_CHUNK else S
    S_pad = -(-S // L) * L
    nc = S_pad // L

    def chunks(a, fill):
        a = jnp.pad(a, [(0, 0), (0, S_pad - S)] + [(0, 0)] * (a.ndim - 2), constant_values=fill)
        return jnp.moveaxis(a.reshape((B, nc, L) + a.shape[2:]), 1, 0)

    qc = chunks(q.astype(jnp.float32), 0.0)
    kc = chunks(k.astype(jnp.float32) * (MLSTM_QK_DIM ** -0.5), 0.0)
    vc = chunks(v.astype(jnp.float32), 0.0)
    ic = chunks(log_i, -jnp.inf)
    fc = chunks(log_f, 0.0)
    causal = jnp.tril(jnp.ones((L, L), dtype=bool))

    def step(carry, inp):
        C, n, m = carry
        qb, kb, vb, ib, fb = inp
        ib = jnp.swapaxes(ib, 1, 2)
        b = jnp.cumsum(jnp.swapaxes(fb, 1, 2), axis=-1)
        log_d = jnp.where(causal, ib[:, :, None, :] + b[:, :, :, None] - b[:, :, None, :], -jnp.inf)
        m_inter = m[:, :, None] + b
        m_t = jnp.maximum(m_inter, jnp.max(log_d, axis=-1))
        d = jnp.exp(log_d - m_t[..., None])
        a_inter = jnp.exp(m_inter - m_t)
        w = jnp.einsum('bthd,bshd->bhts', qb, kb) * d
        num = (jnp.einsum('bhts,bshv->bthv', w, vb)
               + jnp.einsum('bthd,bhdv->bthv', qb, C) * jnp.swapaxes(a_inter, 1, 2)[..., None])
        den = jnp.sum(w, axis=-1) + a_inter * jnp.einsum('bthd,bhd->bht', qb, n)
        den = jnp.maximum(jnp.abs(den), jnp.exp(-m_t))
        h = num / jnp.swapaxes(den, 1, 2)[..., None]
        m_new = m_t[..., -1]
        decay = jnp.exp(ib + b[..., -1:] - b - m_new[..., None])
        carry_scale = jnp.exp(m + b[..., -1] - m_new)
        C_new = carry_scale[..., None, None] * C + jnp.einsum('bhs,bshd,bshv->bhdv', decay, kb, vb)
        n_new = carry_scale[..., None] * n + jnp.einsum('bhs,bshd->bhd', decay, kb)
        return (C_new, n_new, m_new), h

    init = (C0.astype(jnp.float32), n0.astype(jnp.float32), m0.astype(jnp.float32))
    (C, n, m), hs = lax.scan(step, init, (qc, kc, vc, ic, fc))
    h = jnp.moveaxis(hs, 0, 1).reshape(B, S_pad, MLSTM_HEADS, MLSTM_V_DIM)[:, :S]
    return h, C, n, m


def _clamped_swiglu(h):
    glu, lin = h[..., ::2], h[..., 1::2]
    glu = jnp.minimum(glu, SWIGLU_LIMIT)
    lin = jnp.clip(lin, -SWIGLU_LIMIT, SWIGLU_LIMIT)
    return glu * jax.nn.sigmoid(SWIGLU_ALPHA * glu) * (lin + 1.0)


def _moe(x, w_router, b_router, w_up, b_up, w_down, b_down):
    N, D = x.shape
    NK = N * TOP_K
    logits = jnp.dot(x, w_router, preferred_element_type=jnp.float32) + b_router.astype(jnp.float32)
    top_logit, top_e = lax.top_k(logits, TOP_K)
    gate = jax.nn.softmax(top_logit, axis=-1)
    flat_e = top_e.reshape(NK)
    order = jnp.argsort(flat_e)
    sorted_e = flat_e[order]
    sorted_tok = order // TOP_K
    sorted_gate = gate.reshape(NK)[order]
    counts = jnp.bincount(flat_e, length=N_EXPERTS)
    padded = (counts + MOE_BLOCK - 1) // MOE_BLOCK * MOE_BLOCK
    pad_end = jnp.cumsum(padded)
    pad_start = pad_end - padded
    grp_start = jnp.cumsum(counts) - counts
    dest = pad_start[sorted_e] + jnp.arange(NK) - grp_start[sorted_e]
    n_blocks = -(-NK // MOE_BLOCK) + N_EXPERTS
    row_tok = jnp.zeros((n_blocks * MOE_BLOCK,), jnp.int32).at[dest].set(sorted_tok)
    block_e = jnp.minimum(
        jnp.searchsorted(pad_end, jnp.arange(n_blocks) * MOE_BLOCK, side='right'), N_EXPERTS - 1)
    xb = x[row_tok].reshape(n_blocks, MOE_BLOCK, D)

    def expert_block(args):
        xe, e = args
        h = jnp.dot(xe, w_up[e]) + b_up[e]
        return jnp.dot(_clamped_swiglu(h), w_down[e]) + b_down[e]

    yb = lax.map(expert_block, (xb, block_e)).reshape(n_blocks * MOE_BLOCK, D)
    contrib = yb[dest] * sorted_gate[:, None].astype(yb.dtype)
    return jax.ops.segment_sum(contrib, sorted_tok, num_segments=N)


def _layer(x, c, prefix_k, prefix_v, prefix_valid, C0, n0, m0,
           w_ada, b_ada, g_mix, w_in, b_igate, b_fgate, q_norm_g, k_norm_g, attn_sinks,
           mlstm_norm_g, w_attn_up, w_mlstm_up, w_out, g_ffn, w_router, b_router,
           w_up, b_up, w_down, b_down):
    B, S, D = x.shape
    mod = (jax.nn.silu(c) @ w_ada + b_ada).reshape(B, N_MOD, 1, D)
    shift1, scale1, gate1, shift2, scale2, gate2 = (mod[:, i] for i in range(N_MOD))
    u = _rmsnorm(x, g_mix) * (1.0 + scale1) + shift1
    cuts = np.cumsum(IN_SPLITS)[:-1].tolist()
    aq, ak, av, mq, mk, mv, mo, mi, mf, ga, gm = jnp.split(u @ w_in, cuts, axis=-1)
    aq = _rmsnorm(aq.reshape(B, S, ATTN_HEADS, ATTN_HEAD_DIM), q_norm_g)
    ak = _rmsnorm(ak.reshape(B, S, ATTN_KV_HEADS, ATTN_HEAD_DIM), k_norm_g)
    av = av.reshape(B, S, ATTN_KV_HEADS, ATTN_HEAD_DIM)
    h_attn, win_k, win_v = _window_attention(aq, ak, av, prefix_k, prefix_v, prefix_valid, attn_sinks)
    log_i = (mi + b_igate).astype(jnp.float32)
    log_f = jax.nn.log_sigmoid((mf + b_fgate).astype(jnp.float32))
    h_m, C, n, m = _mlstm(mq.reshape(B, S, MLSTM_HEADS, MLSTM_QK_DIM),
                          mk.reshape(B, S, MLSTM_HEADS, MLSTM_QK_DIM),
                          mv.reshape(B, S, MLSTM_HEADS, MLSTM_V_DIM), log_i, log_f, C0, n0, m0)
    h_m = (_rmsnorm(h_m.astype(x.dtype), mlstm_norm_g.reshape(MLSTM_HEADS, MLSTM_V_DIM))
           * jax.nn.sigmoid(mo).reshape(B, S, MLSTM_HEADS, MLSTM_V_DIM))
    merged = (jax.nn.sigmoid(ga) * (h_attn @ w_attn_up)
              + jax.nn.sigmoid(gm) * (h_m.reshape(B, S, MLSTM_V_W) @ w_mlstm_up))
    x = x + gate1 * (merged @ w_out)
    u2 = _rmsnorm(x, g_ffn) * (1.0 + scale2) + shift2
    y = _moe(u2.reshape(B * S, D), w_router, b_router, w_up, b_up, w_down, b_down).reshape(B, S, D)
    x = x + gate2 * y
    return x, win_k, win_v, C, n, m


def setup_inputs(seed: int = 0) -> dict:
    key = jax.random.key(seed)
    keys = iter(jax.random.split(key, 32))

    def nrm(shape, scale=1.0):
        return jax.random.normal(next(keys), shape, jnp.float32) * scale

    L, D = DEPTH, D_MODEL
    return {
        'x_prompt': nrm((BATCH, SEQ, D)),
        'x_sample': nrm((DEC_BATCH, DEC_SEQ, D)),
        'cache_k': nrm((L, DEC_BATCH, WINDOW, ATTN_KV_HEADS, ATTN_HEAD_DIM)),
        'cache_v': nrm((L, DEC_BATCH, WINDOW, ATTN_KV_HEADS, ATTN_HEAD_DIM)),
        'state_C': nrm((L, DEC_BATCH, MLSTM_HEADS, MLSTM_QK_DIM, MLSTM_V_DIM), MLSTM_QK_DIM ** -0.5),
        'state_n': nrm((L, DEC_BATCH, MLSTM_HEADS, MLSTM_QK_DIM), MLSTM_QK_DIM ** -0.5),
        'state_m': nrm((L, DEC_BATCH, MLSTM_HEADS), 0.5),
        'c_prompt': nrm((BATCH, D)),
        'c_sample': nrm((DEC_BATCH, D)),
        'w_ada': nrm((L, D, N_MOD * D), 0.5 * D ** -0.5),
        'b_ada': nrm((L, N_MOD * D), 0.02),
        'g_mix': 1.0 + nrm((L, D), 0.05),
        'w_in': nrm((L, D, IN_WIDTH), D ** -0.5),
        'b_igate': nrm((L, MLSTM_HEADS), 0.1),
        'b_fgate': 3.0 + nrm((L, MLSTM_HEADS), 0.5),
        'q_norm_g': 1.0 + nrm((L, ATTN_HEAD_DIM), 0.05),
        'k_norm_g': 1.0 + nrm((L, ATTN_HEAD_DIM), 0.05),
        'attn_sinks': nrm((L, ATTN_HEADS), 0.5),
        'mlstm_norm_g': 1.0 + nrm((L, MLSTM_V_W), 0.05),
        'w_attn_up': nrm((L, ATTN_Q_W, D), ATTN_Q_W ** -0.5),
        'w_mlstm_up': nrm((L, MLSTM_V_W, D), MLSTM_V_W ** -0.5),
        'w_out': nrm((L, D, D), D ** -0.5),
        'g_ffn': 1.0 + nrm((L, D), 0.05),
        'w_router': nrm((L, D, N_EXPERTS), D ** -0.5),
        'b_router': nrm((L, N_EXPERTS), 0.01),
        'w_up': nrm((L, N_EXPERTS, D, 2 * D_EXPERT), D ** -0.5),
        'b_up': nrm((L, N_EXPERTS, 2 * D_EXPERT), 0.01),
        'w_down': nrm((L, N_EXPERTS, D_EXPERT, D), D_EXPERT ** -0.5),
        'b_down': nrm((L, N_EXPERTS, D), 0.01),
    }


def reference(x_prompt, x_sample, cache_k, cache_v, state_C, state_n, state_m, c_prompt, c_sample,
              w_ada, b_ada, g_mix, w_in, b_igate, b_fgate, q_norm_g, k_norm_g, attn_sinks,
              mlstm_norm_g, w_attn_up, w_mlstm_up, w_out, g_ffn, w_router, b_router,
              w_up, b_up, w_down, b_down):
    B = x_prompt.shape[0]
    zero_kv = jnp.zeros((B, WINDOW, ATTN_KV_HEADS, ATTN_HEAD_DIM), x_prompt.dtype)
    zero_C = jnp.zeros((B, MLSTM_HEADS, MLSTM_QK_DIM, MLSTM_V_DIM), jnp.float32)
    zero_n = jnp.zeros((B, MLSTM_HEADS, MLSTM_QK_DIM), jnp.float32)
    zero_m = jnp.zeros((B, MLSTM_HEADS), jnp.float32)
    hp, hs = x_prompt, x_sample
    new_p = [[] for _ in range(5)]
    new_s = [[] for _ in range(5)]
    for l in range(DEPTH):
        lw = (w_ada[l], b_ada[l], g_mix[l], w_in[l], b_igate[l], b_fgate[l], q_norm_g[l], k_norm_g[l],
              attn_sinks[l], mlstm_norm_g[l], w_attn_up[l], w_mlstm_up[l], w_out[l], g_ffn[l],
              w_router[l], b_router[l], w_up[l], b_up[l], w_down[l], b_down[l])
        hp, *st_p = _layer(hp, c_prompt, zero_kv, zero_kv, False, zero_C, zero_n, zero_m, *lw)
        hs, *st_s = _layer(hs, c_sample, cache_k[l], cache_v[l], True,
                           state_C[l], state_n[l], state_m[l], *lw)
        for lst, a in zip(new_p, st_p):
            lst.append(a)
        for lst, a in zip(new_s, st_s):
            lst.append(a)
    k_p, v_p, C_p, n_p, m_p = (jnp.stack(a) for a in new_p)
    k_s, v_s, C_s, n_s, m_s = (jnp.stack(a) for a in new_s)
    return (hp, hs, k_p, v_p, C_p, n_p, m_p, k_s, v_s, C_s, n_s, m_s)
```

```python
import functools

import numpy as np
import jax
import jax.numpy as jnp
from jax import lax
from jax.experimental import pallas as pl
from jax.experimental.pallas import tpu as pltpu

F32 = jnp.float32
BF16 = jnp.bfloat16

ATTN_HEADS = 16
ATTN_KV = 4
HD = 64
GQA = ATTN_HEADS // ATTN_KV
MH = 4
MDK = 128
MDV = 256
TOP_K = 4
N_MOD = 6
NORM_EPS = 1e-6
SWIGLU_LIMIT = 7.0
SWIGLU_ALPHA = 1.702

AQ_W = ATTN_HEADS * HD
AKV_W = ATTN_KV * HD
MQK_W = MH * MDK
MV_W = MH * MDV
GATE_PAD = 512

LANES = 128
MOE_BM = 256
MLSTM_L = 128
VMEM_LIMIT = 56 * 1024 * 1024

_SLOPES = [float(np.exp2(np.float32(-8.0 * (h + 1) / ATTN_HEADS))) for h in range(ATTN_HEADS)]


def _cparams(sem):
    return pltpu.CompilerParams(dimension_semantics=sem, vmem_limit_bytes=VMEM_LIMIT)


def _col_layout(d):
    off = {}
    o = 0
    for name, w in (("ga", d), ("gm", d), ("aq", AQ_W), ("mv", MV_W), ("mo", MV_W), ("ak", AKV_W),
                    ("av", AKV_W), ("mq", MQK_W), ("mk", MQK_W), ("gt", GATE_PAD)):
        off[name] = o
        o += w
    return off, o


def _rms(x, g):
    return x * lax.rsqrt(jnp.mean(x * x, axis=-1, keepdims=True) + NORM_EPS) * g


def _adaln_kernel(c_ref, w_ref, b_ref, o_ref):
    c = c_ref[...]
    s = (c * jax.nn.sigmoid(c)).astype(BF16)
    o_ref[...] = jnp.dot(s, w_ref[...].astype(BF16), preferred_element_type=F32) + b_ref[...]


def _adaln(c_all, w_ada, b_ada):
    r, d = c_all.shape
    w = w_ada.shape[1]
    tn = 1024
    return pl.pallas_call(
        _adaln_kernel,
        grid=(w // tn,),
        in_specs=[pl.BlockSpec((r, d), lambda j: (0, 0)),
                  pl.BlockSpec((d, tn), lambda j: (0, j)),
                  pl.BlockSpec((1, tn), lambda j: (0, j))],
        out_specs=pl.BlockSpec((r, tn), lambda j: (0, j)),
        out_shape=jax.ShapeDtypeStruct((r, w), F32),
        compiler_params=_cparams(("arbitrary",)),
        name="adaln_mod",
    )(c_all, w_ada, b_ada.reshape(1, w))


def _inproj_kernel(x_ref, g_ref, sh_ref, sc_ref, w_ref, o_ref, u_scr):
    @pl.when(pl.program_id(1) == 0)
    def _():
        y = _rms(x_ref[...], g_ref[...])
        u_scr[...] = (y * (1.0 + sc_ref[...]) + sh_ref[...]).astype(BF16)

    o_ref[...] = jnp.dot(u_scr[...], w_ref[...], preferred_element_type=F32)


def _inproj(x2d, g, mod_specs, mod_args, w_r, tm, name):
    n, d = x2d.shape
    w = w_r.shape[1]
    tn = 1024
    return pl.pallas_call(
        _inproj_kernel,
        grid=(n // tm, w // tn),
        in_specs=[pl.BlockSpec((tm, d), lambda i, j: (i, 0)),
                  pl.BlockSpec((1, d), lambda i, j: (0, 0)),
                  mod_specs[0], mod_specs[1],
                  pl.BlockSpec((d, tn), lambda i, j: (0, j))],
        out_specs=pl.BlockSpec((tm, tn), lambda i, j: (i, j)),
        out_shape=jax.ShapeDtypeStruct((n, w), F32),
        scratch_shapes=[pltpu.VMEM((tm, d), BF16)],
        compiler_params=_cparams(("arbitrary", "arbitrary")),
        name=name,
    )(x2d, g, mod_args[0], mod_args[1], w_r)


def _attn_prompt_kernel(sink_ref, q_ref, kc_ref, kp_ref, vc_ref, vp_ref, qg_ref, kg_ref, o_ref, kn_ref):
    n = pl.program_id(1)
    lq = q_ref.shape[0]
    t = lax.broadcasted_iota(jnp.int32, (lq, 2 * lq), 0)
    j = lax.broadcasted_iota(jnp.int32, (lq, 2 * lq), 1)
    dist = lq + t - j
    valid = (dist >= 0) & (dist < lq) & ((j >= lq) | (n > 0))
    distf = dist.astype(F32)
    qg = qg_ref[...]
    kg = kg_ref[...]
    for h in range(ATTN_KV):
        sl = slice(h * HD, (h + 1) * HD)
        kch = _rms(kc_ref[:, sl], kg)
        kph = _rms(kp_ref[:, sl], kg)
        kn_ref[:, sl] = kch
        kctx = jnp.concatenate([kph, kch], axis=0).astype(BF16)
        vctx = jnp.concatenate([vp_ref[:, sl], vc_ref[:, sl]], axis=0).astype(BF16)
        for g in range(GQA):
            hq = h * GQA + g
            qh = _rms(q_ref[:, hq * HD:(hq + 1) * HD], qg).astype(BF16)
            s = lax.dot_general(qh, kctx, (((1,), (1,)), ((), ())), preferred_element_type=F32) * (HD ** -0.5)
            s = jnp.where(valid, s - _SLOPES[hq] * distf, -jnp.inf)
            sink = sink_ref[hq]
            m = jnp.maximum(jnp.max(s, axis=-1, keepdims=True), sink)
            p = jnp.exp(s - m)
            den = jnp.sum(p, axis=-1, keepdims=True) + jnp.exp(sink - m)
            o = jnp.dot(p.astype(BF16), vctx, preferred_element_type=F32) / den
            o_ref[:, hq * HD:(hq + 1) * HD] = o.astype(o_ref.dtype)


def _attn_prompt(proj, off, b, s, sinks, qg, kg):
    lq = 128
    nq = s // lq
    n = b * s
    aq_b, ak_b, av_b = off["aq"] // AQ_W, off["ak"] // AKV_W, off["av"] // AKV_W
    cur = lambda bi, ni: bi * nq + ni
    prv = lambda bi, ni: bi * nq + jnp.maximum(ni - 1, 0)
    return pl.pallas_call(
        _attn_prompt_kernel,
        grid=(b, nq),
        in_specs=[pl.BlockSpec(memory_space=pltpu.SMEM),
                  pl.BlockSpec((lq, AQ_W), lambda bi, ni: (cur(bi, ni), aq_b)),
                  pl.BlockSpec((lq, AKV_W), lambda bi, ni: (cur(bi, ni), ak_b)),
                  pl.BlockSpec((lq, AKV_W), lambda bi, ni: (prv(bi, ni), ak_b)),
                  pl.BlockSpec((lq, AKV_W), lambda bi, ni: (cur(bi, ni), av_b)),
                  pl.BlockSpec((lq, AKV_W), lambda bi, ni: (prv(bi, ni), av_b)),
                  pl.BlockSpec((1, HD), lambda bi, ni: (0, 0)),
                  pl.BlockSpec((1, HD), lambda bi, ni: (0, 0))],
        out_specs=[pl.BlockSpec((lq, AQ_W), lambda bi, ni: (cur(bi, ni), 0)),
                   pl.BlockSpec((lq, AKV_W), lambda bi, ni: (cur(bi, ni), 0))],
        out_shape=[jax.ShapeDtypeStruct((n, AQ_W), BF16), jax.ShapeDtypeStruct((n, AKV_W), F32)],
        compiler_params=_cparams(("arbitrary", "arbitrary")),
        name="attn_prompt",
    )(sinks, proj, proj, proj, proj, proj, qg, kg)


def _attn_sample_kernel(q_ref, kn_ref, vn_ref, ck_ref, cv_ref, qg_ref, kg_ref, hs_ref, o_ref, ok_ref, ov_ref):
    tb, nh, _ = q_ref.shape
    win = ck_ref.shape[1]
    q = _rms(q_ref[...], qg_ref[...])
    hq_i = lax.broadcasted_iota(jnp.int32, (1, nh, HD), 1)
    qbd = jnp.concatenate([jnp.where(hq_i // GQA == kv, q, 0.0) for kv in range(ATTN_KV)], axis=-1)
    kn = kn_ref[...]
    lane = lax.broadcasted_iota(jnp.int32, (1, 1, AKV_W), 2)
    kg4 = jnp.concatenate([kg_ref[...]] * ATTN_KV, axis=-1).reshape(1, 1, AKV_W)
    rs = jnp.zeros_like(kn)
    for kv in range(ATTN_KV):
        msk = lane // HD == kv
        ms = jnp.sum(jnp.where(msk, kn * kn, 0.0), axis=-1, keepdims=True) * (1.0 / HD)
        rs = jnp.where(msk, lax.rsqrt(ms + NORM_EPS), rs)
    knn = kn * rs * kg4
    vn = vn_ref[...]
    ck = ck_ref[...]
    cv = cv_ref[...]
    ok_ref[:, 0:win - 1, :] = ck[:, 1:win, :]
    ok_ref[:, win - 1:win, :] = knn
    ov_ref[:, 0:win - 1, :] = cv[:, 1:win, :]
    ov_ref[:, win - 1:win, :] = vn
    slope = hs_ref[0][None]
    sink = hs_ref[1][None][:, :, 0:1]
    jj = lax.broadcasted_iota(jnp.int32, (1, 1, win), 2)
    s = jnp.einsum("bhc,bjc->bhj", qbd.astype(BF16), ck.astype(BF16), preferred_element_type=F32) * (HD ** -0.5)
    s = jnp.where(jj >= 1, s - slope * (win - jj).astype(F32), -jnp.inf)
    s_new = jnp.sum(qbd * knn, axis=-1, keepdims=True) * (HD ** -0.5)
    m = jnp.maximum(jnp.maximum(jnp.max(s, axis=-1, keepdims=True), s_new), sink)
    p = jnp.exp(s - m)
    p_new = jnp.exp(s_new - m)
    den = jnp.sum(p, axis=-1, keepdims=True) + p_new + jnp.exp(sink - m)
    of = jnp.einsum("bhj,bjc->bhc", p.astype(BF16), cv.astype(BF16), preferred_element_type=F32) + p_new * vn
    of = of / den
    o = jnp.zeros((tb, nh, HD), F32)
    for kv in range(ATTN_KV):
        o = o + jnp.where(hq_i // GQA == kv, of[:, :, kv * HD:(kv + 1) * HD], 0.0)
    o_ref[...] = o.astype(o_ref.dtype)


def _attn_sample(q3, kn3, vn3, ck, cv, qg, kg, hs):
    bs, nh, _ = q3.shape
    win = ck.shape[1]
    tb = 16
    return pl.pallas_call(
        _attn_sample_kernel,
        grid=(bs // tb,),
        in_specs=[pl.BlockSpec((tb, nh, HD), lambda i: (i, 0, 0)),
                  pl.BlockSpec((tb, 1, AKV_W), lambda i: (i, 0, 0)),
                  pl.BlockSpec((tb, 1, AKV_W), lambda i: (i, 0, 0)),
                  pl.BlockSpec((tb, win, AKV_W), lambda i: (i, 0, 0)),
                  pl.BlockSpec((tb, win, AKV_W), lambda i: (i, 0, 0)),
                  pl.BlockSpec((1, 1, HD), lambda i: (0, 0, 0)),
                  pl.BlockSpec((1, HD), lambda i: (0, 0)),
                  pl.BlockSpec((2, nh, LANES), lambda i: (0, 0, 0))],
        out_specs=[pl.BlockSpec((tb, nh, HD), lambda i: (i, 0, 0)),
                   pl.BlockSpec((tb, win, AKV_W), lambda i: (i, 0, 0)),
                   pl.BlockSpec((tb, win, AKV_W), lambda i: (i, 0, 0))],
        out_shape=[jax.ShapeDtypeStruct((bs, nh, HD), BF16),
                   jax.ShapeDtypeStruct((bs, win, AKV_W), F32),
                   jax.ShapeDtypeStruct((bs, win, AKV_W), F32)],
        compiler_params=_cparams(("arbitrary",)),
        name="attn_sample",
    )(q3, kn3, vn3, ck, cv, qg.reshape(1, 1, HD), kg, hs)


def _log_gates(g_pre, bi_ref, bf_ref):
    lane = lax.broadcasted_iota(jnp.int32, (1, LANES), 1)
    bias = jnp.zeros((1, LANES), F32)
    for h in range(MH):
        bias = jnp.where(lane == h, bi_ref[h], bias)
        bias = jnp.where(lane == MH + h, bf_ref[h], bias)
    pre = g_pre + bias
    logsig = jnp.minimum(pre, 0.0) - jnp.log1p(jnp.exp(-jnp.abs(pre)))
    return jnp.where(lane < MH, pre, logsig)


def _mlstm_prompt_kernel(bi_ref, bf_ref, q_ref, k_ref, v_ref, o_ref, g_ref, ng_ref,
                         hm_ref, c_ref, n_ref, m_ref, m_scr):
    ci = pl.program_id(1)
    ln = q_ref.shape[0]

    @pl.when(ci == 0)
    def _():
        c_ref[...] = jnp.zeros_like(c_ref)
        n_ref[...] = jnp.zeros_like(n_ref)
        m_scr[...] = jnp.zeros_like(m_scr)

    lf = _log_gates(g_ref[...], bi_ref, bf_ref)
    row = lax.broadcasted_iota(jnp.int32, (ln, ln), 0)
    col = lax.broadcasted_iota(jnp.int32, (ln, ln), 1)
    causal = row >= col
    tril = causal.astype(F32)
    bc = jnp.dot(tril, lf, preferred_element_type=F32, precision=lax.Precision.HIGHEST)
    lft = lf.T
    bct = bc.T
    lane = lax.broadcasted_iota(jnp.int32, (1, LANES), 1)
    m_out = jnp.zeros((1, LANES), F32)
    for h in range(MH):
        i_row = lft[h:h + 1, :]
        b_row = bct[MH + h:MH + h + 1, :]
        i_col = lf[:, h:h + 1]
        b_col = bc[:, MH + h:MH + h + 1]
        m_prev = m_scr[h][:, 0:1]
        log_d = jnp.where(causal, i_row + b_col - b_row, -jnp.inf)
        m_inter = m_prev + b_col
        m_t = jnp.maximum(m_inter, jnp.max(log_d, axis=-1, keepdims=True))
        d = jnp.exp(log_d - m_t)
        a_inter = jnp.exp(m_inter - m_t)
        q = q_ref[:, h * MDK:(h + 1) * MDK]
        k = k_ref[:, h * MDK:(h + 1) * MDK] * (MDK ** -0.5)
        qb = q.astype(BF16)
        vb = v_ref[:, h * MDV:(h + 1) * MDV].astype(BF16)
        w = lax.dot_general(qb, k.astype(BF16), (((1,), (1,)), ((), ())), preferred_element_type=F32) * d
        c_old = c_ref[0, h]
        n_old = n_ref[0, h:h + 1, :]
        num = (jnp.dot(w.astype(BF16), vb, preferred_element_type=F32)
               + jnp.dot(qb, c_old.astype(BF16), preferred_element_type=F32) * a_inter)
        den = jnp.sum(w, axis=-1, keepdims=True) + a_inter * jnp.sum(q * n_old, axis=-1, keepdims=True)
        den = jnp.maximum(jnp.abs(den), jnp.exp(-m_t))
        hh = num / den
        m_new = m_t[ln - 1:ln, :]
        b_last = b_col[ln - 1:ln, :]
        decay = jnp.exp(i_col + b_last - b_col - m_new)
        carry = jnp.exp(m_prev + b_last - m_new)
        kd = k * decay
        c_ref[0, h] = carry * c_old + jnp.dot(kd.T.astype(BF16), vb, preferred_element_type=F32)
        n_ref[0, h:h + 1, :] = carry * n_old + jnp.sum(kd, axis=0, keepdims=True)
        m_scr[h] = jnp.broadcast_to(m_new, (1, LANES))
        m_out = jnp.where(lane == h, m_new, m_out)
        hn = _rms(hh, ng_ref[:, h * MDV:(h + 1) * MDV]) * jax.nn.sigmoid(o_ref[:, h * MDV:(h + 1) * MDV])
        hm_ref[:, h * MDV:(h + 1) * MDV] = hn.astype(hm_ref.dtype)
    m_ref[0] = m_out


def _mlstm_prompt(proj, off, b, s, b_i, b_f, ng):
    ln = MLSTM_L
    nc = s // ln
    n = b * s
    mq_b, mk_b = off["mq"] // MQK_W, off["mk"] // MQK_W
    mv_b, mo_b, gt_b = off["mv"] // MV_W, off["mo"] // MV_W, off["gt"] // LANES
    rowb = lambda bi, ci: bi * nc + ci
    smem = pl.BlockSpec(memory_space=pltpu.SMEM)
    return pl.pallas_call(
        _mlstm_prompt_kernel,
        grid=(b, nc),
        in_specs=[smem, smem,
                  pl.BlockSpec((ln, MQK_W), lambda bi, ci: (rowb(bi, ci), mq_b)),
                  pl.BlockSpec((ln, MQK_W), lambda bi, ci: (rowb(bi, ci), mk_b)),
                  pl.BlockSpec((ln, MV_W), lambda bi, ci: (rowb(bi, ci), mv_b)),
                  pl.BlockSpec((ln, MV_W), lambda bi, ci: (rowb(bi, ci), mo_b)),
                  pl.BlockSpec((ln, LANES), lambda bi, ci: (rowb(bi, ci), gt_b)),
                  pl.BlockSpec((1, MV_W), lambda bi, ci: (0, 0))],
        out_specs=[pl.BlockSpec((ln, MV_W), lambda bi, ci: (rowb(bi, ci), 0)),
                   pl.BlockSpec((1, MH, MDK, MDV), lambda bi, ci: (bi, 0, 0, 0)),
                   pl.BlockSpec((1, MH, MDK), lambda bi, ci: (bi, 0, 0)),
                   pl.BlockSpec((1, 1, LANES), lambda bi, ci: (bi, 0, 0))],
        out_shape=[jax.ShapeDtypeStruct((n, MV_W), BF16),
                   jax.ShapeDtypeStruct((b, MH, MDK, MDV), F32),
                   jax.ShapeDtypeStruct((b, MH, MDK), F32),
                   jax.ShapeDtypeStruct((b, 1, LANES), F32)],
        scratch_shapes=[pltpu.VMEM((MH, 1, LANES), F32)],
        compiler_params=_cparams(("arbitrary", "arbitrary")),
        name="mlstm_prompt",
    )(b_i, b_f, proj, proj, proj, proj, proj, ng)


def _mlstm_sample_kernel(bi_ref, bf_ref, q_ref, k_ref, v_ref, o_ref, g_ref, ng_ref, c0_ref, n0_ref, m0_ref,
                         hm_ref, c_ref, n_ref, m_ref):
    tb = q_ref.shape[0]
    lf = _log_gates(g_ref[...], bi_ref, bf_ref)
    lane = lax.broadcasted_iota(jnp.int32, (1, LANES), 1)
    m_out = jnp.zeros((tb, LANES), F32)
    for h in range(MH):
        li = lf[:, h:h + 1]
        lfg = lf[:, MH + h:MH + h + 1]
        m_prev = m0_ref[:, h:h + 1]
        m_inter = m_prev + lfg
        m_t = jnp.maximum(m_inter, li)
        d = jnp.exp(li - m_t)
        a = jnp.exp(m_inter - m_t)
        q = q_ref[:, h * MDK:(h + 1) * MDK]
        k = k_ref[:, h * MDK:(h + 1) * MDK] * (MDK ** -0.5)
        v = v_ref[:, h * MDV:(h + 1) * MDV]
        n_old = n0_ref[:, h, :]
        w = jnp.sum(q * k, axis=-1, keepdims=True) * d
        den = w + a * jnp.sum(q * n_old, axis=-1, keepdims=True)
        den = jnp.maximum(jnp.abs(den), jnp.exp(-m_t))
        dk = k * d
        qt = q.T
        dkt = dk.T
        rows = []
        for b in range(tb):
            c_old = c0_ref[b, h]
            qc = jnp.sum(c_old * qt[:, b:b + 1], axis=0, keepdims=True)
            a_b = a[b:b + 1, :]
            vrow = v[b:b + 1, :]
            rows.append((w[b:b + 1, :] * vrow + qc * a_b) / den[b:b + 1, :])
            c_ref[b, h] = a_b * c_old + dkt[:, b:b + 1] * vrow
        hh = jnp.concatenate(rows, axis=0)
        n_ref[:, h, :] = a * n_old + dk
        m_out = jnp.where(lane == h, m_t, m_out)
        hn = _rms(hh, ng_ref[:, h * MDV:(h + 1) * MDV]) * jax.nn.sigmoid(o_ref[:, h * MDV:(h + 1) * MDV])
        hm_ref[:, h * MDV:(h + 1) * MDV] = hn.astype(hm_ref.dtype)
    m_ref[...] = m_out


def _mlstm_sample(proj, off, bs, b_i, b_f, ng, c0, n0, m0):
    tb = 8
    mq_b, mk_b = off["mq"] // MQK_W, off["mk"] // MQK_W
    mv_b, mo_b, gt_b = off["mv"] // MV_W, off["mo"] // MV_W, off["gt"] // LANES
    smem = pl.BlockSpec(memory_space=pltpu.SMEM)
    return pl.pallas_call(
        _mlstm_sample_kernel,
        grid=(bs // tb,),
        in_specs=[smem, smem,
                  pl.BlockSpec((tb, MQK_W), lambda i: (i, mq_b)),
                  pl.BlockSpec((tb, MQK_W), lambda i: (i, mk_b)),
                  pl.BlockSpec((tb, MV_W), lambda i: (i, mv_b)),
                  pl.BlockSpec((tb, MV_W), lambda i: (i, mo_b)),
                  pl.BlockSpec((tb, LANES), lambda i: (i, gt_b)),
                  pl.BlockSpec((1, MV_W), lambda i: (0, 0)),
                  pl.BlockSpec((tb, MH, MDK, MDV), lambda i: (i, 0, 0, 0)),
                  pl.BlockSpec((tb, MH, MDK), lambda i: (i, 0, 0)),
                  pl.BlockSpec((tb, MH), lambda i: (i, 0))],
        out_specs=[pl.BlockSpec((tb, MV_W), lambda i: (i, 0)),
                   pl.BlockSpec((tb, MH, MDK, MDV), lambda i: (i, 0, 0, 0)),
                   pl.BlockSpec((tb, MH, MDK), lambda i: (i, 0, 0)),
                   pl.BlockSpec((tb, LANES), lambda i: (i, 0))],
        out_shape=[jax.ShapeDtypeStruct((bs, MV_W), BF16),
                   jax.ShapeDtypeStruct((bs, MH, MDK, MDV), F32),
                   jax.ShapeDtypeStruct((bs, MH, MDK), F32),
                   jax.ShapeDtypeStruct((bs, LANES), F32)],
        compiler_params=_cparams(("arbitrary",)),
        name="mlstm_sample",
    )(b_i, b_f, proj, proj, proj, proj, proj, ng, c0, n0, m0)


def _postmix_kernel(n_exp, ha_ref, hm_ref, ga_ref, gm_ref, x_ref, g1_ref, sh2_ref, sc2_ref,
                    wa_ref, wm_ref, wo_ref, gf_ref, wr_ref, br_ref, x1_ref, u2_ref, ti_ref, tg_ref):
    a = jnp.dot(ha_ref[...], wa_ref[...], preferred_element_type=F32)
    m = jnp.dot(hm_ref[...], wm_ref[...], preferred_element_type=F32)
    merged = jax.nn.sigmoid(ga_ref[...]) * a + jax.nn.sigmoid(gm_ref[...]) * m
    y = jnp.dot(merged.astype(BF16), wo_ref[...], preferred_element_type=F32)
    x1 = x_ref[...] + g1_ref[...] * y
    x1_ref[...] = x1
    u2 = _rms(x1, gf_ref[...]) * (1.0 + sc2_ref[...]) + sh2_ref[...]
    u2_ref[...] = u2
    logits = jnp.dot(u2, wr_ref[...], preferred_element_type=F32, precision=lax.Precision.HIGHEST) + br_ref[...]
    lane = lax.broadcasted_iota(jnp.int32, logits.shape, 1)
    lanef = lane.astype(F32)
    work = jnp.where(lane < n_exp, logits, -jnp.inf)
    vals, idxs = [], []
    for _ in range(TOP_K):
        mx = jnp.max(work, axis=-1, keepdims=True)
        am = jnp.min(jnp.where(work == mx, lanef, float(LANES)), axis=-1, keepdims=True)
        vals.append(mx)
        idxs.append(am)
        work = jnp.where(lanef == am, -jnp.inf, work)
    es = [jnp.exp(v - vals[0]) for v in vals]
    tot = es[0] + es[1] + es[2] + es[3]
    ti = jnp.zeros(logits.shape, F32)
    tg = jnp.zeros(logits.shape, F32)
    for kk in range(TOP_K):
        ti = jnp.where(lane == kk, idxs[kk], ti)
        tg = jnp.where(lane == kk, es[kk] / tot, tg)
    ti_ref[...] = ti.astype(jnp.int32)
    tg_ref[...] = tg


def _postmix(ha, hm, proj, off, x2d, mod_specs, mod_args, wa, wm, wo, gf, wr, br, n_exp, tm, name):
    n, d = x2d.shape
    ga_b, gm_b = off["ga"] // d, off["gm"] // d
    const = lambda shape: pl.BlockSpec(shape, lambda i: (0,) * len(shape), pipeline_mode=pl.Buffered(1))
    return pl.pallas_call(
        functools.partial(_postmix_kernel, n_exp),
        grid=(n // tm,),
        in_specs=[pl.BlockSpec((tm, AQ_W), lambda i: (i, 0)),
                  pl.BlockSpec((tm, MV_W), lambda i: (i, 0)),
                  pl.BlockSpec((tm, d), lambda i: (i, ga_b)),
                  pl.BlockSpec((tm, d), lambda i: (i, gm_b)),
                  pl.BlockSpec((tm, d), lambda i: (i, 0)),
                  mod_specs[0], mod_specs[1], mod_specs[2],
                  const((AQ_W, d)), const((MV_W, d)), const((d, d)),
                  const((1, d)), const((d, LANES)), const((1, LANES))],
        out_specs=[pl.BlockSpec((tm, d), lambda i: (i, 0)),
                   pl.BlockSpec((tm, d), lambda i: (i, 0)),
                   pl.BlockSpec((tm, LANES), lambda i: (i, 0)),
                   pl.BlockSpec((tm, LANES), lambda i: (i, 0))],
        out_shape=[jax.ShapeDtypeStruct((n, d), F32), jax.ShapeDtypeStruct((n, d), F32),
                   jax.ShapeDtypeStruct((n, LANES), jnp.int32), jax.ShapeDtypeStruct((n, LANES), F32)],
        compiler_params=_cparams(("arbitrary",)),
        name=name,
    )(ha, hm, proj, proj, x2d, mod_args[0], mod_args[1], mod_args[2], wa, wm, wo, gf, wr, br)


def _moe_kernel(n_prompt, be_ref, nu_ref, idc_ref, idn_ref, up_ref, us_ref, wg_ref, wl_ref, wd_ref,
                bg_ref, bl_ref, bd_ref, o_ref, xbuf, xb, sem):
    r = pl.program_id(0)
    c = pl.program_id(1)
    nb = pl.num_programs(0)
    bm = xb.shape[0]

    def row_copy(tok, slot, i):
        @pl.when(tok < n_prompt)
        def _():
            pltpu.make_async_copy(up_ref.at[pl.ds(tok, 1)], xbuf.at[slot, pl.ds(i, 1)], sem.at[slot]).start()

        @pl.when(tok >= n_prompt)
        def _():
            pltpu.make_async_copy(us_ref.at[pl.ds(tok - n_prompt, 1)], xbuf.at[slot, pl.ds(i, 1)],
                                  sem.at[slot]).start()

    def issue(idx_ref, slot):
        def body(i, carry):
            row_copy(idx_ref[0, i], slot, i)
            return carry
        lax.fori_loop(0, bm, body, 0)

    def drain(slot):
        def body(i, carry):
            pltpu.make_async_copy(up_ref.at[pl.ds(0, 1)], xbuf.at[slot, pl.ds(i, 1)], sem.at[slot]).wait()
            return carry
        lax.fori_loop(0, bm, body, 0)

    slot = r % 2

    @pl.when(c == 0)
    def _():
        @pl.when(r == 0)
        def _():
            issue(idc_ref, 0)

        @pl.when(r + 1 < nb)
        def _():
            issue(idn_ref, 1 - slot)

        drain(slot)
        xb[...] = xbuf[slot].astype(BF16)

    used = r < nu_ref[0]

    @pl.when(used)
    def _():
        x = xb[...]
        g = jnp.dot(x, wg_ref[...], preferred_element_type=F32) + bg_ref[0]
        l = jnp.dot(x, wl_ref[...], preferred_element_type=F32) + bl_ref[0]
        g = jnp.minimum(g, SWIGLU_LIMIT)
        l = jnp.clip(l, -SWIGLU_LIMIT, SWIGLU_LIMIT)
        act = g * jax.nn.sigmoid(SWIGLU_ALPHA * g) * (l + 1.0)
        y = jnp.dot(act.astype(BF16), wd_ref[...], preferred_element_type=F32)

        @pl.when(c == 0)
        def _():
            o_ref[...] = y + bd_ref[...]

        @pl.when(c != 0)
        def _():
            o_ref[...] += y

    @pl.when(jnp.logical_not(used) & (c == 0))
    def _():
        o_ref[...] = jnp.zeros_like(o_ref)


def _moe(u2_p, u2_s, blk_e, n_used, row_tok, wg, wl, wd, bg, bl, bd):
    n_blocks = blk_e.shape[0]
    bm = MOE_BM
    n_exp, d, de = wg.shape
    hc = de // 2
    n_prompt = u2_p.shape[0]

    def half(r, c, be, nu):
        serp = jnp.where(r % 2 == 0, c, 1 - c)
        last = jnp.maximum(nu[0] - 1, 0)
        return jnp.where(r < nu[0], serp, jnp.where(last % 2 == 0, 1, 0))

    grid_spec = pltpu.PrefetchScalarGridSpec(
        num_scalar_prefetch=2,
        grid=(n_blocks, 2),
        in_specs=[
            pl.BlockSpec((None, 1, bm), lambda r, c, be, nu: (r, 0, 0), memory_space=pltpu.SMEM),
            pl.BlockSpec((None, 1, bm), lambda r, c, be, nu: (jnp.minimum(r + 1, n_blocks - 1), 0, 0),
                         memory_space=pltpu.SMEM),
            pl.BlockSpec(memory_space=pl.ANY),
            pl.BlockSpec(memory_space=pl.ANY),
            pl.BlockSpec((None, d, hc), lambda r, c, be, nu: (be[r], 0, half(r, c, be, nu))),
            pl.BlockSpec((None, d, hc), lambda r, c, be, nu: (be[r], 0, half(r, c, be, nu))),
            pl.BlockSpec((None, hc, d), lambda r, c, be, nu: (be[r], half(r, c, be, nu), 0)),
            pl.BlockSpec((None, 1, 1, hc), lambda r, c, be, nu: (be[r], half(r, c, be, nu), 0, 0)),
            pl.BlockSpec((None, 1, 1, hc), lambda r, c, be, nu: (be[r], half(r, c, be, nu), 0, 0)),
            pl.BlockSpec((None, 1, d), lambda r, c, be, nu: (be[r], 0, 0)),
        ],
        out_specs=pl.BlockSpec((bm, d), lambda r, c, be, nu: (r, 0)),
        scratch_shapes=[pltpu.VMEM((2, bm, d), F32), pltpu.VMEM((bm, d), BF16), pltpu.SemaphoreType.DMA((2,))],
    )
    return pl.pallas_call(
        functools.partial(_moe_kernel, n_prompt),
        grid_spec=grid_spec,
        out_shape=jax.ShapeDtypeStruct((n_blocks * bm, d), F32),
        compiler_params=_cparams(("arbitrary", "arbitrary")),
        name="moe_experts",
    )(blk_e, n_used, row_tok.reshape(n_blocks, 1, bm), row_tok.reshape(n_blocks, 1, bm), u2_p, u2_s,
      wg, wl, wd, bg.reshape(n_exp, 2, 1, hc), bl.reshape(n_exp, 2, 1, hc), bd.reshape(n_exp, 1, d))


def _combine_kernel(pc_ref, pn_ref, yb_ref, x1_ref, g2_ref, tg_ref, o_ref, buf, sem):
    s = pl.program_id(0)
    ns = pl.num_programs(0)
    tm = x1_ref.shape[0]

    def issue(p_ref, slot):
        def body(t, carry):
            for kk in range(TOP_K):
                pltpu.make_async_copy(yb_ref.at[pl.ds(p_ref[0, t * TOP_K + kk], 1)],
                                      buf.at[slot, kk, pl.ds(t, 1)], sem.at[slot]).start()
            return carry
        lax.fori_loop(0, tm, body, 0)

    def drain(slot):
        def body(t, carry):
            for kk in range(TOP_K):
                pltpu.make_async_copy(yb_ref.at[pl.ds(0, 1)], buf.at[slot, kk, pl.ds(t, 1)], sem.at[slot]).wait()
            return carry
        lax.fori_loop(0, tm, body, 0)

    slot = s % 2

    @pl.when(s == 0)
    def _():
        issue(pc_ref, 0)

    @pl.when(s + 1 < ns)
    def _():
        issue(pn_ref, 1 - slot)

    drain(slot)
    tg = tg_ref[...]
    acc = tg[:, 0:1] * buf[slot, 0]
    for kk in range(1, TOP_K):
        acc = acc + tg[:, kk:kk + 1] * buf[slot, kk]
    o_ref[...] = x1_ref[...] + g2_ref[...] * acc


def _combine(pos, yb, x1, g2_spec, g2_arg, tg, tm, name):
    n, d = x1.shape
    ns = n // tm
    pos3 = pos.reshape(ns, 1, tm * TOP_K)
    return pl.pallas_call(
        _combine_kernel,
        grid=(ns,),
        in_specs=[pl.BlockSpec((None, 1, tm * TOP_K), lambda s: (s, 0, 0), memory_space=pltpu.SMEM),
                  pl.BlockSpec((None, 1, tm * TOP_K), lambda s: (jnp.minimum(s + 1, ns - 1), 0, 0),
                               memory_space=pltpu.SMEM),
                  pl.BlockSpec(memory_space=pl.ANY),
                  pl.BlockSpec((tm, d), lambda s: (s, 0)),
                  g2_spec,
                  pl.BlockSpec((tm, LANES), lambda s: (s, 0))],
        out_specs=pl.BlockSpec((tm, d), lambda s: (s, 0)),
        out_shape=jax.ShapeDtypeStruct((n, d), F32),
        scratch_shapes=[pltpu.VMEM((2, TOP_K, tm, d), F32), pltpu.SemaphoreType.DMA((2,))],
        compiler_params=_cparams(("arbitrary",)),
        name=name,
    )(pos3, pos3, yb, x1, g2_arg, tg)


def _routing(top_e, n_exp, n_blocks, bm):
    nk = top_e.size
    flat_e = top_e.reshape(nk)
    oh = (flat_e[:, None] == jnp.arange(n_exp, dtype=jnp.int32)[None, :]).astype(jnp.int32)
    csum = jnp.cumsum(oh, axis=0)
    rank = jnp.sum((csum - oh) * oh, axis=1)
    counts = csum[-1]
    nblk = (counts + bm - 1) // bm
    blk_end = jnp.cumsum(nblk)
    blk_start = blk_end - nblk
    pos = jnp.sum(oh * blk_start[None, :], axis=1) * bm + rank
    n_used = blk_end[-1]
    bidx = jnp.arange(n_blocks, dtype=jnp.int32)
    blk_e = jnp.minimum(jnp.sum((bidx[:, None] >= blk_end[None, :]).astype(jnp.int32), axis=1), n_exp - 1)
    last_e = jnp.sum(jnp.where(bidx == n_used - 1, blk_e, 0))
    blk_e = jnp.where(bidx < n_used, blk_e, last_e).astype(jnp.int32)
    tok = jnp.arange(nk, dtype=jnp.int32) // TOP_K
    row_tok = jnp.zeros((n_blocks * bm,), jnp.int32).at[pos].set(tok, unique_indices=True)
    return pos.astype(jnp.int32), blk_e, n_used.reshape(1).astype(jnp.int32), row_tok


def kernel(x_prompt, x_sample, cache_k, cache_v, state_C, state_n, state_m, c_prompt, c_sample, w_ada, b_ada,
           g_mix, w_in, b_igate, b_fgate, q_norm_g, k_norm_g, attn_sinks, mlstm_norm_g, w_attn_up, w_mlstm_up,
           w_out, g_ffn, w_router, b_router, w_up, b_up, w_down, b_down):
    bp, sp, d = x_prompt.shape
    bs = x_sample.shape[0]
    assert x_sample.shape[1] == 1 and w_ada.shape[0] == 1
    n_p = bp * sp
    win = cache_k.shape[2]
    n_exp = w_router.shape[2]
    de = w_down.shape[2]
    off, _ = _col_layout(d)

    w = w_in[0]
    cuts = np.cumsum([AQ_W, AKV_W, AKV_W, MQK_W, MQK_W, MV_W, MV_W, MH, MH, d, d])[:-1].tolist()
    aq, ak, av, mq, mk, mv, mo, mi, mf, ga, gm = jnp.split(w, cuts, axis=1)
    gpad = jnp.zeros((d, GATE_PAD - 2 * MH), w.dtype)
    w_r = jnp.concatenate([ga, gm, aq, mv, mo, ak, av, mq, mk, mi, mf, gpad], axis=1).astype(BF16)
    wa = w_attn_up[0].astype(BF16)
    wm = w_mlstm_up[0].astype(BF16)
    wo = w_out[0].astype(BF16)
    wr = jnp.pad(w_router[0], ((0, 0), (0, LANES - n_exp)))
    br = jnp.pad(b_router[0], (0, LANES - n_exp)).reshape(1, LANES)
    wup = w_up[0].reshape(n_exp, d, de, 2)
    wg = wup[..., 0].astype(BF16)
    wl = wup[..., 1].astype(BF16)
    wd = w_down[0].astype(BF16)
    bup = b_up[0].reshape(n_exp, de, 2)
    bg, bl = bup[..., 0], bup[..., 1]
    bd = b_down[0]
    gmix = g_mix[0].reshape(1, d)
    gffn = g_ffn[0].reshape(1, d)
    qg = q_norm_g[0].reshape(1, HD)
    kg = k_norm_g[0].reshape(1, HD)
    ng = mlstm_norm_g[0].reshape(1, MV_W)
    sinks = attn_sinks[0]
    hs = jnp.stack([jnp.broadcast_to(jnp.asarray(_SLOPES, F32)[:, None], (ATTN_HEADS, LANES)),
                    jnp.broadcast_to(sinks[:, None], (ATTN_HEADS, LANES))])

    mod = _adaln(jnp.concatenate([c_prompt, c_sample], axis=0), w_ada[0], b_ada[0])
    mod_p = mod[:bp].reshape(bp, 1, N_MOD * d)
    mod_s = mod[bp:]

    xp = x_prompt.reshape(n_p, d)
    tm_in = min(512, sp)
    tps = sp // tm_in
    pspec = lambda t, kk, nargs: pl.BlockSpec(
        (None, 1, d), (lambda i, j: (i // t, 0, kk)) if nargs == 2 else (lambda i: (i // t, 0, kk)))
    proj_p = _inproj(xp, gmix, (pspec(tps, 0, 2), pspec(tps, 1, 2)), (mod_p, mod_p), w_r, tm_in, "inproj_prompt")
    ha_p, kn_p = _attn_prompt(proj_p, off, bp, sp, sinks, qg, kg)
    hm_p, c_p, n_state_p, m_p = _mlstm_prompt(proj_p, off, bp, sp, b_igate[0], b_fgate[0], ng)
    tm_pm = min(256, sp)
    tpp = sp // tm_pm
    x1_p, u2_p, ti_p, tg_p = _postmix(
        ha_p, hm_p, proj_p, off, xp, (pspec(tpp, 2, 1), pspec(tpp, 3, 1), pspec(tpp, 4, 1)),
        (mod_p, mod_p, mod_p), wa, wm, wo, gffn, wr, br, n_exp, tm_pm, "postmix_prompt")

    xs = x_sample.reshape(bs, d)
    sspec2 = lambda kk: pl.BlockSpec((bs, d), lambda i, j: (0, kk))
    sspec1 = lambda kk: pl.BlockSpec((bs, d), lambda i: (0, kk))
    proj_s = _inproj(xs, gmix, (sspec2(0), sspec2(1)), (mod_s, mod_s), w_r, bs, "inproj_sample")
    q3 = proj_s[:, off["aq"]:off["aq"] + AQ_W].reshape(bs, ATTN_HEADS, HD)
    kn3 = proj_s[:, off["ak"]:off["ak"] + AKV_W].reshape(bs, 1, AKV_W)
    vn3 = proj_s[:, off["av"]:off["av"] + AKV_W].reshape(bs, 1, AKV_W)
    ha_s3, k_s, v_s = _attn_sample(q3, kn3, vn3, cache_k[0].reshape(bs, win, AKV_W),
                                   cache_v[0].reshape(bs, win, AKV_W), qg, kg, hs)
    hm_s, c_s, n_state_s, m_s = _mlstm_sample(proj_s, off, bs, b_igate[0], b_fgate[0], ng,
                                              state_C[0], state_n[0], state_m[0])
    x1_s, u2_s, ti_s, tg_s = _postmix(
        ha_s3.reshape(bs, AQ_W), hm_s, proj_s, off, xs, (sspec1(2), sspec1(3), sspec1(4)),
        (mod_s, mod_s, mod_s), wa, wm, wo, gffn, wr, br, n_exp, bs, "postmix_sample")

    bm = MOE_BM
    nk = (n_p + bs) * TOP_K
    n_blocks = -(-nk // bm) + n_exp
    top_e = jnp.concatenate([ti_p[:, :TOP_K], ti_s[:, :TOP_K]], axis=0)
    pos, blk_e, n_used, row_tok = _routing(top_e, n_exp, n_blocks, bm)
    yb = _moe(u2_p, u2_s, blk_e, n_used, row_tok, wg, wl, wd, bg, bl, bd)
    y_p = _combine(pos[:n_p * TOP_K], yb, x1_p, pspec(tpp, 5, 1), mod_p, tg_p, tm_pm, "combine_prompt")
    y_s = _combine(pos[n_p * TOP_K:], yb, x1_s, sspec1(5), mod_s, tg_s, bs, "combine_sample")

    kvshape = (1, bp, win, ATTN_KV, HD)
    k_p = kn_p.reshape(bp, sp, AKV_W)[:, sp - win:].reshape(kvshape)
    v_p = proj_p[:, off["av"]:off["av"] + AKV_W].reshape(bp, sp, AKV_W)[:, sp - win:].reshape(kvshape)
    return (y_p.reshape(bp, sp, d), y_s.reshape(bs, 1, d),
            k_p, v_p, c_p[None], n_state_p[None], m_p[:, 0, :MH][None],
            k_s.reshape(1, bs, win, ATTN_KV, HD), v_s.reshape(1, bs, win, ATTN_KV, HD),
            c_s[None], n_state_s[None], m_s[:, :MH][None])
```

```python
import functools

import numpy as np
import jax
import jax.numpy as jnp
from jax import lax
from jax.experimental import pallas as pl
from jax.experimental.pallas import tpu as pltpu

F32 = jnp.float32
BF16 = jnp.bfloat16

ATTN_HEADS = 16
ATTN_KV = 4
HD = 64
GQA = ATTN_HEADS // ATTN_KV
MH = 4
MDK = 128
MDV = 256
TOP_K = 4
N_MOD = 6
NORM_EPS = 1e-6
SWIGLU_LIMIT = 7.0
SWIGLU_ALPHA = 1.702

AQ_W = ATTN_HEADS * HD
AKV_W = ATTN_KV * HD
MQK_W = MH * MDK
MV_W = MH * MDV
GATE_PAD = 512

LANES = 128
MOE_SUB = 256
MOE_NSUB = 5
MOE_HC = 256
MLSTM_L = 128
VMEM_LIMIT = 56 * 1024 * 1024

_SLOPES = [float(np.exp2(np.float32(-8.0 * (h + 1) / ATTN_HEADS))) for h in range(ATTN_HEADS)]


def _cparams(sem):
    return pltpu.CompilerParams(dimension_semantics=sem, vmem_limit_bytes=VMEM_LIMIT)


def _col_layout(d):
    off = {}
    o = 0
    for name, w in (("ga", d), ("gm", d), ("aq", AQ_W), ("mv", MV_W), ("mo", MV_W), ("ak", AKV_W),
                    ("av", AKV_W), ("mq", MQK_W), ("mk", MQK_W), ("gt", GATE_PAD)):
        off[name] = o
        o += w
    return off, o


def _rms(x, g):
    return x * lax.rsqrt(jnp.mean(x * x, axis=-1, keepdims=True) + NORM_EPS) * g


def _adaln_kernel(c_ref, w_ref, b_ref, o_ref):
    c = c_ref[...]
    s = (c * jax.nn.sigmoid(c)).astype(BF16)
    o_ref[...] = jnp.dot(s, w_ref[...].astype(BF16), preferred_element_type=F32) + b_ref[...]


def _adaln(c_all, w_ada, b_ada):
    r, d = c_all.shape
    w = w_ada.shape[1]
    tn = 1024
    return pl.pallas_call(
        _adaln_kernel,
        grid=(w // tn,),
        in_specs=[pl.BlockSpec((r, d), lambda j: (0, 0)),
                  pl.BlockSpec((d, tn), lambda j: (0, j)),
                  pl.BlockSpec((1, tn), lambda j: (0, j))],
        out_specs=pl.BlockSpec((r, tn), lambda j: (0, j)),
        out_shape=jax.ShapeDtypeStruct((r, w), F32),
        compiler_params=_cparams(("arbitrary",)),
        name="adaln_mod",
    )(c_all, w_ada, b_ada.reshape(1, w))


def _inproj_kernel(x_ref, g_ref, sh_ref, sc_ref, w_ref, o_ref, u_scr):
    @pl.when(pl.program_id(1) == 0)
    def _():
        y = _rms(x_ref[...], g_ref[...])
        u_scr[...] = (y * (1.0 + sc_ref[...]) + sh_ref[...]).astype(BF16)

    o_ref[...] = jnp.dot(u_scr[...], w_ref[...], preferred_element_type=F32)


def _inproj(x2d, g, mod_specs, mod_args, w_r, tm, name):
    n, d = x2d.shape
    w = w_r.shape[1]
    tn = 1024
    return pl.pallas_call(
        _inproj_kernel,
        grid=(n // tm, w // tn),
        in_specs=[pl.BlockSpec((tm, d), lambda i, j: (i, 0)),
                  pl.BlockSpec((1, d), lambda i, j: (0, 0)),
                  mod_specs[0], mod_specs[1],
                  pl.BlockSpec((d, tn), lambda i, j: (0, j))],
        out_specs=pl.BlockSpec((tm, tn), lambda i, j: (i, j)),
        out_shape=jax.ShapeDtypeStruct((n, w), F32),
        scratch_shapes=[pltpu.VMEM((tm, d), BF16)],
        compiler_params=_cparams(("arbitrary", "arbitrary")),
        name=name,
    )(x2d, g, mod_args[0], mod_args[1], w_r)


def _attn_prompt_kernel(sink_ref, q_ref, kc_ref, kp_ref, vc_ref, vp_ref, qg_ref, kg_ref, o_ref, kn_ref):
    n = pl.program_id(1)
    lq = q_ref.shape[0]
    t = lax.broadcasted_iota(jnp.int32, (lq, 2 * lq), 0)
    j = lax.broadcasted_iota(jnp.int32, (lq, 2 * lq), 1)
    dist = lq + t - j
    valid = (dist >= 0) & (dist < lq) & ((j >= lq) | (n > 0))
    distf = dist.astype(F32)
    qg = qg_ref[...]
    kg = kg_ref[...]
    for h in range(ATTN_KV):
        sl = slice(h * HD, (h + 1) * HD)
        kch = _rms(kc_ref[:, sl], kg)
        kph = _rms(kp_ref[:, sl], kg)
        kn_ref[:, sl] = kch
        kctx = jnp.concatenate([kph, kch], axis=0).astype(BF16)
        vctx = jnp.concatenate([vp_ref[:, sl], vc_ref[:, sl]], axis=0).astype(BF16)
        for g in range(GQA):
            hq = h * GQA + g
            qh = _rms(q_ref[:, hq * HD:(hq + 1) * HD], qg).astype(BF16)
            s = lax.dot_general(qh, kctx, (((1,), (1,)), ((), ())), preferred_element_type=F32) * (HD ** -0.5)
            s = jnp.where(valid, s - _SLOPES[hq] * distf, -jnp.inf)
            sink = sink_ref[hq]
            m = jnp.maximum(jnp.max(s, axis=-1, keepdims=True), sink)
            p = jnp.exp(s - m)
            den = jnp.sum(p, axis=-1, keepdims=True) + jnp.exp(sink - m)
            o = jnp.dot(p.astype(BF16), vctx, preferred_element_type=F32) / den
            o_ref[:, hq * HD:(hq + 1) * HD] = o.astype(o_ref.dtype)


def _attn_prompt(proj, off, b, s, sinks, qg, kg):
    lq = 128
    nq = s // lq
    n = b * s
    aq_b, ak_b, av_b = off["aq"] // AQ_W, off["ak"] // AKV_W, off["av"] // AKV_W
    cur = lambda bi, ni: bi * nq + ni
    prv = lambda bi, ni: bi * nq + jnp.maximum(ni - 1, 0)
    return pl.pallas_call(
        _attn_prompt_kernel,
        grid=(b, nq),
        in_specs=[pl.BlockSpec(memory_space=pltpu.SMEM),
                  pl.BlockSpec((lq, AQ_W), lambda bi, ni: (cur(bi, ni), aq_b)),
                  pl.BlockSpec((lq, AKV_W), lambda bi, ni: (cur(bi, ni), ak_b)),
                  pl.BlockSpec((lq, AKV_W), lambda bi, ni: (prv(bi, ni), ak_b)),
                  pl.BlockSpec((lq, AKV_W), lambda bi, ni: (cur(bi, ni), av_b)),
                  pl.BlockSpec((lq, AKV_W), lambda bi, ni: (prv(bi, ni), av_b)),
                  pl.BlockSpec((1, HD), lambda bi, ni: (0, 0)),
                  pl.BlockSpec((1, HD), lambda bi, ni: (0, 0))],
        out_specs=[pl.BlockSpec((lq, AQ_W), lambda bi, ni: (cur(bi, ni), 0)),
                   pl.BlockSpec((lq, AKV_W), lambda bi, ni: (cur(bi, ni), 0))],
        out_shape=[jax.ShapeDtypeStruct((n, AQ_W), BF16), jax.ShapeDtypeStruct((n, AKV_W), F32)],
        compiler_params=_cparams(("arbitrary", "arbitrary")),
        name="attn_prompt",
    )(sinks, proj, proj, proj, proj, proj, qg, kg)


def _attn_sample_kernel(q_ref, kn_ref, vn_ref, ck_ref, cv_ref, qg_ref, kg_ref, hs_ref, o_ref, ok_ref, ov_ref):
    tb, nh, _ = q_ref.shape
    win = ck_ref.shape[1]
    q = _rms(q_ref[...], qg_ref[...])
    hq_i = lax.broadcasted_iota(jnp.int32, (1, nh, HD), 1)
    qbd = jnp.concatenate([jnp.where(hq_i // GQA == kv, q, 0.0) for kv in range(ATTN_KV)], axis=-1)
    kn = kn_ref[...]
    lane = lax.broadcasted_iota(jnp.int32, (1, 1, AKV_W), 2)
    kg4 = jnp.concatenate([kg_ref[...]] * ATTN_KV, axis=-1).reshape(1, 1, AKV_W)
    rs = jnp.zeros_like(kn)
    for kv in range(ATTN_KV):
        msk = lane // HD == kv
        ms = jnp.sum(jnp.where(msk, kn * kn, 0.0), axis=-1, keepdims=True) * (1.0 / HD)
        rs = jnp.where(msk, lax.rsqrt(ms + NORM_EPS), rs)
    knn = kn * rs * kg4
    vn = vn_ref[...]
    ck = ck_ref[...]
    cv = cv_ref[...]
    ok_ref[:, 0:win - 1, :] = ck[:, 1:win, :]
    ok_ref[:, win - 1:win, :] = knn
    ov_ref[:, 0:win - 1, :] = cv[:, 1:win, :]
    ov_ref[:, win - 1:win, :] = vn
    slope = hs_ref[0][None]
    sink = hs_ref[1][None][:, :, 0:1]
    jj = lax.broadcasted_iota(jnp.int32, (1, 1, win), 2)
    s = jnp.einsum("bhc,bjc->bhj", qbd.astype(BF16), ck.astype(BF16), preferred_element_type=F32) * (HD ** -0.5)
    s = jnp.where(jj >= 1, s - slope * (win - jj).astype(F32), -jnp.inf)
    s_new = jnp.sum(qbd * knn, axis=-1, keepdims=True) * (HD ** -0.5)
    m = jnp.maximum(jnp.maximum(jnp.max(s, axis=-1, keepdims=True), s_new), sink)
    p = jnp.exp(s - m)
    p_new = jnp.exp(s_new - m)
    den = jnp.sum(p, axis=-1, keepdims=True) + p_new + jnp.exp(sink - m)
    of = jnp.einsum("bhj,bjc->bhc", p.astype(BF16), cv.astype(BF16), preferred_element_type=F32) + p_new * vn
    of = of / den
    o = jnp.zeros((tb, nh, HD), F32)
    for kv in range(ATTN_KV):
        o = o + jnp.where(hq_i // GQA == kv, of[:, :, kv * HD:(kv + 1) * HD], 0.0)
    o_ref[...] = o.astype(o_ref.dtype)


def _attn_sample(q3, kn3, vn3, ck, cv, qg, kg, hs):
    bs, nh, _ = q3.shape
    win = ck.shape[1]
    tb = 16
    return pl.pallas_call(
        _attn_sample_kernel,
        grid=(bs // tb,),
        in_specs=[pl.BlockSpec((tb, nh, HD), lambda i: (i, 0, 0)),
                  pl.BlockSpec((tb, 1, AKV_W), lambda i: (i, 0, 0)),
                  pl.BlockSpec((tb, 1, AKV_W), lambda i: (i, 0, 0)),
                  pl.BlockSpec((tb, win, AKV_W), lambda i: (i, 0, 0)),
                  pl.BlockSpec((tb, win, AKV_W), lambda i: (i, 0, 0)),
                  pl.BlockSpec((1, 1, HD), lambda i: (0, 0, 0)),
                  pl.BlockSpec((1, HD), lambda i: (0, 0)),
                  pl.BlockSpec((2, nh, LANES), lambda i: (0, 0, 0))],
        out_specs=[pl.BlockSpec((tb, nh, HD), lambda i: (i, 0, 0)),
                   pl.BlockSpec((tb, win, AKV_W), lambda i: (i, 0, 0)),
                   pl.BlockSpec((tb, win, AKV_W), lambda i: (i, 0, 0))],
        out_shape=[jax.ShapeDtypeStruct((bs, nh, HD), BF16),
                   jax.ShapeDtypeStruct((bs, win, AKV_W), F32),
                   jax.ShapeDtypeStruct((bs, win, AKV_W), F32)],
        compiler_params=_cparams(("arbitrary",)),
        name="attn_sample",
    )(q3, kn3, vn3, ck, cv, qg.reshape(1, 1, HD), kg, hs)


def _log_gates(g_pre, bi_ref, bf_ref):
    lane = lax.broadcasted_iota(jnp.int32, (1, LANES), 1)
    bias = jnp.zeros((1, LANES), F32)
    for h in range(MH):
        bias = jnp.where(lane == h, bi_ref[h], bias)
        bias = jnp.where(lane == MH + h, bf_ref[h], bias)
    pre = g_pre + bias
    logsig = jnp.minimum(pre, 0.0) - jnp.log1p(jnp.exp(-jnp.abs(pre)))
    return jnp.where(lane < MH, pre, logsig)


def _mlstm_prompt_kernel(bi_ref, bf_ref, q_ref, k_ref, v_ref, o_ref, g_ref, ng_ref,
                         hm_ref, c_ref, n_ref, m_ref, m_scr):
    ci = pl.program_id(1)
    ln = q_ref.shape[0]

    @pl.when(ci == 0)
    def _():
        c_ref[...] = jnp.zeros_like(c_ref)
        n_ref[...] = jnp.zeros_like(n_ref)
        m_scr[...] = jnp.zeros_like(m_scr)

    lf = _log_gates(g_ref[...], bi_ref, bf_ref)
    row = lax.broadcasted_iota(jnp.int32, (ln, ln), 0)
    col = lax.broadcasted_iota(jnp.int32, (ln, ln), 1)
    causal = row >= col
    tril = causal.astype(F32)
    bc = jnp.dot(tril, lf, preferred_element_type=F32, precision=lax.Precision.HIGHEST)
    lft = lf.T
    bct = bc.T
    lane = lax.broadcasted_iota(jnp.int32, (1, LANES), 1)
    m_out = jnp.zeros((1, LANES), F32)
    for h in range(MH):
        i_row = lft[h:h + 1, :]
        b_row = bct[MH + h:MH + h + 1, :]
        i_col = lf[:, h:h + 1]
        b_col = bc[:, MH + h:MH + h + 1]
        m_prev = m_scr[h][:, 0:1]
        log_d = jnp.where(causal, i_row + b_col - b_row, -jnp.inf)
        m_inter = m_prev + b_col
        m_t = jnp.maximum(m_inter, jnp.max(log_d, axis=-1, keepdims=True))
        d = jnp.exp(log_d - m_t)
        a_inter = jnp.exp(m_inter - m_t)
        q = q_ref[:, h * MDK:(h + 1) * MDK]
        k = k_ref[:, h * MDK:(h + 1) * MDK] * (MDK ** -0.5)
        qb = q.astype(BF16)
        vb = v_ref[:, h * MDV:(h + 1) * MDV].astype(BF16)
        w = lax.dot_general(qb, k.astype(BF16), (((1,), (1,)), ((), ())), preferred_element_type=F32) * d
        c_old = c_ref[0, h]
        n_old = n_ref[0, h:h + 1, :]
        num = (jnp.dot(w.astype(BF16), vb, preferred_element_type=F32)
               + jnp.dot(qb, c_old.astype(BF16), preferred_element_type=F32) * a_inter)
        den = jnp.sum(w, axis=-1, keepdims=True) + a_inter * jnp.sum(q * n_old, axis=-1, keepdims=True)
        den = jnp.maximum(jnp.abs(den), jnp.exp(-m_t))
        hh = num / den
        m_new = m_t[ln - 1:ln, :]
        b_last = b_col[ln - 1:ln, :]
        decay = jnp.exp(i_col + b_last - b_col - m_new)
        carry = jnp.exp(m_prev + b_last - m_new)
        kd = k * decay
        c_ref[0, h] = carry * c_old + jnp.dot(kd.T.astype(BF16), vb, preferred_element_type=F32)
        n_ref[0, h:h + 1, :] = carry * n_old + jnp.sum(kd, axis=0, keepdims=True)
        m_scr[h] = jnp.broadcast_to(m_new, (1, LANES))
        m_out = jnp.where(lane == h, m_new, m_out)
        hn = _rms(hh, ng_ref[:, h * MDV:(h + 1) * MDV]) * jax.nn.sigmoid(o_ref[:, h * MDV:(h + 1) * MDV])
        hm_ref[:, h * MDV:(h + 1) * MDV] = hn.astype(hm_ref.dtype)
    m_ref[0] = m_out


def _mlstm_prompt(proj, off, b, s, b_i, b_f, ng):
    ln = MLSTM_L
    nc = s // ln
    n = b * s
    mq_b, mk_b = off["mq"] // MQK_W, off["mk"] // MQK_W
    mv_b, mo_b, gt_b = off["mv"] // MV_W, off["mo"] // MV_W, off["gt"] // LANES
    rowb = lambda bi, ci: bi * nc + ci
    smem = pl.BlockSpec(memory_space=pltpu.SMEM)
    return pl.pallas_call(
        _mlstm_prompt_kernel,
        grid=(b, nc),
        in_specs=[smem, smem,
                  pl.BlockSpec((ln, MQK_W), lambda bi, ci: (rowb(bi, ci), mq_b)),
                  pl.BlockSpec((ln, MQK_W), lambda bi, ci: (rowb(bi, ci), mk_b)),
                  pl.BlockSpec((ln, MV_W), lambda bi, ci: (rowb(bi, ci), mv_b)),
                  pl.BlockSpec((ln, MV_W), lambda bi, ci: (rowb(bi, ci), mo_b)),
                  pl.BlockSpec((ln, LANES), lambda bi, ci: (rowb(bi, ci), gt_b)),
                  pl.BlockSpec((1, MV_W), lambda bi, ci: (0, 0))],
        out_specs=[pl.BlockSpec((ln, MV_W), lambda bi, ci: (rowb(bi, ci), 0)),
                   pl.BlockSpec((1, MH, MDK, MDV), lambda bi, ci: (bi, 0, 0, 0)),
                   pl.BlockSpec((1, MH, MDK), lambda bi, ci: (bi, 0, 0)),
                   pl.BlockSpec((1, 1, LANES), lambda bi, ci: (bi, 0, 0))],
        out_shape=[jax.ShapeDtypeStruct((n, MV_W), BF16),
                   jax.ShapeDtypeStruct((b, MH, MDK, MDV), F32),
                   jax.ShapeDtypeStruct((b, MH, MDK), F32),
                   jax.ShapeDtypeStruct((b, 1, LANES), F32)],
        scratch_shapes=[pltpu.VMEM((MH, 1, LANES), F32)],
        compiler_params=_cparams(("arbitrary", "arbitrary")),
        name="mlstm_prompt",
    )(b_i, b_f, proj, proj, proj, proj, proj, ng)


def _mlstm_sample_kernel(bi_ref, bf_ref, q_ref, k_ref, v_ref, o_ref, g_ref, ng_ref, c0_ref, n0_ref, m0_ref,
                         hm_ref, c_ref, n_ref, m_ref):
    tb = q_ref.shape[0]
    lf = _log_gates(g_ref[...], bi_ref, bf_ref)
    lane = lax.broadcasted_iota(jnp.int32, (1, LANES), 1)
    m_out = jnp.zeros((tb, LANES), F32)
    for h in range(MH):
        li = lf[:, h:h + 1]
        lfg = lf[:, MH + h:MH + h + 1]
        m_prev = m0_ref[:, h:h + 1]
        m_inter = m_prev + lfg
        m_t = jnp.maximum(m_inter, li)
        d = jnp.exp(li - m_t)
        a = jnp.exp(m_inter - m_t)
        q = q_ref[:, h * MDK:(h + 1) * MDK]
        k = k_ref[:, h * MDK:(h + 1) * MDK] * (MDK ** -0.5)
        v = v_ref[:, h * MDV:(h + 1) * MDV]
        n_old = n0_ref[:, h, :]
        w = jnp.sum(q * k, axis=-1, keepdims=True) * d
        den = w + a * jnp.sum(q * n_old, axis=-1, keepdims=True)
        den = jnp.maximum(jnp.abs(den), jnp.exp(-m_t))
        dk = k * d
        qt = q.T
        dkt = dk.T
        rows = []
        for b in range(tb):
            c_old = c0_ref[b, h]
            qc = jnp.sum(c_old * qt[:, b:b + 1], axis=0, keepdims=True)
            a_b = a[b:b + 1, :]
            vrow = v[b:b + 1, :]
            rows.append((w[b:b + 1, :] * vrow + qc * a_b) / den[b:b + 1, :])
            c_ref[b, h] = a_b * c_old + dkt[:, b:b + 1] * vrow
        hh = jnp.concatenate(rows, axis=0)
        n_ref[:, h, :] = a * n_old + dk
        m_out = jnp.where(lane == h, m_t, m_out)
        hn = _rms(hh, ng_ref[:, h * MDV:(h + 1) * MDV]) * jax.nn.sigmoid(o_ref[:, h * MDV:(h + 1) * MDV])
        hm_ref[:, h * MDV:(h + 1) * MDV] = hn.astype(hm_ref.dtype)
    m_ref[...] = m_out


def _mlstm_sample(proj, off, bs, b_i, b_f, ng, c0, n0, m0):
    tb = 8
    mq_b, mk_b = off["mq"] // MQK_W, off["mk"] // MQK_W
    mv_b, mo_b, gt_b = off["mv"] // MV_W, off["mo"] // MV_W, off["gt"] // LANES
    smem = pl.BlockSpec(memory_space=pltpu.SMEM)
    return pl.pallas_call(
        _mlstm_sample_kernel,
        grid=(bs // tb,),
        in_specs=[smem, smem,
                  pl.BlockSpec((tb, MQK_W), lambda i: (i, mq_b)),
                  pl.BlockSpec((tb, MQK_W), lambda i: (i, mk_b)),
                  pl.BlockSpec((tb, MV_W), lambda i: (i, mv_b)),
                  pl.BlockSpec((tb, MV_W), lambda i: (i, mo_b)),
                  pl.BlockSpec((tb, LANES), lambda i: (i, gt_b)),
                  pl.BlockSpec((1, MV_W), lambda i: (0, 0)),
                  pl.BlockSpec((tb, MH, MDK, MDV), lambda i: (i, 0, 0, 0)),
                  pl.BlockSpec((tb, MH, MDK), lambda i: (i, 0, 0)),
                  pl.BlockSpec((tb, MH), lambda i: (i, 0))],
        out_specs=[pl.BlockSpec((tb, MV_W), lambda i: (i, 0)),
                   pl.BlockSpec((tb, MH, MDK, MDV), lambda i: (i, 0, 0, 0)),
                   pl.BlockSpec((tb, MH, MDK), lambda i: (i, 0, 0)),
                   pl.BlockSpec((tb, LANES), lambda i: (i, 0))],
        out_shape=[jax.ShapeDtypeStruct((bs, MV_W), BF16),
                   jax.ShapeDtypeStruct((bs, MH, MDK, MDV), F32),
                   jax.ShapeDtypeStruct((bs, MH, MDK), F32),
                   jax.ShapeDtypeStruct((bs, LANES), F32)],
        compiler_params=_cparams(("arbitrary",)),
        name="mlstm_sample",
    )(b_i, b_f, proj, proj, proj, proj, proj, ng, c0, n0, m0)


def _postmix_kernel(n_exp, n_main, *refs):
    if n_main is None:
        _postmix_tile(n_exp, *refs)
        return
    tile_refs, u2s_ref, out_refs = refs[:14], refs[14], refs[15:]
    i = pl.program_id(0)

    @pl.when(i < n_main)
    def _():
        _postmix_tile(n_exp, *tile_refs, *out_refs)

    @pl.when(i == n_main)
    def _():
        out_refs[1][0:u2s_ref.shape[0], :] = u2s_ref[...]


def _postmix_tile(n_exp, ha_ref, hm_ref, ga_ref, gm_ref, x_ref, g1_ref, sh2_ref, sc2_ref,
                  wa_ref, wm_ref, wo_ref, gf_ref, wr_ref, br_ref, x1_ref, u2_ref, ti_ref, tg_ref):
    a = jnp.dot(ha_ref[...], wa_ref[...], preferred_element_type=F32)
    m = jnp.dot(hm_ref[...], wm_ref[...], preferred_element_type=F32)
    merged = jax.nn.sigmoid(ga_ref[...]) * a + jax.nn.sigmoid(gm_ref[...]) * m
    y = jnp.dot(merged.astype(BF16), wo_ref[...], preferred_element_type=F32)
    x1 = x_ref[...] + g1_ref[...] * y
    x1_ref[...] = x1
    u2 = _rms(x1, gf_ref[...]) * (1.0 + sc2_ref[...]) + sh2_ref[...]
    u2_ref[...] = u2
    logits = jnp.dot(u2, wr_ref[...], preferred_element_type=F32, precision=lax.Precision.HIGHEST) + br_ref[...]
    lane = lax.broadcasted_iota(jnp.int32, logits.shape, 1)
    lanef = lane.astype(F32)
    work = jnp.where(lane < n_exp, logits, -jnp.inf)
    vals, idxs = [], []
    for _ in range(TOP_K):
        mx = jnp.max(work, axis=-1, keepdims=True)
        am = jnp.min(jnp.where(work == mx, lanef, float(LANES)), axis=-1, keepdims=True)
        vals.append(mx)
        idxs.append(am)
        work = jnp.where(lanef == am, -jnp.inf, work)
    es = [jnp.exp(v - vals[0]) for v in vals]
    tot = es[0] + es[1] + es[2] + es[3]
    ti = jnp.zeros(logits.shape, F32)
    tg = jnp.zeros(logits.shape, F32)
    for kk in range(TOP_K):
        ti = jnp.where(lane == kk, idxs[kk], ti)
        tg = jnp.where(lane == kk, es[kk] / tot, tg)
    ti_ref[...] = ti.astype(jnp.int32)
    tg_ref[...] = tg


def _postmix(ha, hm, proj, off, x2d, mod_block, mod_idx, mod, wa, wm, wo, gf, wr, br, n_exp, tm, name, u2_tail=None):
    n, d = x2d.shape
    nt = n // tm
    ga_b, gm_b = off["ga"] // d, off["gm"] // d
    const = lambda shape: pl.BlockSpec(shape, lambda i: (0,) * len(shape), pipeline_mode=pl.Buffered(1))
    ci = (lambda i: i) if u2_tail is None else (lambda i: jnp.minimum(i, nt - 1))
    row = lambda w, cb=0: pl.BlockSpec((tm, w), lambda i: (ci(i), cb))
    mspec = lambda k: pl.BlockSpec(mod_block, lambda i: mod_idx(ci(i), k))
    in_specs = [row(AQ_W), row(MV_W), row(d, ga_b), row(d, gm_b), row(d),
                mspec(2), mspec(3), mspec(4),
                const((AQ_W, d)), const((MV_W, d)), const((d, d)),
                const((1, d)), const((d, LANES)), const((1, LANES))]
    args = [ha, hm, proj, proj, x2d, mod, mod, mod, wa, wm, wo, gf, wr, br]
    n_u2 = n
    if u2_tail is not None:
        assert u2_tail.shape[0] <= tm
        in_specs.append(const(u2_tail.shape))
        args.append(u2_tail)
        n_u2 = n + u2_tail.shape[0]
    return pl.pallas_call(
        functools.partial(_postmix_kernel, n_exp, None if u2_tail is None else nt),
        grid=(nt if u2_tail is None else nt + 1,),
        in_specs=in_specs,
        out_specs=[row(d), pl.BlockSpec((tm, d), lambda i: (i, 0)), row(LANES), row(LANES)],
        out_shape=[jax.ShapeDtypeStruct((n, d), F32), jax.ShapeDtypeStruct((n_u2, d), F32),
                   jax.ShapeDtypeStruct((n, LANES), jnp.int32), jax.ShapeDtypeStruct((n, LANES), F32)],
        compiler_params=_cparams(("arbitrary",)),
        name=name,
    )(*args)


def _moe_kernel(n_s, n_c, se_ref, sf_ref, sn_ref, nu_ref, tokn_ref, u2_ref, wu_ref, wd_ref, bu_ref, bd_ref,
                yb_ref, xg, xb, acc, wub, wdp, wdb, zbuf, gsem, wsem, zsem):
    s = pl.program_id(0)
    c = pl.program_id(1)
    sub = MOE_SUB
    ns = sn_ref[s]
    ns_prev = sn_ref[jnp.maximum(s - 1, 0)]
    ns_next = sn_ref[jnp.minimum(s + 1, n_s - 1)]
    fb = sf_ref[s]
    per_step = sub // n_c

    def gather_rows(m):
        base = m * sub + c * per_step
        for i in range(per_step):
            j = base + i
            pltpu.make_async_copy(u2_ref.at[pl.ds(tokn_ref[0, j], 1)], xg.at[pl.ds(j, 1)], gsem).start()

    def wb_copy(m, blk):
        return pltpu.make_async_copy(acc.at[pl.ds(pl.multiple_of(m * sub, sub), sub)],
                                     yb_ref.at[pl.ds(pl.multiple_of(blk * sub, sub), sub)], wsem.at[m])

    @pl.when(c == 0)
    def _():
        def wait_row(i, carry):
            pltpu.make_async_copy(u2_ref.at[pl.ds(0, 1)], xg.at[pl.ds(0, 1)], gsem).wait()
            return carry
        lax.fori_loop(0, jnp.where(s > 0, sub * jnp.maximum(ns_prev, ns), 0), wait_row, 0)
        for m in range(MOE_NSUB):
            @pl.when((s > 0) & (m < ns_prev))
            def _():
                wb_copy(m, 0).wait()

        def prep(m, carry):
            r0 = pl.multiple_of(m * sub, sub)
            xb[pl.ds(r0, sub), :] = xg[pl.ds(r0, sub), :].astype(BF16)
            acc[pl.ds(r0, sub), :] = jnp.broadcast_to(bd_ref[...], (sub, acc.shape[1]))
            return carry
        lax.fori_loop(0, ns, prep, 0)

    @pl.when(ns > 0)
    def _():
        wub[...] = wu_ref[...].astype(BF16)
        half = LANES // 2
        for cb in range(wd_ref.shape[1] // LANES):
            cols = slice(cb * LANES, (cb + 1) * LANES)
            for g in range(wd_ref.shape[0] // LANES):
                wdp[cb, pl.ds(g * LANES, half, stride=2), :] = wd_ref[g * LANES:g * LANES + half, cols]
                wdp[cb, pl.ds(g * LANES + 1, half, stride=2), :] = wd_ref[g * LANES + half:(g + 1) * LANES, cols]
            wdb[:, cols] = wdp[cb].astype(BF16)

    def block(m, carry):
        r0 = pl.multiple_of(m * sub, sub)
        h = jnp.dot(xb[pl.ds(r0, sub), :], wub[...], preferred_element_type=F32) + bu_ref[...]
        even = lax.broadcasted_iota(jnp.int32, (sub, LANES), 1) % 2 == 0
        parts = []
        for g in range(h.shape[1] // (2 * LANES)):
            h0 = h[:, 2 * g * LANES:(2 * g + 1) * LANES]
            h1 = h[:, (2 * g + 1) * LANES:(2 * g + 2) * LANES]
            glu = jnp.where(even, h0, pltpu.roll(h1, 1, 1))
            lin = jnp.where(even, pltpu.roll(h0, LANES - 1, 1), h1)
            glu = jnp.minimum(glu, SWIGLU_LIMIT)
            lin = jnp.clip(lin, -SWIGLU_LIMIT, SWIGLU_LIMIT)
            parts.append(glu * jax.nn.sigmoid(SWIGLU_ALPHA * glu) * (lin + 1.0))
        act = jnp.concatenate(parts, axis=1).astype(BF16)
        acc[pl.ds(r0, sub), :] += jnp.dot(act, wdb[...], preferred_element_type=F32)
        gather_rows(m)

        @pl.when(c == n_c - 1)
        def _():
            wb_copy(m, fb + m).start()
        return carry

    lax.fori_loop(0, ns, block, 0)

    def extra(m, carry):
        gather_rows(m)
        return carry
    lax.fori_loop(ns, jnp.maximum(ns, ns_next), extra, 0)

    @pl.when((s == n_s - 1) & (c == n_c - 1))
    def _():
        n_blocks = yb_ref.shape[0] // sub
        zbuf[...] = jnp.zeros_like(zbuf)

        def zstart(b, carry):
            pltpu.make_async_copy(zbuf, yb_ref.at[pl.ds(pl.multiple_of(b * sub, sub), sub)], zsem).start()
            return carry

        def zwait(b, carry):
            pltpu.make_async_copy(zbuf, yb_ref.at[pl.ds(0, sub)], zsem).wait()
            return carry
        lax.fori_loop(nu_ref[0], n_blocks, zstart, 0)
        lax.fori_loop(nu_ref[0], n_blocks, zwait, 0)


def _moe(u2, tabs, n_blocks, w_up, b_up, w_down, b_down):
    sb_e, sb_fb, sb_ns, n_used, sb_tok = tabs
    n_s = sb_e.shape[0]
    n_exp, d, de2 = w_up.shape
    de = de2 // 2
    hc = min(MOE_HC, de)
    n_c = de // hc
    rmax = MOE_NSUB * MOE_SUB
    assert MOE_SUB % n_c == 0 and hc % LANES == 0

    def chunk(s, c, sn):
        return jnp.where(sn[s] > 0, c, jnp.where(s == 0, 0, n_c - 1))

    grid_spec = pltpu.PrefetchScalarGridSpec(
        num_scalar_prefetch=4,
        grid=(n_s, n_c),
        in_specs=[
            pl.BlockSpec((None, 1, rmax), lambda s, c, se, sf, sn, nu: (s + 1, 0, 0), memory_space=pltpu.SMEM),
            pl.BlockSpec(memory_space=pl.ANY),
            pl.BlockSpec((None, d, 2 * hc), lambda s, c, se, sf, sn, nu: (se[s], 0, chunk(s, c, sn))),
            pl.BlockSpec((None, hc, d), lambda s, c, se, sf, sn, nu: (se[s], chunk(s, c, sn), 0)),
            pl.BlockSpec((None, 1, 2 * hc), lambda s, c, se, sf, sn, nu: (se[s], 0, chunk(s, c, sn))),
            pl.BlockSpec((None, 1, d), lambda s, c, se, sf, sn, nu: (se[s], 0, 0)),
        ],
        out_specs=pl.BlockSpec(memory_space=pl.ANY),
        scratch_shapes=[pltpu.VMEM((rmax, d), F32), pltpu.VMEM((rmax, d), BF16), pltpu.VMEM((rmax, d), F32),
                        pltpu.VMEM((d, 2 * hc), BF16), pltpu.VMEM((d // LANES, hc, LANES), F32),
                        pltpu.VMEM((hc, d), BF16),
                        pltpu.VMEM((MOE_SUB, d), F32),
                        pltpu.SemaphoreType.DMA, pltpu.SemaphoreType.DMA((MOE_NSUB,)), pltpu.SemaphoreType.DMA],
    )
    return pl.pallas_call(
        functools.partial(_moe_kernel, n_s, n_c),
        grid_spec=grid_spec,
        out_shape=jax.ShapeDtypeStruct((n_blocks * MOE_SUB, d), F32),
        compiler_params=_cparams(("arbitrary", "arbitrary")),
        name="moe_experts",
    )(sb_e, sb_fb, sb_ns, n_used, sb_tok, u2, w_up, w_down, b_up.reshape(n_exp, 1, de2),
      b_down.reshape(n_exp, 1, d))


def _combine_kernel(pc_ref, pn_ref, yb_ref, x1_ref, g2_ref, tg_ref, o_ref, buf, sem):
    s = pl.program_id(0)
    ns = pl.num_programs(0)
    tm = x1_ref.shape[0]

    def issue(p_ref, slot):
        def body(t, carry):
            for kk in range(TOP_K):
                pltpu.make_async_copy(yb_ref.at[pl.ds(p_ref[0, t * TOP_K + kk], 1)],
                                      buf.at[slot, kk, pl.ds(t, 1)], sem.at[slot]).start()
            return carry
        lax.fori_loop(0, tm, body, 0)

    def drain(slot):
        def body(t, carry):
            for kk in range(TOP_K):
                pltpu.make_async_copy(yb_ref.at[pl.ds(0, 1)], buf.at[slot, kk, pl.ds(t, 1)], sem.at[slot]).wait()
            return carry
        lax.fori_loop(0, tm, body, 0)

    slot = s % 2

    @pl.when(s == 0)
    def _():
        issue(pc_ref, 0)

    @pl.when(s + 1 < ns)
    def _():
        issue(pn_ref, 1 - slot)

    drain(slot)
    tg = tg_ref[...]
    acc = tg[:, 0:1] * buf[slot, 0]
    for kk in range(1, TOP_K):
        acc = acc + tg[:, kk:kk + 1] * buf[slot, kk]
    o_ref[...] = x1_ref[...] + g2_ref[...] * acc


def _combine(pos, yb, x1, g2_spec, g2_arg, tg, tm, name):
    n, d = x1.shape
    ns = n // tm
    pos3 = pos.reshape(ns, 1, tm * TOP_K)
    return pl.pallas_call(
        _combine_kernel,
        grid=(ns,),
        in_specs=[pl.BlockSpec((None, 1, tm * TOP_K), lambda s: (s, 0, 0), memory_space=pltpu.SMEM),
                  pl.BlockSpec((None, 1, tm * TOP_K), lambda s: (jnp.minimum(s + 1, ns - 1), 0, 0),
                               memory_space=pltpu.SMEM),
                  pl.BlockSpec(memory_space=pl.ANY),
                  pl.BlockSpec((tm, d), lambda s: (s, 0)),
                  g2_spec,
                  pl.BlockSpec((tm, LANES), lambda s: (s, 0))],
        out_specs=pl.BlockSpec((tm, d), lambda s: (s, 0)),
        out_shape=jax.ShapeDtypeStruct((n, d), F32),
        scratch_shapes=[pltpu.VMEM((2, TOP_K, tm, d), F32), pltpu.SemaphoreType.DMA((2,))],
        compiler_params=_cparams(("arbitrary",)),
        name=name,
    )(pos3, pos3, yb, x1, g2_arg, tg)


def _routing(top_e, n_exp, n_blocks):
    sub, nsub = MOE_SUB, MOE_NSUB
    rmax = sub * nsub
    i32 = jnp.int32
    nk = top_e.size
    flat_e = top_e.reshape(nk)
    oh = (flat_e[:, None] == jnp.arange(n_exp, dtype=i32)[None, :]).astype(i32)
    csum = jnp.cumsum(oh, axis=0)
    rank = jnp.sum((csum - oh) * oh, axis=1)
    counts = csum[-1]
    nblk = (counts + sub - 1) // sub
    blk_end = jnp.cumsum(nblk)
    blk_start = blk_end - nblk
    pos = jnp.sum(oh * blk_start[None, :], axis=1) * sub + rank
    n_used = blk_end[-1]
    nsb = (nblk + nsub - 1) // nsub
    sb_end = jnp.cumsum(nsb)
    sb_start = sb_end - nsb
    n_sb = sb_end[-1]
    n_real = n_exp + n_blocks // nsub + 1
    sidx = jnp.arange(n_real, dtype=i32)
    e_of = jnp.minimum(jnp.sum((sidx[:, None] >= sb_end[None, :]).astype(i32), axis=1), n_exp - 1)
    oh_s = (e_of[:, None] == jnp.arange(n_exp, dtype=i32)[None, :]).astype(i32)
    k_in = sidx - jnp.sum(oh_s * sb_start[None, :], axis=1)
    fb = jnp.sum(oh_s * blk_start[None, :], axis=1) + nsub * k_in
    ns = jnp.clip(jnp.sum(oh_s * nblk[None, :], axis=1) - nsub * k_in, 0, nsub) * (sidx < n_sb)
    last_e = jnp.sum(jnp.where(sidx == n_sb - 1, e_of, 0))
    e_of = jnp.where(sidx < n_sb, e_of, last_e)
    zero = jnp.zeros((1,), i32)
    sb_e = jnp.concatenate([e_of[:1], e_of, last_e.reshape(1)]).astype(i32)
    sb_fb = jnp.concatenate([zero, fb, zero]).astype(i32)
    sb_ns = jnp.concatenate([zero, ns, zero]).astype(i32)
    slot = 1 + jnp.sum(oh * sb_start[None, :], axis=1) + rank // rmax
    tok = jnp.arange(nk, dtype=i32) // TOP_K
    sb_tok = jnp.zeros(((n_real + 3) * rmax,), i32).at[slot * rmax + rank % rmax].set(tok, unique_indices=True)
    tabs = (sb_e, sb_fb, sb_ns, n_used.reshape(1).astype(i32), sb_tok.reshape(n_real + 3, 1, rmax))
    return pos.astype(i32), tabs


def kernel(x_prompt, x_sample, cache_k, cache_v, state_C, state_n, state_m, c_prompt, c_sample, w_ada, b_ada,
           g_mix, w_in, b_igate, b_fgate, q_norm_g, k_norm_g, attn_sinks, mlstm_norm_g, w_attn_up, w_mlstm_up,
           w_out, g_ffn, w_router, b_router, w_up, b_up, w_down, b_down):
    bp, sp, d = x_prompt.shape
    bs = x_sample.shape[0]
    assert x_sample.shape[1] == 1 and w_ada.shape[0] == 1
    n_p = bp * sp
    win = cache_k.shape[2]
    n_exp = w_router.shape[2]
    de = w_down.shape[2]
    off, _ = _col_layout(d)

    w = w_in[0]
    cuts = np.cumsum([AQ_W, AKV_W, AKV_W, MQK_W, MQK_W, MV_W, MV_W, MH, MH, d, d])[:-1].tolist()
    aq, ak, av, mq, mk, mv, mo, mi, mf, ga, gm = jnp.split(w, cuts, axis=1)
    gpad = jnp.zeros((d, GATE_PAD - 2 * MH), w.dtype)
    w_r = jnp.concatenate([ga, gm, aq, mv, mo, ak, av, mq, mk, mi, mf, gpad], axis=1).astype(BF16)
    wa = w_attn_up[0].astype(BF16)
    wm = w_mlstm_up[0].astype(BF16)
    wo = w_out[0].astype(BF16)
    wr = jnp.pad(w_router[0], ((0, 0), (0, LANES - n_exp)))
    br = jnp.pad(b_router[0], (0, LANES - n_exp)).reshape(1, LANES)
    gmix = g_mix[0].reshape(1, d)
    gffn = g_ffn[0].reshape(1, d)
    qg = q_norm_g[0].reshape(1, HD)
    kg = k_norm_g[0].reshape(1, HD)
    ng = mlstm_norm_g[0].reshape(1, MV_W)
    sinks = attn_sinks[0]
    hs = jnp.stack([jnp.broadcast_to(jnp.asarray(_SLOPES, F32)[:, None], (ATTN_HEADS, LANES)),
                    jnp.broadcast_to(sinks[:, None], (ATTN_HEADS, LANES))])

    mod = _adaln(jnp.concatenate([c_prompt, c_sample], axis=0), w_ada[0], b_ada[0])
    mod_p = mod[:bp].reshape(bp, 1, N_MOD * d)
    mod_s = mod[bp:]

    xs = x_sample.reshape(bs, d)
    sspec2 = lambda kk: pl.BlockSpec((bs, d), lambda i, j: (0, kk))
    sspec1 = lambda kk: pl.BlockSpec((bs, d), lambda i: (0, kk))
    proj_s = _inproj(xs, gmix, (sspec2(0), sspec2(1)), (mod_s, mod_s), w_r, bs, "inproj_sample")
    q3 = proj_s[:, off["aq"]:off["aq"] + AQ_W].reshape(bs, ATTN_HEADS, HD)
    kn3 = proj_s[:, off["ak"]:off["ak"] + AKV_W].reshape(bs, 1, AKV_W)
    vn3 = proj_s[:, off["av"]:off["av"] + AKV_W].reshape(bs, 1, AKV_W)
    ha_s3, k_s, v_s = _attn_sample(q3, kn3, vn3, cache_k[0].reshape(bs, win, AKV_W),
                                   cache_v[0].reshape(bs, win, AKV_W), qg, kg, hs)
    hm_s, c_s, n_state_s, m_s = _mlstm_sample(proj_s, off, bs, b_igate[0], b_fgate[0], ng,
                                              state_C[0], state_n[0], state_m[0])
    x1_s, u2_s, ti_s, tg_s = _postmix(
        ha_s3.reshape(bs, AQ_W), hm_s, proj_s, off, xs, (bs, d), lambda i, k: (0, k), mod_s,
        wa, wm, wo, gffn, wr, br, n_exp, bs, "postmix_sample")

    xp = x_prompt.reshape(n_p, d)
    tm_in = min(512, sp)
    tps = sp // tm_in
    pspec = lambda t, kk, nargs: pl.BlockSpec(
        (None, 1, d), (lambda i, j: (i // t, 0, kk)) if nargs == 2 else (lambda i: (i // t, 0, kk)))
    proj_p = _inproj(xp, gmix, (pspec(tps, 0, 2), pspec(tps, 1, 2)), (mod_p, mod_p), w_r, tm_in, "inproj_prompt")
    ha_p, kn_p = _attn_prompt(proj_p, off, bp, sp, sinks, qg, kg)
    hm_p, c_p, n_state_p, m_p = _mlstm_prompt(proj_p, off, bp, sp, b_igate[0], b_fgate[0], ng)
    tm_pm = min(256, sp)
    tpp = sp // tm_pm
    x1_p, u2_all, ti_p, tg_p = _postmix(
        ha_p, hm_p, proj_p, off, xp, (None, 1, d), lambda i, k: (i // tpp, 0, k), mod_p,
        wa, wm, wo, gffn, wr, br, n_exp, tm_pm, "postmix_prompt", u2_tail=u2_s)

    nk = (n_p + bs) * TOP_K
    n_blocks = -(-nk // MOE_SUB) + n_exp
    top_e = jnp.concatenate([ti_p[:, :TOP_K], ti_s[:, :TOP_K]], axis=0)
    pos, tabs = _routing(top_e, n_exp, n_blocks)
    yb = _moe(u2_all, tabs, n_blocks, w_up[0], b_up[0], w_down[0], b_down[0])
    y_p = _combine(pos[:n_p * TOP_K], yb, x1_p, pspec(tpp, 5, 1), mod_p, tg_p, tm_pm, "combine_prompt")
    y_s = _combine(pos[n_p * TOP_K:], yb, x1_s, sspec1(5), mod_s, tg_s, bs, "combine_sample")

    kvshape = (1, bp, win, ATTN_KV, HD)
    k_p = kn_p.reshape(bp, sp, AKV_W)[:, sp - win:].reshape(kvshape)
    v_p = proj_p[:, off["av"]:off["av"] + AKV_W].reshape(bp, sp, AKV_W)[:, sp - win:].reshape(kvshape)
    return (y_p.reshape(bp, sp, d), y_s.reshape(bs, 1, d),
            k_p, v_p, c_p[None], n_state_p[None], m_p[:, 0, :MH][None],
            k_s.reshape(1, bs, win, ATTN_KV, HD), v_s.reshape(1, bs, win, ATTN_KV, HD),
            c_s[None], n_state_s[None], m_s[:, :MH][None])
```

```python
import functools

import numpy as np
import jax
import jax.numpy as jnp
from jax import lax
from jax.experimental import pallas as pl
from jax.experimental.pallas import tpu as pltpu

F32 = jnp.float32
BF16 = jnp.bfloat16

ATTN_HEADS = 16
ATTN_KV = 4
HD = 64
GQA = ATTN_HEADS // ATTN_KV
MH = 4
MDK = 128
MDV = 256
TOP_K = 4
N_MOD = 6
NORM_EPS = 1e-6
SWIGLU_LIMIT = 7.0
SWIGLU_ALPHA = 1.702

AQ_W = ATTN_HEADS * HD
AKV_W = ATTN_KV * HD
MQK_W = MH * MDK
MV_W = MH * MDV
GATE_PAD = 512

LANES = 128
MOE_SUB = 256
MOE_NSUB = 5
MOE_HC = 256
MOE_GATHER_GROUP = 8
MLSTM_L = 128
VMEM_LIMIT = 56 * 1024 * 1024

_SLOPES = [float(np.exp2(np.float32(-8.0 * (h + 1) / ATTN_HEADS))) for h in range(ATTN_HEADS)]


def _cparams(sem):
    return pltpu.CompilerParams(dimension_semantics=sem, vmem_limit_bytes=VMEM_LIMIT)


def _col_layout(d):
    off = {}
    o = 0
    for name, w in (("ga", d), ("gm", d), ("aq", AQ_W), ("mv", MV_W), ("mo", MV_W), ("ak", AKV_W),
                    ("av", AKV_W), ("mq", MQK_W), ("mk", MQK_W), ("gt", GATE_PAD)):
        off[name] = o
        o += w
    return off, o


def _rms(x, g):
    return x * lax.rsqrt(jnp.mean(x * x, axis=-1, keepdims=True) + NORM_EPS) * g


def _adaln_kernel(c_ref, w_ref, b_ref, o_ref):
    c = c_ref[...]
    s = (c * jax.nn.sigmoid(c)).astype(BF16)
    o_ref[...] = jnp.dot(s, w_ref[...].astype(BF16), preferred_element_type=F32) + b_ref[...]


def _adaln(c_all, w_ada, b_ada):
    r, d = c_all.shape
    w = w_ada.shape[1]
    tn = 1024
    return pl.pallas_call(
        _adaln_kernel,
        grid=(w // tn,),
        in_specs=[pl.BlockSpec((r, d), lambda j: (0, 0)),
                  pl.BlockSpec((d, tn), lambda j: (0, j)),
                  pl.BlockSpec((1, tn), lambda j: (0, j))],
        out_specs=pl.BlockSpec((r, tn), lambda j: (0, j)),
        out_shape=jax.ShapeDtypeStruct((r, w), F32),
        compiler_params=_cparams(("arbitrary",)),
        name="adaln_mod",
    )(c_all, w_ada, b_ada.reshape(1, w))


def _winprep_kernel(segs, gt, w_ref, o_ref):
    for src, width, dst in segs:
        o_ref[:, dst:dst + width] = w_ref[:, src:src + width].astype(o_ref.dtype)
    src, dst = gt
    lane = lax.broadcasted_iota(jnp.int32, (w_ref.shape[0], LANES), 1)
    o_ref[:, dst:dst + LANES] = jnp.where(lane < 2 * MH, w_ref[:, src:src + LANES], 0.0).astype(o_ref.dtype)
    o_ref[:, dst + LANES:dst + GATE_PAD] = jnp.zeros((w_ref.shape[0], GATE_PAD - LANES), o_ref.dtype)


def _winprep(w_in, d):
    off, wtot = _col_layout(d)
    src = dict(aq=0, ak=AQ_W, av=AQ_W + AKV_W, mq=AQ_W + 2 * AKV_W, mk=AQ_W + 2 * AKV_W + MQK_W,
               mv=AQ_W + 2 * AKV_W + 2 * MQK_W, mo=AQ_W + 2 * AKV_W + 2 * MQK_W + MV_W)
    src["gt"] = src["mo"] + MV_W
    src["ga"] = src["gt"] + 2 * MH
    src["gm"] = src["ga"] + d
    width = dict(aq=AQ_W, ak=AKV_W, av=AKV_W, mq=MQK_W, mk=MQK_W, mv=MV_W, mo=MV_W, ga=d, gm=d)
    segs = tuple((src[k], width[k], off[k]) for k in width)
    tr = 256
    return pl.pallas_call(
        functools.partial(_winprep_kernel, segs, (src["gt"], off["gt"])),
        grid=(w_in.shape[0] // tr,),
        in_specs=[pl.BlockSpec((tr, w_in.shape[1]), lambda i: (i, 0))],
        out_specs=pl.BlockSpec((tr, wtot), lambda i: (i, 0)),
        out_shape=jax.ShapeDtypeStruct((w_in.shape[0], wtot), BF16),
        compiler_params=_cparams(("arbitrary",)),
        name="winprep",
    )(w_in)


def _inproj_kernel(x_ref, g_ref, sh_ref, sc_ref, w_ref, o_ref, u_scr):
    @pl.when(pl.program_id(1) == 0)
    def _():
        y = _rms(x_ref[...], g_ref[...])
        u_scr[...] = (y * (1.0 + sc_ref[...]) + sh_ref[...]).astype(BF16)

    o_ref[...] = jnp.dot(u_scr[...], w_ref[...], preferred_element_type=F32)


def _inproj(x2d, g, mod_specs, mod_args, w_r, tm, name):
    n, d = x2d.shape
    w = w_r.shape[1]
    tn = 1024
    return pl.pallas_call(
        _inproj_kernel,
        grid=(n // tm, w // tn),
        in_specs=[pl.BlockSpec((tm, d), lambda i, j: (i, 0)),
                  pl.BlockSpec((1, d), lambda i, j: (0, 0)),
                  mod_specs[0], mod_specs[1],
                  pl.BlockSpec((d, tn), lambda i, j: (0, j))],
        out_specs=pl.BlockSpec((tm, tn), lambda i, j: (i, j)),
        out_shape=jax.ShapeDtypeStruct((n, w), F32),
        scratch_shapes=[pltpu.VMEM((tm, d), BF16)],
        compiler_params=_cparams(("arbitrary", "arbitrary")),
        name=name,
    )(x2d, g, mod_args[0], mod_args[1], w_r)


def _attn_prompt_kernel(sink_ref, q_ref, kc_ref, vc_ref, vp_ref, bias_ref, bd_ref, qg_ref, kg_ref, o_ref, kn_ref,
                        kprev):
    n = pl.program_id(1)
    lq = q_ref.shape[0]

    @pl.when(n == 0)
    def _():
        kprev[...] = jnp.zeros_like(kprev)

    def head_rms(x, g):
        x2 = x * x
        hi = x2.astype(BF16)
        lo = (x2 - hi.astype(F32)).astype(BF16)
        bd = bd_ref[0:x.shape[1], 0:x.shape[1]]
        ss = jnp.dot(hi, bd, preferred_element_type=F32) + jnp.dot(lo, bd, preferred_element_type=F32)
        return x * lax.rsqrt(ss * (1.0 / HD) + NORM_EPS) * g

    qn = head_rms(q_ref[...], qg_ref[...] * (HD ** -0.5))
    kn = head_rms(kc_ref[...], kg_ref[...])
    kn_ref[...] = kn
    grp = lax.broadcasted_iota(jnp.int32, (GQA * lq, 1), 0) // lq
    ones_col = jnp.where(lax.broadcasted_iota(jnp.int32, (2 * lq, HD), 1) == 0, 1.0, 0.0)
    for h in range(ATTN_KV):
        sl = slice(h * HD, (h + 1) * HD)
        kctx = jnp.concatenate([kprev[:, sl], kn[:, sl]], axis=0).astype(BF16)
        v = jnp.concatenate([vp_ref[:, sl], vc_ref[:, sl]], axis=0)
        vext = jnp.concatenate([v, ones_col], axis=1).astype(BF16)
        q4 = jnp.concatenate([qn[:, (h * GQA + g) * HD:(h * GQA + g + 1) * HD] for g in range(GQA)],
                             axis=0).astype(BF16)
        s = lax.dot_general(q4, kctx, (((1,), (1,)), ((), ())), preferred_element_type=F32) + bias_ref[h]
        sink = jnp.zeros((GQA * lq, 1), F32)
        for g in range(GQA):
            sink = jnp.where(grp == g, sink_ref[h * GQA + g], sink)
        m = jnp.maximum(jnp.max(s, axis=-1, keepdims=True), sink)
        p = jnp.exp(s - m).astype(BF16)
        oe = jnp.dot(p, vext, preferred_element_type=F32)
        o = oe[:, :HD] / (oe[:, HD:HD + 1] + jnp.exp(sink - m))
        for g in range(GQA):
            hq = h * GQA + g
            o_ref[:, hq * HD:(hq + 1) * HD] = o[g * lq:(g + 1) * lq].astype(o_ref.dtype)
    kprev[...] = kn


def _attn_bias(lq):
    t = jnp.arange(lq)[:, None]
    j = jnp.arange(2 * lq)[None, :]
    dist = lq + t - j
    inwin = (dist >= 0) & (dist < lq)
    slopes = jnp.asarray(_SLOPES, F32).reshape(ATTN_KV, GQA, 1, 1)
    b = -slopes * dist.astype(F32)[None, None]
    variants = [jnp.where((inwin & (j >= lq))[None, None], b, -jnp.inf), jnp.where(inwin[None, None], b, -jnp.inf)]
    return jnp.stack(variants).reshape(2, ATTN_KV, GQA * lq, 2 * lq)


def _attn_prompt(proj, off, b, s, sinks, qg, kg):
    lq = 128
    nq = s // lq
    n = b * s
    aq_b, ak_b, av_b = off["aq"] // AQ_W, off["ak"] // AKV_W, off["av"] // AKV_W
    cur = lambda bi, ni: bi * nq + ni
    prv = lambda bi, ni: bi * nq + jnp.maximum(ni - 1, 0)
    hid = jnp.arange(AQ_W) // HD
    same_head = (hid[:, None] == hid[None, :]).astype(BF16)
    return pl.pallas_call(
        _attn_prompt_kernel,
        grid=(b, nq),
        in_specs=[pl.BlockSpec(memory_space=pltpu.SMEM),
                  pl.BlockSpec((lq, AQ_W), lambda bi, ni: (cur(bi, ni), aq_b)),
                  pl.BlockSpec((lq, AKV_W), lambda bi, ni: (cur(bi, ni), ak_b)),
                  pl.BlockSpec((lq, AKV_W), lambda bi, ni: (cur(bi, ni), av_b)),
                  pl.BlockSpec((lq, AKV_W), lambda bi, ni: (prv(bi, ni), av_b)),
                  pl.BlockSpec((None, ATTN_KV, GQA * lq, 2 * lq), lambda bi, ni: (jnp.minimum(ni, 1), 0, 0, 0)),
                  pl.BlockSpec((AQ_W, AQ_W), lambda bi, ni: (0, 0)),
                  pl.BlockSpec((1, AQ_W), lambda bi, ni: (0, 0)),
                  pl.BlockSpec((1, AKV_W), lambda bi, ni: (0, 0))],
        out_specs=[pl.BlockSpec((lq, AQ_W), lambda bi, ni: (cur(bi, ni), 0)),
                   pl.BlockSpec((lq, AKV_W), lambda bi, ni: (cur(bi, ni), 0))],
        out_shape=[jax.ShapeDtypeStruct((n, AQ_W), BF16), jax.ShapeDtypeStruct((n, AKV_W), F32)],
        scratch_shapes=[pltpu.VMEM((lq, AKV_W), F32)],
        compiler_params=_cparams(("arbitrary", "arbitrary")),
        name="attn_prompt",
    )(sinks, proj, proj, proj, proj, _attn_bias(lq), same_head,
      jnp.tile(qg, (1, ATTN_HEADS)), jnp.tile(kg, (1, ATTN_KV)))


def _attn_sample_kernel(q_ref, kn_ref, vn_ref, ck_ref, cv_ref, qg_ref, kg_ref, hs_ref, o_ref, ok_ref, ov_ref):
    tb, nh, _ = q_ref.shape
    win = ck_ref.shape[1]
    q = _rms(q_ref[...], qg_ref[...])
    hq_i = lax.broadcasted_iota(jnp.int32, (1, nh, HD), 1)
    qbd = jnp.concatenate([jnp.where(hq_i // GQA == kv, q, 0.0) for kv in range(ATTN_KV)], axis=-1)
    kn = kn_ref[...]
    lane = lax.broadcasted_iota(jnp.int32, (1, 1, AKV_W), 2)
    kg4 = jnp.concatenate([kg_ref[...]] * ATTN_KV, axis=-1).reshape(1, 1, AKV_W)
    rs = jnp.zeros_like(kn)
    for kv in range(ATTN_KV):
        msk = lane // HD == kv
        ms = jnp.sum(jnp.where(msk, kn * kn, 0.0), axis=-1, keepdims=True) * (1.0 / HD)
        rs = jnp.where(msk, lax.rsqrt(ms + NORM_EPS), rs)
    knn = kn * rs * kg4
    vn = vn_ref[...]
    ck = ck_ref[...]
    cv = cv_ref[...]
    ok_ref[:, 0:win - 1, :] = ck[:, 1:win, :]
    ok_ref[:, win - 1:win, :] = knn
    ov_ref[:, 0:win - 1, :] = cv[:, 1:win, :]
    ov_ref[:, win - 1:win, :] = vn
    slope = hs_ref[0][None]
    sink = hs_ref[1][None][:, :, 0:1]
    jj = lax.broadcasted_iota(jnp.int32, (1, 1, win), 2)
    s = jnp.einsum("bhc,bjc->bhj", qbd.astype(BF16), ck.astype(BF16), preferred_element_type=F32) * (HD ** -0.5)
    s = jnp.where(jj >= 1, s - slope * (win - jj).astype(F32), -jnp.inf)
    s_new = jnp.sum(qbd * knn, axis=-1, keepdims=True) * (HD ** -0.5)
    m = jnp.maximum(jnp.maximum(jnp.max(s, axis=-1, keepdims=True), s_new), sink)
    p = jnp.exp(s - m)
    p_new = jnp.exp(s_new - m)
    den = jnp.sum(p, axis=-1, keepdims=True) + p_new + jnp.exp(sink - m)
    of = jnp.einsum("bhj,bjc->bhc", p.astype(BF16), cv.astype(BF16), preferred_element_type=F32) + p_new * vn
    of = of / den
    o = jnp.zeros((tb, nh, HD), F32)
    for kv in range(ATTN_KV):
        o = o + jnp.where(hq_i // GQA == kv, of[:, :, kv * HD:(kv + 1) * HD], 0.0)
    o_ref[...] = o.astype(o_ref.dtype)


def _attn_sample(q3, kn3, vn3, ck, cv, qg, kg, hs):
    bs, nh, _ = q3.shape
    win = ck.shape[1]
    tb = 16
    return pl.pallas_call(
        _attn_sample_kernel,
        grid=(bs // tb,),
        in_specs=[pl.BlockSpec((tb, nh, HD), lambda i: (i, 0, 0)),
                  pl.BlockSpec((tb, 1, AKV_W), lambda i: (i, 0, 0)),
                  pl.BlockSpec((tb, 1, AKV_W), lambda i: (i, 0, 0)),
                  pl.BlockSpec((tb, win, AKV_W), lambda i: (i, 0, 0)),
                  pl.BlockSpec((tb, win, AKV_W), lambda i: (i, 0, 0)),
                  pl.BlockSpec((1, 1, HD), lambda i: (0, 0, 0)),
                  pl.BlockSpec((1, HD), lambda i: (0, 0)),
                  pl.BlockSpec((2, nh, LANES), lambda i: (0, 0, 0))],
        out_specs=[pl.BlockSpec((tb, nh, HD), lambda i: (i, 0, 0)),
                   pl.BlockSpec((tb, win, AKV_W), lambda i: (i, 0, 0)),
                   pl.BlockSpec((tb, win, AKV_W), lambda i: (i, 0, 0))],
        out_shape=[jax.ShapeDtypeStruct((bs, nh, HD), BF16),
                   jax.ShapeDtypeStruct((bs, win, AKV_W), F32),
                   jax.ShapeDtypeStruct((bs, win, AKV_W), F32)],
        compiler_params=_cparams(("arbitrary",)),
        name="attn_sample",
    )(q3, kn3, vn3, ck, cv, qg.reshape(1, 1, HD), kg, hs)


def _log_gates(g_pre, bi_ref, bf_ref):
    lane = lax.broadcasted_iota(jnp.int32, (1, LANES), 1)
    bias = jnp.zeros((1, LANES), F32)
    for h in range(MH):
        bias = jnp.where(lane == h, bi_ref[h], bias)
        bias = jnp.where(lane == MH + h, bf_ref[h], bias)
    pre = g_pre + bias
    logsig = jnp.minimum(pre, 0.0) - jnp.log1p(jnp.exp(-jnp.abs(pre)))
    return jnp.where(lane < MH, pre, logsig)


def _mlstm_prompt_kernel(bi_ref, bf_ref, q_ref, k_ref, v_ref, o_ref, g_ref, ng_ref,
                         hm_ref, c_ref, n_ref, m_ref, m_scr):
    ci = pl.program_id(1)
    ln = q_ref.shape[0]

    @pl.when(ci == 0)
    def _():
        c_ref[...] = jnp.zeros_like(c_ref)
        n_ref[...] = jnp.zeros_like(n_ref)
        m_scr[...] = jnp.zeros_like(m_scr)

    lf = _log_gates(g_ref[...], bi_ref, bf_ref)
    row = lax.broadcasted_iota(jnp.int32, (ln, ln), 0)
    col = lax.broadcasted_iota(jnp.int32, (ln, ln), 1)
    causal = row >= col
    tril = causal.astype(F32)
    bc = jnp.dot(tril, lf, preferred_element_type=F32, precision=lax.Precision.HIGHEST)
    lft = lf.T
    bct = bc.T
    lane = lax.broadcasted_iota(jnp.int32, (1, LANES), 1)
    m_out = jnp.zeros((1, LANES), F32)
    for h in range(MH):
        i_row = lft[h:h + 1, :]
        b_row = bct[MH + h:MH + h + 1, :]
        i_col = lf[:, h:h + 1]
        b_col = bc[:, MH + h:MH + h + 1]
        m_prev = m_scr[h][:, 0:1]
        log_d = jnp.where(causal, i_row + b_col - b_row, -jnp.inf)
        m_inter = m_prev + b_col
        m_t = jnp.maximum(m_inter, jnp.max(log_d, axis=-1, keepdims=True))
        d = jnp.exp(log_d - m_t)
        a_inter = jnp.exp(m_inter - m_t)
        q = q_ref[:, h * MDK:(h + 1) * MDK]
        k = k_ref[:, h * MDK:(h + 1) * MDK] * (MDK ** -0.5)
        qb = q.astype(BF16)
        vb = v_ref[:, h * MDV:(h + 1) * MDV].astype(BF16)
        w = lax.dot_general(qb, k.astype(BF16), (((1,), (1,)), ((), ())), preferred_element_type=F32) * d
        c_old = c_ref[0, h]
        n_old = n_ref[0, h:h + 1, :]
        num = (jnp.dot(w.astype(BF16), vb, preferred_element_type=F32)
               + jnp.dot(qb, c_old.astype(BF16), preferred_element_type=F32) * a_inter)
        den = jnp.sum(w, axis=-1, keepdims=True) + a_inter * jnp.sum(q * n_old, axis=-1, keepdims=True)
        den = jnp.maximum(jnp.abs(den), jnp.exp(-m_t))
        hh = num / den
        m_new = m_t[ln - 1:ln, :]
        b_last = b_col[ln - 1:ln, :]
        decay = jnp.exp(i_col + b_last - b_col - m_new)
        carry = jnp.exp(m_prev + b_last - m_new)
        kd = k * decay
        c_ref[0, h] = carry * c_old + jnp.dot(kd.T.astype(BF16), vb, preferred_element_type=F32)
        n_ref[0, h:h + 1, :] = carry * n_old + jnp.sum(kd, axis=0, keepdims=True)
        m_scr[h] = jnp.broadcast_to(m_new, (1, LANES))
        m_out = jnp.where(lane == h, m_new, m_out)
        hn = _rms(hh, ng_ref[:, h * MDV:(h + 1) * MDV]) * jax.nn.sigmoid(o_ref[:, h * MDV:(h + 1) * MDV])
        hm_ref[:, h * MDV:(h + 1) * MDV] = hn.astype(hm_ref.dtype)
    m_ref[0] = m_out


def _mlstm_prompt(proj, off, b, s, b_i, b_f, ng):
    ln = MLSTM_L
    nc = s // ln
    n = b * s
    mq_b, mk_b = off["mq"] // MQK_W, off["mk"] // MQK_W
    mv_b, mo_b, gt_b = off["mv"] // MV_W, off["mo"] // MV_W, off["gt"] // LANES
    rowb = lambda bi, ci: bi * nc + ci
    smem = pl.BlockSpec(memory_space=pltpu.SMEM)
    return pl.pallas_call(
        _mlstm_prompt_kernel,
        grid=(b, nc),
        in_specs=[smem, smem,
                  pl.BlockSpec((ln, MQK_W), lambda bi, ci: (rowb(bi, ci), mq_b)),
                  pl.BlockSpec((ln, MQK_W), lambda bi, ci: (rowb(bi, ci), mk_b)),
                  pl.BlockSpec((ln, MV_W), lambda bi, ci: (rowb(bi, ci), mv_b)),
                  pl.BlockSpec((ln, MV_W), lambda bi, ci: (rowb(bi, ci), mo_b)),
                  pl.BlockSpec((ln, LANES), lambda bi, ci: (rowb(bi, ci), gt_b)),
                  pl.BlockSpec((1, MV_W), lambda bi, ci: (0, 0))],
        out_specs=[pl.BlockSpec((ln, MV_W), lambda bi, ci: (rowb(bi, ci), 0)),
                   pl.BlockSpec((1, MH, MDK, MDV), lambda bi, ci: (bi, 0, 0, 0)),
                   pl.BlockSpec((1, MH, MDK), lambda bi, ci: (bi, 0, 0)),
                   pl.BlockSpec((1, 1, LANES), lambda bi, ci: (bi, 0, 0))],
        out_shape=[jax.ShapeDtypeStruct((n, MV_W), BF16),
                   jax.ShapeDtypeStruct((b, MH, MDK, MDV), F32),
                   jax.ShapeDtypeStruct((b, MH, MDK), F32),
                   jax.ShapeDtypeStruct((b, 1, LANES), F32)],
        scratch_shapes=[pltpu.VMEM((MH, 1, LANES), F32)],
        compiler_params=_cparams(("arbitrary", "arbitrary")),
        name="mlstm_prompt",
    )(b_i, b_f, proj, proj, proj, proj, proj, ng)


def _mlstm_sample_kernel(bi_ref, bf_ref, q_ref, k_ref, v_ref, o_ref, g_ref, ng_ref, c0_ref, n0_ref, m0_ref,
                         hm_ref, c_ref, n_ref, m_ref):
    tb = q_ref.shape[0]
    lf = _log_gates(g_ref[...], bi_ref, bf_ref)
    lane = lax.broadcasted_iota(jnp.int32, (1, LANES), 1)
    m_out = jnp.zeros((tb, LANES), F32)
    for h in range(MH):
        li = lf[:, h:h + 1]
        lfg = lf[:, MH + h:MH + h + 1]
        m_prev = m0_ref[:, h:h + 1]
        m_inter = m_prev + lfg
        m_t = jnp.maximum(m_inter, li)
        d = jnp.exp(li - m_t)
        a = jnp.exp(m_inter - m_t)
        q = q_ref[:, h * MDK:(h + 1) * MDK]
        k = k_ref[:, h * MDK:(h + 1) * MDK] * (MDK ** -0.5)
        v = v_ref[:, h * MDV:(h + 1) * MDV]
        n_old = n0_ref[:, h, :]
        w = jnp.sum(q * k, axis=-1, keepdims=True) * d
        den = w + a * jnp.sum(q * n_old, axis=-1, keepdims=True)
        den = jnp.maximum(jnp.abs(den), jnp.exp(-m_t))
        dk = k * d
        qt = q.T
        dkt = dk.T
        rows = []
        for b in range(tb):
            c_old = c0_ref[b, h]
            qc = jnp.sum(c_old * qt[:, b:b + 1], axis=0, keepdims=True)
            a_b = a[b:b + 1, :]
            vrow = v[b:b + 1, :]
            rows.append((w[b:b + 1, :] * vrow + qc * a_b) / den[b:b + 1, :])
            c_ref[b, h] = a_b * c_old + dkt[:, b:b + 1] * vrow
        hh = jnp.concatenate(rows, axis=0)
        n_ref[:, h, :] = a * n_old + dk
        m_out = jnp.where(lane == h, m_t, m_out)
        hn = _rms(hh, ng_ref[:, h * MDV:(h + 1) * MDV]) * jax.nn.sigmoid(o_ref[:, h * MDV:(h + 1) * MDV])
        hm_ref[:, h * MDV:(h + 1) * MDV] = hn.astype(hm_ref.dtype)
    m_ref[...] = m_out


def _mlstm_sample(proj, off, bs, b_i, b_f, ng, c0, n0, m0):
    tb = 8
    mq_b, mk_b = off["mq"] // MQK_W, off["mk"] // MQK_W
    mv_b, mo_b, gt_b = off["mv"] // MV_W, off["mo"] // MV_W, off["gt"] // LANES
    smem = pl.BlockSpec(memory_space=pltpu.SMEM)
    return pl.pallas_call(
        _mlstm_sample_kernel,
        grid=(bs // tb,),
        in_specs=[smem, smem,
                  pl.BlockSpec((tb, MQK_W), lambda i: (i, mq_b)),
                  pl.BlockSpec((tb, MQK_W), lambda i: (i, mk_b)),
                  pl.BlockSpec((tb, MV_W), lambda i: (i, mv_b)),
                  pl.BlockSpec((tb, MV_W), lambda i: (i, mo_b)),
                  pl.BlockSpec((tb, LANES), lambda i: (i, gt_b)),
                  pl.BlockSpec((1, MV_W), lambda i: (0, 0)),
                  pl.BlockSpec((tb, MH, MDK, MDV), lambda i: (i, 0, 0, 0)),
                  pl.BlockSpec((tb, MH, MDK), lambda i: (i, 0, 0)),
                  pl.BlockSpec((tb, MH), lambda i: (i, 0))],
        out_specs=[pl.BlockSpec((tb, MV_W), lambda i: (i, 0)),
                   pl.BlockSpec((tb, MH, MDK, MDV), lambda i: (i, 0, 0, 0)),
                   pl.BlockSpec((tb, MH, MDK), lambda i: (i, 0, 0)),
                   pl.BlockSpec((tb, LANES), lambda i: (i, 0))],
        out_shape=[jax.ShapeDtypeStruct((bs, MV_W), BF16),
                   jax.ShapeDtypeStruct((bs, MH, MDK, MDV), F32),
                   jax.ShapeDtypeStruct((bs, MH, MDK), F32),
                   jax.ShapeDtypeStruct((bs, LANES), F32)],
        compiler_params=_cparams(("arbitrary",)),
        name="mlstm_sample",
    )(b_i, b_f, proj, proj, proj, proj, proj, ng, c0, n0, m0)


def _postmix_kernel(n_exp, n_main, *refs):
    if n_main is None:
        _postmix_tile(n_exp, *refs)
        return
    tile_refs, u2s_ref, out_refs = refs[:15], refs[15], refs[16:]
    i = pl.program_id(0)

    @pl.when(i < n_main)
    def _():
        _postmix_tile(n_exp, *tile_refs, *out_refs)

    @pl.when(i == n_main)
    def _():
        out_refs[1][0:u2s_ref.shape[0], :] = u2s_ref[...]


def _postmix_tile(n_exp, ha_ref, hm_ref, ga_ref, gm_ref, x_ref, g1_ref, sh2_ref, sc2_ref,
                  wa_ref, wm_ref, wo_ref, gf_ref, wrh_ref, wrl_ref, br_ref, x1_ref, u2_ref, ti_ref, tg_ref):
    a = jnp.dot(ha_ref[...], wa_ref[...], preferred_element_type=F32)
    m = jnp.dot(hm_ref[...], wm_ref[...], preferred_element_type=F32)
    merged = jax.nn.sigmoid(ga_ref[...]) * a + jax.nn.sigmoid(gm_ref[...]) * m
    y = jnp.dot(merged.astype(BF16), wo_ref[...], preferred_element_type=F32)
    x1 = x_ref[...] + g1_ref[...] * y
    x1_ref[...] = x1
    u2 = _rms(x1, gf_ref[...]) * (1.0 + sc2_ref[...]) + sh2_ref[...]
    u2_ref[...] = u2
    u2h = u2.astype(BF16)
    u2l = (u2 - u2h.astype(F32)).astype(BF16)
    logits = (jnp.dot(u2h, wrh_ref[...], preferred_element_type=F32)
              + (jnp.dot(u2l, wrh_ref[...], preferred_element_type=F32)
                 + jnp.dot(u2h, wrl_ref[...], preferred_element_type=F32))) + br_ref[...]
    lane = lax.broadcasted_iota(jnp.int32, logits.shape, 1)
    lanef = lane.astype(F32)
    work = jnp.where(lane < n_exp, logits, -jnp.inf)
    vals, idxs = [], []
    for _ in range(TOP_K):
        mx = jnp.max(work, axis=-1, keepdims=True)
        am = jnp.min(jnp.where(work == mx, lanef, float(LANES)), axis=-1, keepdims=True)
        vals.append(mx)
        idxs.append(am)
        work = jnp.where(lanef == am, -jnp.inf, work)
    es = [jnp.exp(v - vals[0]) for v in vals]
    tot = es[0] + es[1] + es[2] + es[3]
    ti = jnp.zeros(logits.shape, F32)
    tg = jnp.zeros(logits.shape, F32)
    for kk in range(TOP_K):
        ti = jnp.where(lane == kk, idxs[kk], ti)
        tg = jnp.where(lane == kk, es[kk] / tot, tg)
    ti_ref[...] = ti.astype(jnp.int32)
    tg_ref[...] = tg


def _postmix(ha, hm, proj, off, x2d, mod_block, mod_idx, mod, wa, wm, wo, gf, wr, br, n_exp, tm, name, u2_tail=None):
    n, d = x2d.shape
    nt = n // tm
    ga_b, gm_b = off["ga"] // d, off["gm"] // d
    const = lambda shape: pl.BlockSpec(shape, lambda i: (0,) * len(shape), pipeline_mode=pl.Buffered(1))
    ci = (lambda i: i) if u2_tail is None else (lambda i: jnp.minimum(i, nt - 1))
    row = lambda w, cb=0: pl.BlockSpec((tm, w), lambda i: (ci(i), cb))
    mspec = lambda k: pl.BlockSpec(mod_block, lambda i: mod_idx(ci(i), k))
    in_specs = [row(AQ_W), row(MV_W), row(d, ga_b), row(d, gm_b), row(d),
                mspec(2), mspec(3), mspec(4),
                const((AQ_W, d)), const((MV_W, d)), const((d, d)),
                const((1, d)), const((d, LANES)), const((d, LANES)), const((1, LANES))]
    args = [ha, hm, proj, proj, x2d, mod, mod, mod, wa, wm, wo, gf, wr[0], wr[1], br]
    n_u2 = n
    if u2_tail is not None:
        assert u2_tail.shape[0] <= tm
        in_specs.append(const(u2_tail.shape))
        args.append(u2_tail)
        n_u2 = n + u2_tail.shape[0]
    return pl.pallas_call(
        functools.partial(_postmix_kernel, n_exp, None if u2_tail is None else nt),
        grid=(nt if u2_tail is None else nt + 1,),
        in_specs=in_specs,
        out_specs=[row(d), pl.BlockSpec((tm, d), lambda i: (i, 0)), row(LANES), row(LANES)],
        out_shape=[jax.ShapeDtypeStruct((n, d), F32), jax.ShapeDtypeStruct((n_u2, d), F32),
                   jax.ShapeDtypeStruct((n, LANES), jnp.int32), jax.ShapeDtypeStruct((n, LANES), F32)],
        compiler_params=_cparams(("arbitrary",)),
        name=name,
    )(*args)


def _moe_kernel(n_s, n_c, se_ref, sf_ref, sn_ref, sr_ref, nu_ref, tokn_ref, u2_ref, wu_ref, wd_ref, bu_ref,
                bd_ref, yb_ref, xg, xb, acc, wub, wdp, wdb, zbuf, gsem, wsem, zsem):
    s = pl.program_id(0)
    c = pl.program_id(1)
    sub = MOE_SUB
    ns = sn_ref[s]
    ns_prev = sn_ref[jnp.maximum(s - 1, 0)]
    fb = sf_ref[s]
    grp = MOE_GATHER_GROUP
    groups = (sr_ref[s] + grp - 1) // grp
    groups_next = jnp.where(s + 1 < n_s, (sr_ref[jnp.minimum(s + 1, n_s - 1)] + grp - 1) // grp, 0)

    def gather_group(g, carry):
        base = pl.multiple_of(g * grp, grp)
        for i in range(grp):
            j = base + i
            pltpu.make_async_copy(u2_ref.at[pl.ds(tokn_ref[0, j], 1)], xg.at[pl.ds(j, 1)], gsem).start()
        return carry

    def wb_copy(m, blk):
        return pltpu.make_async_copy(acc.at[pl.ds(pl.multiple_of(m * sub, sub), sub)],
                                     yb_ref.at[pl.ds(pl.multiple_of(blk * sub, sub), sub)], wsem.at[m])

    @pl.when((s == 0) & (c == 0))
    def _():
        xg[...] = jnp.zeros_like(xg)

    @pl.when(c == 0)
    def _():
        def wait_group(i, carry):
            pltpu.make_async_copy(u2_ref.at[pl.ds(0, grp)], xg.at[pl.ds(0, grp)], gsem).wait()
            return carry
        lax.fori_loop(0, groups, wait_group, 0)
        for m in range(MOE_NSUB):
            @pl.when((s > 0) & (m < ns_prev))
            def _():
                wb_copy(m, 0).wait()

        def prep(m, carry):
            r0 = pl.multiple_of(m * sub, sub)
            xb[pl.ds(r0, sub), :] = xg[pl.ds(r0, sub), :].astype(BF16)
            acc[pl.ds(r0, sub), :] = jnp.broadcast_to(bd_ref[...], (sub, acc.shape[1]))
            return carry
        lax.fori_loop(0, ns, prep, 0)

    @pl.when(ns > 0)
    def _():
        wub[...] = wu_ref[...].astype(BF16)
        half = LANES // 2
        for cb in range(wd_ref.shape[1] // LANES):
            cols = slice(cb * LANES, (cb + 1) * LANES)
            for g in range(wd_ref.shape[0] // LANES):
                wdp[cb, pl.ds(g * LANES, half, stride=2), :] = wd_ref[g * LANES:g * LANES + half, cols]
                wdp[cb, pl.ds(g * LANES + 1, half, stride=2), :] = wd_ref[g * LANES + half:(g + 1) * LANES, cols]
            wdb[:, cols] = wdp[cb].astype(BF16)

    def down(m, act):
        r0 = pl.multiple_of(m * sub, sub)
        pw = 4 * LANES
        for p in range(wdb.shape[1] // pw):
            cols = slice(p * pw, (p + 1) * pw)
            acc[pl.ds(r0, sub), cols] += jnp.dot(act, wdb[:, cols], preferred_element_type=F32)

    def block(m, act_prev):
        r0 = pl.multiple_of(m * sub, sub)
        x = xb[pl.ds(r0, sub), :]
        gw = 2 * LANES
        hs = [jnp.dot(x, wub[:, g * gw:(g + 1) * gw], preferred_element_type=F32) + bu_ref[:, g * gw:(g + 1) * gw]
              for g in range(wub.shape[1] // gw)]
        down(jnp.maximum(m - 1, 0), act_prev)
        even = lax.broadcasted_iota(jnp.int32, (sub, LANES), 1) % 2 == 0
        parts = []
        for h in hs:
            h0 = h[:, :LANES]
            h1 = h[:, LANES:]
            glu = jnp.where(even, h0, pltpu.roll(h1, 1, 1))
            lin = jnp.where(even, pltpu.roll(h0, LANES - 1, 1), h1)
            glu = jnp.minimum(glu, SWIGLU_LIMIT)
            lin = jnp.clip(lin, -SWIGLU_LIMIT, SWIGLU_LIMIT)
            parts.append(glu * jax.nn.sigmoid(SWIGLU_ALPHA * glu) * (lin + 1.0))
        return jnp.concatenate(parts, axis=1).astype(BF16)

    def wb_after(m):
        @pl.when((c == n_c - 1) & (m > 0))
        def _():
            wb_copy(m - 1, fb + m - 1).start()

    def pair(p, act_prev):
        act = block(2 * p + 1, block(2 * p, act_prev))
        wb_after(2 * p)
        wb_after(2 * p + 1)
        return act

    def single(m, act_prev):
        act = block(m, act_prev)
        wb_after(m)
        return act

    act_last = lax.fori_loop(0, ns // 2, pair, jnp.zeros((sub, wdb.shape[0]), BF16))
    act_last = lax.fori_loop(2 * (ns // 2), ns, single, act_last)

    @pl.when(ns > 0)
    def _():
        down(ns - 1, act_last)

        @pl.when(c == n_c - 1)
        def _():
            wb_copy(ns - 1, fb + ns - 1).start()

    per_step = (groups_next + n_c - 1) // n_c
    lax.fori_loop(jnp.minimum(c * per_step, groups_next), jnp.minimum((c + 1) * per_step, groups_next),
                  gather_group, 0)

    @pl.when((s == n_s - 1) & (c == n_c - 1))
    def _():
        n_blocks = yb_ref.shape[0] // sub
        zbuf[...] = jnp.zeros_like(zbuf)

        def zstart(b, carry):
            pltpu.make_async_copy(zbuf, yb_ref.at[pl.ds(pl.multiple_of(b * sub, sub), sub)], zsem).start()
            return carry

        def zwait(b, carry):
            pltpu.make_async_copy(zbuf, yb_ref.at[pl.ds(0, sub)], zsem).wait()
            return carry
        lax.fori_loop(nu_ref[0], n_blocks, zstart, 0)
        lax.fori_loop(nu_ref[0], n_blocks, zwait, 0)


def _moe(u2, tabs, n_blocks, w_up, b_up, w_down, b_down):
    sb_e, sb_fb, sb_ns, sb_rows, n_used, sb_tok = tabs
    n_s = sb_e.shape[0]
    n_exp, d, de2 = w_up.shape
    de = de2 // 2
    hc = min(MOE_HC, de)
    n_c = de // hc
    rmax = MOE_NSUB * MOE_SUB
    assert rmax % MOE_GATHER_GROUP == 0 and hc % LANES == 0

    def chunk(s, c, sn):
        return jnp.where(sn[s] > 0, c, jnp.where(s == 0, 0, n_c - 1))

    grid_spec = pltpu.PrefetchScalarGridSpec(
        num_scalar_prefetch=5,
        grid=(n_s, n_c),
        in_specs=[
            pl.BlockSpec((None, 1, rmax), lambda s, c, se, sf, sn, sr, nu: (s + 1, 0, 0), memory_space=pltpu.SMEM),
            pl.BlockSpec(memory_space=pl.ANY),
            pl.BlockSpec((None, d, 2 * hc), lambda s, c, se, sf, sn, sr, nu: (se[s], 0, chunk(s, c, sn))),
            pl.BlockSpec((None, hc, d), lambda s, c, se, sf, sn, sr, nu: (se[s], chunk(s, c, sn), 0)),
            pl.BlockSpec((None, 1, 2 * hc), lambda s, c, se, sf, sn, sr, nu: (se[s], 0, chunk(s, c, sn))),
            pl.BlockSpec((None, 1, d), lambda s, c, se, sf, sn, sr, nu: (se[s], 0, 0)),
        ],
        out_specs=pl.BlockSpec(memory_space=pl.ANY),
        scratch_shapes=[pltpu.VMEM((rmax, d), F32), pltpu.VMEM((rmax, d), BF16), pltpu.VMEM((rmax, d), F32),
                        pltpu.VMEM((d, 2 * hc), BF16), pltpu.VMEM((d // LANES, hc, LANES), F32),
                        pltpu.VMEM((hc, d), BF16),
                        pltpu.VMEM((MOE_SUB, d), F32),
                        pltpu.SemaphoreType.DMA, pltpu.SemaphoreType.DMA((MOE_NSUB,)), pltpu.SemaphoreType.DMA],
    )
    return pl.pallas_call(
        functools.partial(_moe_kernel, n_s, n_c),
        grid_spec=grid_spec,
        out_shape=jax.ShapeDtypeStruct((n_blocks * MOE_SUB, d), F32),
        compiler_params=_cparams(("arbitrary", "arbitrary")),
        name="moe_experts",
    )(sb_e, sb_fb, sb_ns, sb_rows, n_used, sb_tok, u2, w_up, w_down, b_up.reshape(n_exp, 1, de2),
      b_down.reshape(n_exp, 1, d))


def _combine_kernel(pc_ref, pn_ref, yb_ref, x1_ref, g2_ref, tg_ref, o_ref, buf, sem):
    s = pl.program_id(0)
    ns = pl.num_programs(0)
    tm = x1_ref.shape[0]

    def issue(p_ref, slot):
        def body(t, carry):
            for kk in range(TOP_K):
                pltpu.make_async_copy(yb_ref.at[pl.ds(p_ref[0, t * TOP_K + kk], 1)],
                                      buf.at[slot, kk, pl.ds(t, 1)], sem.at[slot]).start()
            return carry
        lax.fori_loop(0, tm, body, 0)

    def drain(slot):
        for kk in range(TOP_K):
            pltpu.make_async_copy(yb_ref.at[pl.ds(0, tm)], buf.at[slot, kk], sem.at[slot]).wait()

    slot = s % 2

    @pl.when(s == 0)
    def _():
        issue(pc_ref, 0)

    @pl.when(s + 1 < ns)
    def _():
        issue(pn_ref, 1 - slot)

    drain(slot)
    tg = tg_ref[...]
    acc = tg[:, 0:1] * buf[slot, 0]
    for kk in range(1, TOP_K):
        acc = acc + tg[:, kk:kk + 1] * buf[slot, kk]
    o_ref[...] = x1_ref[...] + g2_ref[...] * acc


def _combine(pos, yb, x1, g2_spec, g2_arg, tg, tm, name):
    n, d = x1.shape
    ns = n // tm
    pos3 = pos.reshape(ns, 1, tm * TOP_K)
    return pl.pallas_call(
        _combine_kernel,
        grid=(ns,),
        in_specs=[pl.BlockSpec((None, 1, tm * TOP_K), lambda s: (s, 0, 0), memory_space=pltpu.SMEM),
                  pl.BlockSpec((None, 1, tm * TOP_K), lambda s: (jnp.minimum(s + 1, ns - 1), 0, 0),
                               memory_space=pltpu.SMEM),
                  pl.BlockSpec(memory_space=pl.ANY),
                  pl.BlockSpec((tm, d), lambda s: (s, 0)),
                  g2_spec,
                  pl.BlockSpec((tm, LANES), lambda s: (s, 0))],
        out_specs=pl.BlockSpec((tm, d), lambda s: (s, 0)),
        out_shape=jax.ShapeDtypeStruct((n, d), F32),
        scratch_shapes=[pltpu.VMEM((2, TOP_K, tm, d), F32), pltpu.SemaphoreType.DMA((2,))],
        compiler_params=_cparams(("arbitrary",)),
        name=name,
    )(pos3, pos3, yb, x1, g2_arg, tg)


def _routing(top_e, n_exp, n_blocks):
    sub, nsub = MOE_SUB, MOE_NSUB
    rmax = sub * nsub
    i32 = jnp.int32
    nk = top_e.size
    flat_e = top_e.reshape(nk)
    oh = (flat_e[:, None] == jnp.arange(n_exp, dtype=i32)[None, :]).astype(i32)
    csum = jnp.cumsum(oh, axis=0)
    rank = jnp.sum((csum - oh) * oh, axis=1)
    counts = csum[-1]
    nblk = (counts + sub - 1) // sub
    blk_end = jnp.cumsum(nblk)
    blk_start = blk_end - nblk
    pos = jnp.sum(oh * blk_start[None, :], axis=1) * sub + rank
    n_used = blk_end[-1]
    nsb = (nblk + nsub - 1) // nsub
    sb_end = jnp.cumsum(nsb)
    sb_start = sb_end - nsb
    n_sb = sb_end[-1]
    n_real = n_exp + n_blocks // nsub + 1
    sidx = jnp.arange(n_real, dtype=i32)
    e_of = jnp.minimum(jnp.sum((sidx[:, None] >= sb_end[None, :]).astype(i32), axis=1), n_exp - 1)
    oh_s = (e_of[:, None] == jnp.arange(n_exp, dtype=i32)[None, :]).astype(i32)
    k_in = sidx - jnp.sum(oh_s * sb_start[None, :], axis=1)
    fb = jnp.sum(oh_s * blk_start[None, :], axis=1) + nsub * k_in
    ns = jnp.clip(jnp.sum(oh_s * nblk[None, :], axis=1) - nsub * k_in, 0, nsub) * (sidx < n_sb)
    last_e = jnp.sum(jnp.where(sidx == n_sb - 1, e_of, 0))
    e_of = jnp.where(sidx < n_sb, e_of, last_e)
    zero = jnp.zeros((1,), i32)
    sb_e = jnp.concatenate([e_of[:1], e_of, last_e.reshape(1)]).astype(i32)
    sb_fb = jnp.concatenate([zero, fb, zero]).astype(i32)
    sb_ns = jnp.concatenate([zero, ns, zero]).astype(i32)
    rows = jnp.clip(jnp.sum(oh_s * counts[None, :], axis=1) - rmax * k_in, 0, rmax) * (sidx < n_sb)
    sb_rows = jnp.concatenate([zero, rows, zero]).astype(i32)
    slot = 1 + jnp.sum(oh * sb_start[None, :], axis=1) + rank // rmax
    tok = (jnp.arange(nk, dtype=i32) // TOP_K).astype(i32)
    flat = (slot * rmax + rank % rmax).astype(i32)
    sb_tok = jnp.zeros(((n_real + 3) * rmax,), i32).at[flat].set(tok, unique_indices=True)
    tabs = (sb_e, sb_fb, sb_ns, sb_rows, n_used.reshape(1).astype(i32), sb_tok.reshape(n_real + 3, 1, rmax))
    return pos.astype(i32), tabs


def kernel(x_prompt, x_sample, cache_k, cache_v, state_C, state_n, state_m, c_prompt, c_sample, w_ada, b_ada,
           g_mix, w_in, b_igate, b_fgate, q_norm_g, k_norm_g, attn_sinks, mlstm_norm_g, w_attn_up, w_mlstm_up,
           w_out, g_ffn, w_router, b_router, w_up, b_up, w_down, b_down):
    bp, sp, d = x_prompt.shape
    bs = x_sample.shape[0]
    assert x_sample.shape[1] == 1 and w_ada.shape[0] == 1
    n_p = bp * sp
    win = cache_k.shape[2]
    n_exp = w_router.shape[2]
    de = w_down.shape[2]
    off, _ = _col_layout(d)

    w_r = _winprep(w_in[0], d)
    wa =w_attn_up[0].astype(BF16)
    wm = w_mlstm_up[0].astype(BF16)
    wo = w_out[0].astype(BF16)
    wr32 = jnp.pad(w_router[0], ((0, 0), (0, LANES - n_exp)))
    wrh = wr32.astype(BF16)
    wr = (wrh, (wr32 - wrh.astype(F32)).astype(BF16))
    br = jnp.pad(b_router[0], (0, LANES - n_exp)).reshape(1, LANES)
    gmix = g_mix[0].reshape(1, d)
    gffn = g_ffn[0].reshape(1, d)
    qg = q_norm_g[0].reshape(1, HD)
    kg = k_norm_g[0].reshape(1, HD)
    ng = mlstm_norm_g[0].reshape(1, MV_W)
    sinks = attn_sinks[0]
    hs = jnp.stack([jnp.broadcast_to(jnp.asarray(_SLOPES, F32)[:, None], (ATTN_HEADS, LANES)),
                    jnp.broadcast_to(sinks[:, None], (ATTN_HEADS, LANES))])

    mod = _adaln(jnp.concatenate([c_prompt, c_sample], axis=0), w_ada[0], b_ada[0])
    mod_p = mod[:bp].reshape(bp, 1, N_MOD * d)
    mod_s = mod[bp:]

    xs = x_sample.reshape(bs, d)
    sspec2 = lambda kk: pl.BlockSpec((bs, d), lambda i, j: (0, kk))
    sspec1 = lambda kk: pl.BlockSpec((bs, d), lambda i: (0, kk))
    proj_s = _inproj(xs, gmix, (sspec2(0), sspec2(1)), (mod_s, mod_s), w_r, bs, "inproj_sample")
    q3 = proj_s[:, off["aq"]:off["aq"] + AQ_W].reshape(bs, ATTN_HEADS, HD)
    kn3 = proj_s[:, off["ak"]:off["ak"] + AKV_W].reshape(bs, 1, AKV_W)
    vn3 = proj_s[:, off["av"]:off["av"] + AKV_W].reshape(bs, 1, AKV_W)
    ha_s3, k_s, v_s = _attn_sample(q3, kn3, vn3, cache_k[0].reshape(bs, win, AKV_W),
                                   cache_v[0].reshape(bs, win, AKV_W), qg, kg, hs)
    hm_s, c_s, n_state_s, m_s = _mlstm_sample(proj_s, off, bs, b_igate[0], b_fgate[0], ng,
                                              state_C[0], state_n[0], state_m[0])
    x1_s, u2_s, ti_s, tg_s = _postmix(
        ha_s3.reshape(bs, AQ_W), hm_s, proj_s, off, xs, (bs, d), lambda i, k: (0, k), mod_s,
        wa, wm, wo, gffn, wr, br, n_exp, bs, "postmix_sample")

    xp = x_prompt.reshape(n_p, d)
    tm_in = min(1024, sp)
    tps = sp // tm_in
    pspec = lambda t, kk, nargs: pl.BlockSpec(
        (None, 1, d), (lambda i, j: (i // t, 0, kk)) if nargs == 2 else (lambda i: (i // t, 0, kk)))
    proj_p = _inproj(xp, gmix, (pspec(tps, 0, 2), pspec(tps, 1, 2)), (mod_p, mod_p), w_r, tm_in, "inproj_prompt")
    ha_p, kn_p = _attn_prompt(proj_p, off, bp, sp, sinks, qg, kg)
    hm_p, c_p, n_state_p, m_p = _mlstm_prompt(proj_p, off, bp, sp, b_igate[0], b_fgate[0], ng)
    tm_pm = min(256, sp)
    tpp = sp // tm_pm
    x1_p, u2_all, ti_p, tg_p = _postmix(
        ha_p, hm_p, proj_p, off, xp, (None, 1, d), lambda i, k: (i // tpp, 0, k), mod_p,
        wa, wm, wo, gffn, wr, br, n_exp, tm_pm, "postmix_prompt", u2_tail=u2_s)

    nk = (n_p + bs) * TOP_K
    n_blocks = -(-nk // MOE_SUB) + n_exp
    top_e = jnp.concatenate([ti_p[:, :TOP_K], ti_s[:, :TOP_K]], axis=0)
    pos, tabs = _routing(top_e, n_exp, n_blocks)
    yb = _moe(u2_all, tabs, n_blocks, w_up[0], b_up[0], w_down[0], b_down[0])
    y_p = _combine(pos[:n_p * TOP_K], yb, x1_p, pspec(tpp, 5, 1), mod_p, tg_p, tm_pm, "combine_prompt")
    y_s = _combine(pos[n_p * TOP_K:], yb, x1_s, sspec1(5), mod_s, tg_s, bs, "combine_sample")

    kvshape = (1, bp, win, ATTN_KV, HD)
    k_p = kn_p.reshape(bp, sp, AKV_W)[:, sp - win:].reshape(kvshape)
    v_p = proj_p[:, off["av"]:off["av"] + AKV_W].reshape(bp, sp, AKV_W)[:, sp - win:].reshape(kvshape)
    return (y_p.reshape(bp, sp, d), y_s.reshape(bs, 1, d),
            k_p, v_p, c_p[None], n_state_p[None], m_p[:, 0, :MH][None],
            k_s.reshape(1, bs, win, ATTN_KV, HD), v_s.reshape(1, bs, win, ATTN_KV, HD),
            c_s[None], n_state_s[None], m_s[:, :MH][None])
```

```python
import functools

import numpy as np
import jax
import jax.numpy as jnp
from jax import lax
from jax.experimental import pallas as pl
from jax.experimental.pallas import tpu as pltpu

F32 = jnp.float32
BF16 = jnp.bfloat16

ATTN_HEADS = 16
ATTN_KV = 4
HD = 64
GQA = ATTN_HEADS // ATTN_KV
MH = 4
MDK = 128
MDV = 256
TOP_K = 4
N_MOD = 6
NORM_EPS = 1e-6
SWIGLU_LIMIT = 7.0
SWIGLU_ALPHA = 1.702

AQ_W = ATTN_HEADS * HD
AKV_W = ATTN_KV * HD
MQK_W = MH * MDK
MV_W = MH * MDV
GATE_PAD = 512

LANES = 128
MOE_SUB = 256
MOE_NSUB = 5
MOE_HC = 256
MOE_GATHER_GROUP = 8
MLSTM_L = 128
VMEM_LIMIT = 56 * 1024 * 1024

_SLOPES = [float(np.exp2(np.float32(-8.0 * (h + 1) / ATTN_HEADS))) for h in range(ATTN_HEADS)]


def _cparams(sem):
    return pltpu.CompilerParams(dimension_semantics=sem, vmem_limit_bytes=VMEM_LIMIT)


def _col_layout(d):
    off = {}
    o = 0
    for name, w in (("ga", d), ("gm", d), ("aq", AQ_W), ("mv", MV_W), ("mo", MV_W), ("ak", AKV_W),
                    ("av", AKV_W), ("mq", MQK_W), ("mk", MQK_W), ("gt", GATE_PAD)):
        off[name] = o
        o += w
    return off, o


def _rms(x, g):
    return x * lax.rsqrt(jnp.mean(x * x, axis=-1, keepdims=True) + NORM_EPS) * g


def _adaln_kernel(c_ref, w_ref, b_ref, o_ref):
    c = c_ref[...]
    s = (c * jax.nn.sigmoid(c)).astype(BF16)
    o_ref[...] = jnp.dot(s, w_ref[...].astype(BF16), preferred_element_type=F32) + b_ref[...]


def _adaln(c_all, w_ada, b_ada):
    r, d = c_all.shape
    w = w_ada.shape[1]
    tn = 1024
    return pl.pallas_call(
        _adaln_kernel,
        grid=(w // tn,),
        in_specs=[pl.BlockSpec((r, d), lambda j: (0, 0)),
                  pl.BlockSpec((d, tn), lambda j: (0, j)),
                  pl.BlockSpec((1, tn), lambda j: (0, j))],
        out_specs=pl.BlockSpec((r, tn), lambda j: (0, j)),
        out_shape=jax.ShapeDtypeStruct((r, w), F32),
        compiler_params=_cparams(("arbitrary",)),
        name="adaln_mod",
    )(c_all, w_ada, b_ada.reshape(1, w))


def _winprep_kernel(segs, gt, w_ref, o_ref):
    for src, width, dst in segs:
        o_ref[:, dst:dst + width] = w_ref[:, src:src + width].astype(o_ref.dtype)
    src, dst = gt
    lane = lax.broadcasted_iota(jnp.int32, (w_ref.shape[0], LANES), 1)
    o_ref[:, dst:dst + LANES] = jnp.where(lane < 2 * MH, w_ref[:, src:src + LANES], 0.0).astype(o_ref.dtype)
    o_ref[:, dst + LANES:dst + GATE_PAD] = jnp.zeros((w_ref.shape[0], GATE_PAD - LANES), o_ref.dtype)


def _winprep(w_in, d):
    off, wtot = _col_layout(d)
    src = dict(aq=0, ak=AQ_W, av=AQ_W + AKV_W, mq=AQ_W + 2 * AKV_W, mk=AQ_W + 2 * AKV_W + MQK_W,
               mv=AQ_W + 2 * AKV_W + 2 * MQK_W, mo=AQ_W + 2 * AKV_W + 2 * MQK_W + MV_W)
    src["gt"] = src["mo"] + MV_W
    src["ga"] = src["gt"] + 2 * MH
    src["gm"] = src["ga"] + d
    width = dict(aq=AQ_W, ak=AKV_W, av=AKV_W, mq=MQK_W, mk=MQK_W, mv=MV_W, mo=MV_W, ga=d, gm=d)
    segs = tuple((src[k], width[k], off[k]) for k in width)
    tr = 256
    return pl.pallas_call(
        functools.partial(_winprep_kernel, segs, (src["gt"], off["gt"])),
        grid=(w_in.shape[1] // tr,),
        in_specs=[pl.BlockSpec((None, tr, w_in.shape[2]), lambda i: (0, i, 0))],
        out_specs=pl.BlockSpec((tr, wtot), lambda i: (i, 0)),
        out_shape=jax.ShapeDtypeStruct((w_in.shape[1], wtot), BF16),
        compiler_params=_cparams(("arbitrary",)),
        name="winprep",
    )(w_in)


def _inproj_kernel(x_ref, g_ref, sh_ref, sc_ref, w_ref, o_ref, u_scr):
    @pl.when(pl.program_id(1) == 0)
    def _():
        y = _rms(x_ref[...], g_ref[...])
        u_scr[...] = (y * (1.0 + sc_ref[...]) + sh_ref[...]).astype(BF16)

    o_ref[...] = jnp.dot(u_scr[...], w_ref[...], preferred_element_type=F32)


def _inproj(x2d, g, mod_specs, mod_args, w_r, tm, name):
    n, d = x2d.shape
    w = w_r.shape[1]
    tn = 1024
    return pl.pallas_call(
        _inproj_kernel,
        grid=(n // tm, w // tn),
        in_specs=[pl.BlockSpec((tm, d), lambda i, j: (i, 0)),
                  pl.BlockSpec((1, d), lambda i, j: (0, 0)),
                  mod_specs[0], mod_specs[1],
                  pl.BlockSpec((d, tn), lambda i, j: (0, j))],
        out_specs=pl.BlockSpec((tm, tn), lambda i, j: (i, j)),
        out_shape=jax.ShapeDtypeStruct((n, w), F32),
        scratch_shapes=[pltpu.VMEM((tm, d), BF16)],
        compiler_params=_cparams(("arbitrary", "arbitrary")),
        name=name,
    )(x2d, g, mod_args[0], mod_args[1], w_r)


def _attn_prompt_kernel(sink_ref, q_ref, kc_ref, vc_ref, vp_ref, bias_ref, bd_ref, qg_ref, kg_ref, o_ref, kn_ref,
                        kprev):
    n = pl.program_id(1)
    lq = q_ref.shape[0]

    @pl.when(n == 0)
    def _():
        kprev[...] = jnp.zeros_like(kprev)

    def head_rms(x, g):
        x2 = x * x
        hi = x2.astype(BF16)
        lo = (x2 - hi.astype(F32)).astype(BF16)
        bd = bd_ref[0:x.shape[1], 0:x.shape[1]]
        ss = jnp.dot(hi, bd, preferred_element_type=F32) + jnp.dot(lo, bd, preferred_element_type=F32)
        return x * lax.rsqrt(ss * (1.0 / HD) + NORM_EPS) * g

    qn = head_rms(q_ref[...], qg_ref[...] * (HD ** -0.5))
    kn = head_rms(kc_ref[...], kg_ref[...])
    kn_ref[...] = kn
    grp = lax.broadcasted_iota(jnp.int32, (GQA * lq, 1), 0) // lq
    ones_col = jnp.where(lax.broadcasted_iota(jnp.int32, (2 * lq, HD), 1) == 0, 1.0, 0.0)
    for h in range(ATTN_KV):
        sl = slice(h * HD, (h + 1) * HD)
        kctx = jnp.concatenate([kprev[:, sl], kn[:, sl]], axis=0).astype(BF16)
        v = jnp.concatenate([vp_ref[:, sl], vc_ref[:, sl]], axis=0)
        vext = jnp.concatenate([v, ones_col], axis=1).astype(BF16)
        q4 = jnp.concatenate([qn[:, (h * GQA + g) * HD:(h * GQA + g + 1) * HD] for g in range(GQA)],
                             axis=0).astype(BF16)
        s = lax.dot_general(q4, kctx, (((1,), (1,)), ((), ())), preferred_element_type=F32) + bias_ref[h]
        sink = jnp.zeros((GQA * lq, 1), F32)
        for g in range(GQA):
            sink = jnp.where(grp == g, sink_ref[h * GQA + g], sink)
        m = jnp.maximum(jnp.max(s, axis=-1, keepdims=True), sink)
        p = jnp.exp(s - m).astype(BF16)
        oe = jnp.dot(p, vext, preferred_element_type=F32)
        o = oe[:, :HD] / (oe[:, HD:HD + 1] + jnp.exp(sink - m))
        for g in range(GQA):
            hq = h * GQA + g
            o_ref[:, hq * HD:(hq + 1) * HD] = o[g * lq:(g + 1) * lq].astype(o_ref.dtype)
    kprev[...] = kn


def _attn_bias(lq):
    t = jnp.arange(lq)[:, None]
    j = jnp.arange(2 * lq)[None, :]
    dist = lq + t - j
    inwin = (dist >= 0) & (dist < lq)
    slopes = jnp.asarray(_SLOPES, F32).reshape(ATTN_KV, GQA, 1, 1)
    b = -slopes * dist.astype(F32)[None, None]
    variants = [jnp.where((inwin & (j >= lq))[None, None], b, -jnp.inf), jnp.where(inwin[None, None], b, -jnp.inf)]
    return jnp.stack(variants).reshape(2, ATTN_KV, GQA * lq, 2 * lq)


def _attn_prompt(proj, off, b, s, sinks, qg, kg):
    lq = 128
    nq = s // lq
    n = b * s
    aq_b, ak_b, av_b = off["aq"] // AQ_W, off["ak"] // AKV_W, off["av"] // AKV_W
    cur = lambda bi, ni: bi * nq + ni
    prv = lambda bi, ni: bi * nq + jnp.maximum(ni - 1, 0)
    hid = jnp.arange(AQ_W) // HD
    same_head = (hid[:, None] == hid[None, :]).astype(BF16)
    return pl.pallas_call(
        _attn_prompt_kernel,
        grid=(b, nq),
        in_specs=[pl.BlockSpec(memory_space=pltpu.SMEM),
                  pl.BlockSpec((lq, AQ_W), lambda bi, ni: (cur(bi, ni), aq_b)),
                  pl.BlockSpec((lq, AKV_W), lambda bi, ni: (cur(bi, ni), ak_b)),
                  pl.BlockSpec((lq, AKV_W), lambda bi, ni: (cur(bi, ni), av_b)),
                  pl.BlockSpec((lq, AKV_W), lambda bi, ni: (prv(bi, ni), av_b)),
                  pl.BlockSpec((None, ATTN_KV, GQA * lq, 2 * lq), lambda bi, ni: (jnp.minimum(ni, 1), 0, 0, 0)),
                  pl.BlockSpec((AQ_W, AQ_W), lambda bi, ni: (0, 0)),
                  pl.BlockSpec((1, AQ_W), lambda bi, ni: (0, 0)),
                  pl.BlockSpec((1, AKV_W), lambda bi, ni: (0, 0))],
        out_specs=[pl.BlockSpec((lq, AQ_W), lambda bi, ni: (cur(bi, ni), 0)),
                   pl.BlockSpec((lq, AKV_W), lambda bi, ni: (cur(bi, ni), 0))],
        out_shape=[jax.ShapeDtypeStruct((n, AQ_W), BF16), jax.ShapeDtypeStruct((n, AKV_W), F32)],
        scratch_shapes=[pltpu.VMEM((lq, AKV_W), F32)],
        compiler_params=_cparams(("arbitrary", "arbitrary")),
        name="attn_prompt",
    )(sinks, proj, proj, proj, proj, _attn_bias(lq), same_head,
      jnp.tile(qg, (1, ATTN_HEADS)), jnp.tile(kg, (1, ATTN_KV)))


def _attn_sample_kernel(q_ref, kn_ref, vn_ref, ck_ref, cv_ref, qg_ref, kg_ref, hs_ref, o_ref, ok_ref, ov_ref):
    tb, nh, _ = q_ref.shape
    win = ck_ref.shape[1]
    q = _rms(q_ref[...], qg_ref[...])
    hq_i = lax.broadcasted_iota(jnp.int32, (1, nh, HD), 1)
    qbd = jnp.concatenate([jnp.where(hq_i // GQA == kv, q, 0.0) for kv in range(ATTN_KV)], axis=-1)
    kn = kn_ref[...]
    lane = lax.broadcasted_iota(jnp.int32, (1, 1, AKV_W), 2)
    kg4 = jnp.concatenate([kg_ref[...]] * ATTN_KV, axis=-1).reshape(1, 1, AKV_W)
    rs = jnp.zeros_like(kn)
    for kv in range(ATTN_KV):
        msk = lane // HD == kv
        ms = jnp.sum(jnp.where(msk, kn * kn, 0.0), axis=-1, keepdims=True) * (1.0 / HD)
        rs = jnp.where(msk, lax.rsqrt(ms + NORM_EPS), rs)
    knn = kn * rs * kg4
    vn = vn_ref[...]
    ck = ck_ref[...]
    cv = cv_ref[...]
    ok_ref[:, 0:win - 1, :] = ck[:, 1:win, :]
    ok_ref[:, win - 1:win, :] = knn
    ov_ref[:, 0:win - 1, :] = cv[:, 1:win, :]
    ov_ref[:, win - 1:win, :] = vn
    slope = hs_ref[0][None]
    sink = hs_ref[1][None][:, :, 0:1]
    jj = lax.broadcasted_iota(jnp.int32, (1, 1, win), 2)
    s = jnp.einsum("bhc,bjc->bhj", qbd.astype(BF16), ck.astype(BF16), preferred_element_type=F32) * (HD ** -0.5)
    s = jnp.where(jj >= 1, s - slope * (win - jj).astype(F32), -jnp.inf)
    s_new = jnp.sum(qbd * knn, axis=-1, keepdims=True) * (HD ** -0.5)
    m = jnp.maximum(jnp.maximum(jnp.max(s, axis=-1, keepdims=True), s_new), sink)
    p = jnp.exp(s - m)
    p_new = jnp.exp(s_new - m)
    den = jnp.sum(p, axis=-1, keepdims=True) + p_new + jnp.exp(sink - m)
    of = jnp.einsum("bhj,bjc->bhc", p.astype(BF16), cv.astype(BF16), preferred_element_type=F32) + p_new * vn
    of = of / den
    o = jnp.zeros((tb, nh, HD), F32)
    for kv in range(ATTN_KV):
        o = o + jnp.where(hq_i // GQA == kv, of[:, :, kv * HD:(kv + 1) * HD], 0.0)
    o_ref[...] = o.astype(o_ref.dtype)


def _attn_sample(q3, kn3, vn3, ck, cv, qg, kg, hs):
    bs, nh, _ = q3.shape
    win = ck.shape[1]
    tb = 16
    return pl.pallas_call(
        _attn_sample_kernel,
        grid=(bs // tb,),
        in_specs=[pl.BlockSpec((tb, nh, HD), lambda i: (i, 0, 0)),
                  pl.BlockSpec((tb, 1, AKV_W), lambda i: (i, 0, 0)),
                  pl.BlockSpec((tb, 1, AKV_W), lambda i: (i, 0, 0)),
                  pl.BlockSpec((tb, win, AKV_W), lambda i: (i, 0, 0)),
                  pl.BlockSpec((tb, win, AKV_W), lambda i: (i, 0, 0)),
                  pl.BlockSpec((1, 1, HD), lambda i: (0, 0, 0)),
                  pl.BlockSpec((1, HD), lambda i: (0, 0)),
                  pl.BlockSpec((2, nh, LANES), lambda i: (0, 0, 0))],
        out_specs=[pl.BlockSpec((tb, nh, HD), lambda i: (i, 0, 0)),
                   pl.BlockSpec((tb, win, AKV_W), lambda i: (i, 0, 0)),
                   pl.BlockSpec((tb, win, AKV_W), lambda i: (i, 0, 0))],
        out_shape=[jax.ShapeDtypeStruct((bs, nh, HD), BF16),
                   jax.ShapeDtypeStruct((bs, win, AKV_W), F32),
                   jax.ShapeDtypeStruct((bs, win, AKV_W), F32)],
        compiler_params=_cparams(("arbitrary",)),
        name="attn_sample",
    )(q3, kn3, vn3, ck, cv, qg.reshape(1, 1, HD), kg, hs)


def _log_gates(g_pre, bi_ref, bf_ref):
    lane = lax.broadcasted_iota(jnp.int32, (1, LANES), 1)
    bias = jnp.zeros((1, LANES), F32)
    for h in range(MH):
        bias = jnp.where(lane == h, bi_ref[h], bias)
        bias = jnp.where(lane == MH + h, bf_ref[h], bias)
    pre = g_pre + bias
    logsig = jnp.minimum(pre, 0.0) - jnp.log1p(jnp.exp(-jnp.abs(pre)))
    return jnp.where(lane < MH, pre, logsig)


def _mlstm_prompt_kernel(bi_ref, bf_ref, q_ref, k_ref, v_ref, o_ref, g_ref, ng_ref,
                         hm_ref, c_ref, n_ref, m_ref, m_scr):
    ci = pl.program_id(1)
    nb, ln = q_ref.shape[0], q_ref.shape[1]

    @pl.when(ci == 0)
    def _():
        c_ref[...] = jnp.zeros_like(c_ref)
        n_ref[...] = jnp.zeros_like(n_ref)
        m_scr[...] = jnp.zeros_like(m_scr)

    row = lax.broadcasted_iota(jnp.int32, (ln, ln), 0)
    col = lax.broadcasted_iota(jnp.int32, (ln, ln), 1)
    causal = row >= col
    tril = causal.astype(F32)
    lane = lax.broadcasted_iota(jnp.int32, (1, LANES), 1)
    for j in range(nb):
        lf = _log_gates(g_ref[j], bi_ref, bf_ref)
        bc = jnp.dot(tril, lf, preferred_element_type=F32, precision=lax.Precision.HIGHEST)
        lft = lf.T
        bct = bc.T
        m_out = jnp.zeros((1, LANES), F32)
        for h in range(MH):
            i_row = lft[h:h + 1, :]
            b_row = bct[MH + h:MH + h + 1, :]
            i_col = lf[:, h:h + 1]
            b_col = bc[:, MH + h:MH + h + 1]
            m_prev = m_scr[j * MH + h][:, 0:1]
            log_d = jnp.where(causal, i_row + b_col - b_row, -jnp.inf)
            m_inter = m_prev + b_col
            m_t = jnp.maximum(m_inter, jnp.max(log_d, axis=-1, keepdims=True))
            d = jnp.exp(log_d - m_t)
            a_inter = jnp.exp(m_inter - m_t)
            q = q_ref[j, :, h * MDK:(h + 1) * MDK]
            k = k_ref[j, :, h * MDK:(h + 1) * MDK] * (MDK ** -0.5)
            qb = q.astype(BF16)
            vb = v_ref[j, :, h * MDV:(h + 1) * MDV].astype(BF16)
            w = lax.dot_general(qb, k.astype(BF16), (((1,), (1,)), ((), ())), preferred_element_type=F32) * d
            c_old = c_ref[j, h]
            n_old = n_ref[j, h:h + 1, :]
            num = (jnp.dot(w.astype(BF16), vb, preferred_element_type=F32)
                   + jnp.dot(qb, c_old.astype(BF16), preferred_element_type=F32) * a_inter)
            den = jnp.sum(w, axis=-1, keepdims=True) + a_inter * jnp.sum(q * n_old, axis=-1, keepdims=True)
            den = jnp.maximum(jnp.abs(den), jnp.exp(-m_t))
            hh = num / den
            m_new = m_t[ln - 1:ln, :]
            b_last = b_col[ln - 1:ln, :]
            decay = jnp.exp(i_col + b_last - b_col - m_new)
            carry = jnp.exp(m_prev + b_last - m_new)
            kd = k * decay
            c_ref[j, h] = carry * c_old + jnp.dot(kd.T.astype(BF16), vb, preferred_element_type=F32)
            n_ref[j, h:h + 1, :] = carry * n_old + jnp.sum(kd, axis=0, keepdims=True)
            m_scr[j * MH + h] = jnp.broadcast_to(m_new, (1, LANES))
            m_out = jnp.where(lane == h, m_new, m_out)
            hn = (_rms(hh, ng_ref[:, h * MDV:(h + 1) * MDV])
                  * jax.nn.sigmoid(o_ref[j, :, h * MDV:(h + 1) * MDV]))
            hm_ref[j, :, h * MDV:(h + 1) * MDV] = hn.astype(hm_ref.dtype)
        m_ref[j] = m_out


def _mlstm_prompt(proj, off, b, s, b_i, b_f, ng):
    ln = MLSTM_L
    nc = s // ln
    nb = 1
    mq_b, mk_b = off["mq"] // MQK_W, off["mk"] // MQK_W
    mv_b, mo_b, gt_b = off["mv"] // MV_W, off["mo"] // MV_W, off["gt"] // LANES
    proj3 = proj.reshape(b, s, proj.shape[1])
    smem = pl.BlockSpec(memory_space=pltpu.SMEM)
    col = lambda width, cb: pl.BlockSpec((nb, ln, width), lambda bi, ci: (bi, ci, cb))
    hm, c, n_state, m = pl.pallas_call(
        _mlstm_prompt_kernel,
        grid=(b // nb, nc),
        in_specs=[smem, smem, col(MQK_W, mq_b), col(MQK_W, mk_b), col(MV_W, mv_b), col(MV_W, mo_b),
                  col(LANES, gt_b), pl.BlockSpec((1, MV_W), lambda bi, ci: (0, 0))],
        out_specs=[col(MV_W, 0),
                   pl.BlockSpec((nb, MH, MDK, MDV), lambda bi, ci: (bi, 0, 0, 0)),
                   pl.BlockSpec((nb, MH, MDK), lambda bi, ci: (bi, 0, 0)),
                   pl.BlockSpec((nb, 1, LANES), lambda bi, ci: (bi, 0, 0))],
        out_shape=[jax.ShapeDtypeStruct((b, s, MV_W), BF16),
                   jax.ShapeDtypeStruct((b, MH, MDK, MDV), F32),
                   jax.ShapeDtypeStruct((b, MH, MDK), F32),
                   jax.ShapeDtypeStruct((b, 1, LANES), F32)],
        scratch_shapes=[pltpu.VMEM((nb * MH, 1, LANES), F32)],
        compiler_params=_cparams(("arbitrary", "arbitrary")),
        name="mlstm_prompt",
    )(b_i, b_f, proj3, proj3, proj3, proj3, proj3, ng)
    return hm.reshape(b * s, MV_W), c, n_state, m


def _mlstm_sample_kernel(bi_ref, bf_ref, q_ref, k_ref, v_ref, o_ref, g_ref, ng_ref, c0_ref, n0_ref, m0_ref,
                         hm_ref, c_ref, n_ref, m_ref):
    tb = q_ref.shape[0]
    lf = _log_gates(g_ref[...], bi_ref, bf_ref)
    lane = lax.broadcasted_iota(jnp.int32, (1, LANES), 1)
    m_out = jnp.zeros((tb, LANES), F32)
    for h in range(MH):
        li = lf[:, h:h + 1]
        lfg = lf[:, MH + h:MH + h + 1]
        m_prev = m0_ref[:, h:h + 1]
        m_inter = m_prev + lfg
        m_t = jnp.maximum(m_inter, li)
        d = jnp.exp(li - m_t)
        a = jnp.exp(m_inter - m_t)
        q = q_ref[:, h * MDK:(h + 1) * MDK]
        k = k_ref[:, h * MDK:(h + 1) * MDK] * (MDK ** -0.5)
        v = v_ref[:, h * MDV:(h + 1) * MDV]
        n_old = n0_ref[:, h, :]
        w = jnp.sum(q * k, axis=-1, keepdims=True) * d
        den = w + a * jnp.sum(q * n_old, axis=-1, keepdims=True)
        den = jnp.maximum(jnp.abs(den), jnp.exp(-m_t))
        dk = k * d
        qt = q.T
        dkt = dk.T
        rows = []
        for b in range(tb):
            c_old = c0_ref[b, h]
            qc = jnp.sum(c_old * qt[:, b:b + 1], axis=0, keepdims=True)
            a_b = a[b:b + 1, :]
            vrow = v[b:b + 1, :]
            rows.append((w[b:b + 1, :] * vrow + qc * a_b) / den[b:b + 1, :])
            c_ref[b, h] = a_b * c_old + dkt[:, b:b + 1] * vrow
        hh = jnp.concatenate(rows, axis=0)
        n_ref[:, h, :] = a * n_old + dk
        m_out = jnp.where(lane == h, m_t, m_out)
        hn = _rms(hh, ng_ref[:, h * MDV:(h + 1) * MDV]) * jax.nn.sigmoid(o_ref[:, h * MDV:(h + 1) * MDV])
        hm_ref[:, h * MDV:(h + 1) * MDV] = hn.astype(hm_ref.dtype)
    m_ref[...] = m_out


def _mlstm_sample(proj, off, bs, b_i, b_f, ng, c0, n0, m0):
    tb = 8
    mq_b, mk_b = off["mq"] // MQK_W, off["mk"] // MQK_W
    mv_b, mo_b, gt_b = off["mv"] // MV_W, off["mo"] // MV_W, off["gt"] // LANES
    smem = pl.BlockSpec(memory_space=pltpu.SMEM)
    return pl.pallas_call(
        _mlstm_sample_kernel,
        grid=(bs // tb,),
        in_specs=[smem, smem,
                  pl.BlockSpec((tb, MQK_W), lambda i: (i, mq_b)),
                  pl.BlockSpec((tb, MQK_W), lambda i: (i, mk_b)),
                  pl.BlockSpec((tb, MV_W), lambda i: (i, mv_b)),
                  pl.BlockSpec((tb, MV_W), lambda i: (i, mo_b)),
                  pl.BlockSpec((tb, LANES), lambda i: (i, gt_b)),
                  pl.BlockSpec((1, MV_W), lambda i: (0, 0)),
                  pl.BlockSpec((tb, MH, MDK, MDV), lambda i: (i, 0, 0, 0)),
                  pl.BlockSpec((tb, MH, MDK), lambda i: (i, 0, 0)),
                  pl.BlockSpec((tb, MH), lambda i: (i, 0))],
        out_specs=[pl.BlockSpec((tb, MV_W), lambda i: (i, 0)),
                   pl.BlockSpec((tb, MH, MDK, MDV), lambda i: (i, 0, 0, 0)),
                   pl.BlockSpec((tb, MH, MDK), lambda i: (i, 0, 0)),
                   pl.BlockSpec((tb, LANES), lambda i: (i, 0))],
        out_shape=[jax.ShapeDtypeStruct((bs, MV_W), BF16),
                   jax.ShapeDtypeStruct((bs, MH, MDK, MDV), F32),
                   jax.ShapeDtypeStruct((bs, MH, MDK), F32),
                   jax.ShapeDtypeStruct((bs, LANES), F32)],
        compiler_params=_cparams(("arbitrary",)),
        name="mlstm_sample",
    )(b_i, b_f, proj, proj, proj, proj, proj, ng, c0, n0, m0)


def _postmix_kernel(n_exp, n_main, *refs):
    if n_main is None:
        _postmix_tile(n_exp, *refs)
        return
    tile_refs, u2s_ref, out_refs = refs[:15], refs[15], refs[16:]
    i = pl.program_id(0)

    @pl.when(i < n_main)
    def _():
        _postmix_tile(n_exp, *tile_refs, *out_refs)

    @pl.when(i == n_main)
    def _():
        out_refs[1][0:u2s_ref.shape[0], :] = u2s_ref[...]


def _postmix_tile(n_exp, ha_ref, hm_ref, ga_ref, gm_ref, x_ref, g1_ref, sh2_ref, sc2_ref,
                  wa_ref, wm_ref, wo_ref, gf_ref, wrh_ref, wrl_ref, br_ref, x1_ref, u2_ref, ti_ref, tg_ref):
    a = jnp.dot(ha_ref[...], wa_ref[...], preferred_element_type=F32)
    m = jnp.dot(hm_ref[...], wm_ref[...], preferred_element_type=F32)
    merged = jax.nn.sigmoid(ga_ref[...]) * a + jax.nn.sigmoid(gm_ref[...]) * m
    y = jnp.dot(merged.astype(BF16), wo_ref[...], preferred_element_type=F32)
    x1 = x_ref[...] + g1_ref[...] * y
    x1_ref[...] = x1
    u2 = _rms(x1, gf_ref[...]) * (1.0 + sc2_ref[...]) + sh2_ref[...]
    u2_ref[...] = u2
    u2h = u2.astype(BF16)
    u2l = (u2 - u2h.astype(F32)).astype(BF16)
    logits = (jnp.dot(u2h, wrh_ref[...], preferred_element_type=F32)
              + (jnp.dot(u2l, wrh_ref[...], preferred_element_type=F32)
                 + jnp.dot(u2h, wrl_ref[...], preferred_element_type=F32))) + br_ref[...]
    lane = lax.broadcasted_iota(jnp.int32, logits.shape, 1)
    lanef = lane.astype(F32)
    work = jnp.where(lane < n_exp, logits, -jnp.inf)
    vals, idxs = [], []
    for _ in range(TOP_K):
        mx = jnp.max(work, axis=-1, keepdims=True)
        am = jnp.min(jnp.where(work == mx, lanef, float(LANES)), axis=-1, keepdims=True)
        vals.append(mx)
        idxs.append(am)
        work = jnp.where(lanef == am, -jnp.inf, work)
    es = [jnp.exp(v - vals[0]) for v in vals]
    tot = es[0] + es[1] + es[2] + es[3]
    ti = jnp.zeros(logits.shape, F32)
    tg = jnp.zeros(logits.shape, F32)
    for kk in range(TOP_K):
        ti = jnp.where(lane == kk, idxs[kk], ti)
        tg = jnp.where(lane == kk, es[kk] / tot, tg)
    ti_ref[...] = ti.astype(jnp.int32)
    tg_ref[...] = tg


def _postmix(ha, hm, proj, off, x2d, mod_block, mod_idx, mod, wa, wm, wo, gf, wr, br, n_exp, tm, name, u2_tail=None):
    n, d = x2d.shape
    nt = n // tm
    ga_b, gm_b = off["ga"] // d, off["gm"] // d
    const = lambda shape: pl.BlockSpec(shape, lambda i: (0,) * len(shape), pipeline_mode=pl.Buffered(1))
    ci = (lambda i: i) if u2_tail is None else (lambda i: jnp.minimum(i, nt - 1))
    row = lambda w, cb=0: pl.BlockSpec((tm, w), lambda i: (ci(i), cb))
    mspec = lambda k: pl.BlockSpec(mod_block, lambda i: mod_idx(ci(i), k))
    in_specs = [row(AQ_W), row(MV_W), row(d, ga_b), row(d, gm_b), row(d),
                mspec(2), mspec(3), mspec(4),
                const((AQ_W, d)), const((MV_W, d)), const((d, d)),
                const((1, d)), const((d, LANES)), const((d, LANES)), const((1, LANES))]
    args = [ha, hm, proj, proj, x2d, mod, mod, mod, wa, wm, wo, gf, wr[0], wr[1], br]
    n_u2 = n
    if u2_tail is not None:
        assert u2_tail.shape[0] <= tm
        in_specs.append(const(u2_tail.shape))
        args.append(u2_tail)
        n_u2 = n + u2_tail.shape[0]
    return pl.pallas_call(
        functools.partial(_postmix_kernel, n_exp, None if u2_tail is None else nt),
        grid=(nt if u2_tail is None else nt + 1,),
        in_specs=in_specs,
        out_specs=[row(d), pl.BlockSpec((tm, d), lambda i: (i, 0)), row(LANES), row(LANES)],
        out_shape=[jax.ShapeDtypeStruct((n, d), F32), jax.ShapeDtypeStruct((n_u2, d), F32),
                   jax.ShapeDtypeStruct((n, LANES), jnp.int32), jax.ShapeDtypeStruct((n, LANES), F32)],
        compiler_params=_cparams(("arbitrary",)),
        name=name,
    )(*args)


def _moe_kernel(n_s, n_c, se_ref, sf_ref, sn_ref, sr_ref, nu_ref, tokn_ref, u2_ref, wu_ref, wd_ref, bu_ref,
                bd_ref, yb_ref, xg, xb, acc, wub, wdp, wdb, act_carry, zbuf, gsem, wsem, zsem):
    s = pl.program_id(0)
    c = pl.program_id(1)
    sub = MOE_SUB
    ns = sn_ref[s]
    ns_prev = sn_ref[jnp.maximum(s - 1, 0)]
    fb = sf_ref[s]
    grp = MOE_GATHER_GROUP
    groups = (sr_ref[s] + grp - 1) // grp
    groups_next = jnp.where(s + 1 < n_s, (sr_ref[jnp.minimum(s + 1, n_s - 1)] + grp - 1) // grp, 0)

    def gather_group(g, carry):
        base = pl.multiple_of(g * grp, grp)
        for i in range(grp):
            j = base + i
            pltpu.make_async_copy(u2_ref.at[pl.ds(tokn_ref[0, j], 1)], xg.at[pl.ds(j, 1)], gsem).start()
        return carry

    def wb_copy(m, blk):
        return pltpu.make_async_copy(acc.at[pl.ds(pl.multiple_of(m * sub, sub), sub)],
                                     yb_ref.at[pl.ds(pl.multiple_of(blk * sub, sub), sub)], wsem.at[m])

    @pl.when((s == 0) & (c == 0))
    def _():
        xg[...] = jnp.zeros_like(xg)
        wdb[...] = jnp.zeros_like(wdb)
        act_carry[...] = jnp.zeros_like(act_carry)

    @pl.when(c == 0)
    def _():
        def wait_group(i, carry):
            pltpu.make_async_copy(u2_ref.at[pl.ds(0, grp)], xg.at[pl.ds(0, grp)], gsem).wait()
            return carry
        lax.fori_loop(0, groups, wait_group, 0)
        for m in range(MOE_NSUB):
            @pl.when((s > 0) & (m < ns_prev))
            def _():
                wb_copy(m, 0).wait()

        def prep(m, carry):
            r0 = pl.multiple_of(m * sub, sub)
            xb[pl.ds(r0, sub), :] = xg[pl.ds(r0, sub), :].astype(BF16)
            acc[pl.ds(r0, sub), :] = jnp.broadcast_to(bd_ref[...], (sub, acc.shape[1]))
            return carry
        lax.fori_loop(0, ns, prep, 0)

    def down(m, act):
        r0 = pl.multiple_of(m * sub, sub)
        pw = 4 * LANES
        for p in range(wdb.shape[1] // pw):
            cols = slice(p * pw, (p + 1) * pw)
            acc[pl.ds(r0, sub), cols] += jnp.dot(act, wdb[:, cols], preferred_element_type=F32)

    @pl.when(ns > 0)
    def _():
        down(ns - 1, act_carry[...])
        wub[...] = wu_ref[...].astype(BF16)
        half = LANES // 2
        for cb in range(wd_ref.shape[1] // LANES):
            cols = slice(cb * LANES, (cb + 1) * LANES)
            for g in range(wd_ref.shape[0] // LANES):
                wdp[cb, pl.ds(g * LANES, half, stride=2), :] = wd_ref[g * LANES:g * LANES + half, cols]
                wdp[cb, pl.ds(g * LANES + 1, half, stride=2), :] = wd_ref[g * LANES + half:(g + 1) * LANES, cols]
            wdb[:, cols] = wdp[cb].astype(BF16)

    def block(m, act_prev):
        r0 = pl.multiple_of(m * sub, sub)
        x = xb[pl.ds(r0, sub), :]
        gw = 2 * LANES
        hs = [jnp.dot(x, wub[:, g * gw:(g + 1) * gw], preferred_element_type=F32) + bu_ref[:, g * gw:(g + 1) * gw]
              for g in range(wub.shape[1] // gw)]
        if act_prev is not None:
            down(jnp.maximum(m - 1, 0), act_prev)
        even = lax.broadcasted_iota(jnp.int32, (sub, LANES), 1) % 2 == 0
        parts = []
        for h in hs:
            h0 = h[:, :LANES]
            h1 = h[:, LANES:]
            glu = jnp.where(even, h0, pltpu.roll(h1, 1, 1))
            lin = jnp.where(even, pltpu.roll(h0, LANES - 1, 1), h1)
            glu = jnp.minimum(glu, SWIGLU_LIMIT)
            lin = jnp.clip(lin, -SWIGLU_LIMIT, SWIGLU_LIMIT)
            parts.append(glu * jax.nn.sigmoid(SWIGLU_ALPHA * glu) * (lin + 1.0))
        return jnp.concatenate(parts, axis=1).astype(BF16)

    def wb_after(m):
        @pl.when((c == n_c - 1) & (m > 0))
        def _():
            wb_copy(m - 1, fb + m - 1).start()

    def pair(p, act_prev):
        act = block(2 * p + 1, block(2 * p, act_prev))
        wb_after(2 * p)
        wb_after(2 * p + 1)
        return act

    def single(m, act_prev):
        act = block(m, act_prev)
        wb_after(m)
        return act

    def first_pair():
        act = block(1, block(0, None))
        wb_after(1)
        return act

    act_last = lax.cond(ns >= 2, first_pair, lambda: jnp.zeros(act_carry.shape, BF16))
    act_last = lax.fori_loop(1, ns // 2, pair, act_last)
    act_last = lax.fori_loop(2 * (ns // 2), ns, single, act_last)

    @pl.when((ns > 0) & (c < n_c - 1))
    def _():
        act_carry[...] = act_last

    @pl.when((ns > 0) & (c == n_c - 1))
    def _():
        down(ns - 1, act_last)
        wb_copy(ns - 1, fb + ns - 1).start()
        act_carry[...] = jnp.zeros_like(act_carry)

    per_step = (groups_next + n_c - 1) // n_c
    lax.fori_loop(jnp.minimum(c * per_step, groups_next), jnp.minimum((c + 1) * per_step, groups_next),
                  gather_group, 0)

    @pl.when((s == n_s - 1) & (c == n_c - 1))
    def _():
        n_blocks = yb_ref.shape[0] // sub
        zbuf[...] = jnp.zeros_like(zbuf)

        def zstart(b, carry):
            pltpu.make_async_copy(zbuf, yb_ref.at[pl.ds(pl.multiple_of(b * sub, sub), sub)], zsem).start()
            return carry

        def zwait(b, carry):
            pltpu.make_async_copy(zbuf, yb_ref.at[pl.ds(0, sub)], zsem).wait()
            return carry
        lax.fori_loop(nu_ref[0], n_blocks, zstart, 0)
        lax.fori_loop(nu_ref[0], n_blocks, zwait, 0)


def _moe(u2, tabs, n_blocks, w_up, b_up, w_down, b_down):
    sb_e, sb_fb, sb_ns, sb_rows, n_used, sb_tok = tabs
    n_s = sb_e.shape[0]
    n_exp, d, de2 = w_up.shape
    de = de2 // 2
    hc = min(MOE_HC, de)
    n_c = de // hc
    rmax = MOE_NSUB * MOE_SUB
    assert rmax % MOE_GATHER_GROUP == 0 and hc % LANES == 0

    def chunk(s, c, sn):
        return jnp.where(sn[s] > 0, c, jnp.where(s == 0, 0, n_c - 1))

    grid_spec = pltpu.PrefetchScalarGridSpec(
        num_scalar_prefetch=5,
        grid=(n_s, n_c),
        in_specs=[
            pl.BlockSpec((None, 1, rmax), lambda s, c, se, sf, sn, sr, nu: (s + 1, 0, 0), memory_space=pltpu.SMEM),
            pl.BlockSpec(memory_space=pl.ANY),
            pl.BlockSpec((None, d, 2 * hc), lambda s, c, se, sf, sn, sr, nu: (se[s], 0, chunk(s, c, sn))),
            pl.BlockSpec((None, hc, d), lambda s, c, se, sf, sn, sr, nu: (se[s], chunk(s, c, sn), 0)),
            pl.BlockSpec((None, 1, 2 * hc), lambda s, c, se, sf, sn, sr, nu: (se[s], 0, chunk(s, c, sn))),
            pl.BlockSpec((None, 1, d), lambda s, c, se, sf, sn, sr, nu: (se[s], 0, 0)),
        ],
        out_specs=pl.BlockSpec(memory_space=pl.ANY),
        scratch_shapes=[pltpu.VMEM((rmax, d), F32), pltpu.VMEM((rmax, d), BF16), pltpu.VMEM((rmax, d), F32),
                        pltpu.VMEM((d, 2 * hc), BF16), pltpu.VMEM((d // LANES, hc, LANES), F32),
                        pltpu.VMEM((hc, d), BF16), pltpu.VMEM((MOE_SUB, hc), BF16),
                        pltpu.VMEM((MOE_SUB, d), F32),
                        pltpu.SemaphoreType.DMA, pltpu.SemaphoreType.DMA((MOE_NSUB,)), pltpu.SemaphoreType.DMA],
    )
    return pl.pallas_call(
        functools.partial(_moe_kernel, n_s, n_c),
        grid_spec=grid_spec,
        out_shape=jax.ShapeDtypeStruct((n_blocks * MOE_SUB, d), F32),
        compiler_params=_cparams(("arbitrary", "arbitrary")),
        name="moe_experts",
    )(sb_e, sb_fb, sb_ns, sb_rows, n_used, sb_tok, u2, w_up, w_down, b_up.reshape(n_exp, 1, de2),
      b_down.reshape(n_exp, 1, d))


def _combine_kernel(pc_ref, pn_ref, yb_ref, x1_ref, g2_ref, tg_ref, o_ref, buf, sem):
    s = pl.program_id(0)
    ns = pl.num_programs(0)
    tm = x1_ref.shape[0]

    def issue(p_ref, slot):
        def body(t, carry):
            for kk in range(TOP_K):
                pltpu.make_async_copy(yb_ref.at[pl.ds(p_ref[0, t * TOP_K + kk], 1)],
                                      buf.at[slot, kk, pl.ds(t, 1)], sem.at[slot]).start()
            return carry
        lax.fori_loop(0, tm, body, 0)

    def drain(slot):
        for kk in range(TOP_K):
            pltpu.make_async_copy(yb_ref.at[pl.ds(0, tm)], buf.at[slot, kk], sem.at[slot]).wait()

    slot = s % 2

    @pl.when(s == 0)
    def _():
        issue(pc_ref, 0)

    @pl.when(s + 1 < ns)
    def _():
        issue(pn_ref, 1 - slot)

    drain(slot)
    tg = tg_ref[...]
    acc = tg[:, 0:1] * buf[slot, 0]
    for kk in range(1, TOP_K):
        acc = acc + tg[:, kk:kk + 1] * buf[slot, kk]
    o_ref[...] = x1_ref[...] + g2_ref[...] * acc


def _combine(pos, yb, x1, g2_spec, g2_arg, tg, tm, name):
    n, d = x1.shape
    ns = n // tm
    pos3 = pos.reshape(ns, 1, tm * TOP_K)
    return pl.pallas_call(
        _combine_kernel,
        grid=(ns,),
        in_specs=[pl.BlockSpec((None, 1, tm * TOP_K), lambda s: (s, 0, 0), memory_space=pltpu.SMEM),
                  pl.BlockSpec((None, 1, tm * TOP_K), lambda s: (jnp.minimum(s + 1, ns - 1), 0, 0),
                               memory_space=pltpu.SMEM),
                  pl.BlockSpec(memory_space=pl.ANY),
                  pl.BlockSpec((tm, d), lambda s: (s, 0)),
                  g2_spec,
                  pl.BlockSpec((tm, LANES), lambda s: (s, 0))],
        out_specs=pl.BlockSpec((tm, d), lambda s: (s, 0)),
        out_shape=jax.ShapeDtypeStruct((n, d), F32),
        scratch_shapes=[pltpu.VMEM((2, TOP_K, tm, d), F32), pltpu.SemaphoreType.DMA((2,))],
        compiler_params=_cparams(("arbitrary",)),
        name=name,
    )(pos3, pos3, yb, x1, g2_arg, tg)


def _routing(top_e, n_exp, n_blocks):
    sub, nsub = MOE_SUB, MOE_NSUB
    rmax = sub * nsub
    i32 = jnp.int32
    nk = top_e.size
    flat_e = top_e.reshape(nk)
    oh = (flat_e[:, None] == jnp.arange(n_exp, dtype=i32)[None, :]).astype(i32)
    csum = jnp.cumsum(oh, axis=0)
    rank = jnp.sum((csum - oh) * oh, axis=1)
    counts = csum[-1]
    nblk = (counts + sub - 1) // sub
    blk_end = jnp.cumsum(nblk)
    blk_start = blk_end - nblk
    pos = jnp.sum(oh * blk_start[None, :], axis=1) * sub + rank
    n_used = blk_end[-1]
    nsb = (nblk + nsub - 1) // nsub
    sb_end = jnp.cumsum(nsb)
    sb_start = sb_end - nsb
    n_sb = sb_end[-1]
    n_real = n_exp + n_blocks // nsub + 1
    sidx = jnp.arange(n_real, dtype=i32)
    e_of = jnp.minimum(jnp.sum((sidx[:, None] >= sb_end[None, :]).astype(i32), axis=1), n_exp - 1)
    oh_s = (e_of[:, None] == jnp.arange(n_exp, dtype=i32)[None, :]).astype(i32)
    k_in = sidx - jnp.sum(oh_s * sb_start[None, :], axis=1)
    fb = jnp.sum(oh_s * blk_start[None, :], axis=1) + nsub * k_in
    ns = jnp.clip(jnp.sum(oh_s * nblk[None, :], axis=1) - nsub * k_in, 0, nsub) * (sidx < n_sb)
    last_e = jnp.sum(jnp.where(sidx == n_sb - 1, e_of, 0))
    e_of = jnp.where(sidx < n_sb, e_of, last_e)
    zero = jnp.zeros((1,), i32)
    sb_e = jnp.concatenate([e_of[:1], e_of, last_e.reshape(1)]).astype(i32)
    sb_fb = jnp.concatenate([zero, fb, zero]).astype(i32)
    sb_ns = jnp.concatenate([zero, ns, zero]).astype(i32)
    rows = jnp.clip(jnp.sum(oh_s * counts[None, :], axis=1) - rmax * k_in, 0, rmax) * (sidx < n_sb)
    sb_rows = jnp.concatenate([zero, rows, zero]).astype(i32)
    slot = 1 + jnp.sum(oh * sb_start[None, :], axis=1) + rank // rmax
    tok = (jnp.arange(nk, dtype=i32) // TOP_K).astype(i32)
    flat = (slot * rmax + rank % rmax).astype(i32)
    sb_tok = jnp.zeros(((n_real + 3) * rmax,), i32).at[flat].set(tok, unique_indices=True)
    tabs = (sb_e, sb_fb, sb_ns, sb_rows, n_used.reshape(1).astype(i32), sb_tok.reshape(n_real + 3, 1, rmax))
    return pos.astype(i32), tabs


def kernel(x_prompt, x_sample, cache_k, cache_v, state_C, state_n, state_m, c_prompt, c_sample, w_ada, b_ada,
           g_mix, w_in, b_igate, b_fgate, q_norm_g, k_norm_g, attn_sinks, mlstm_norm_g, w_attn_up, w_mlstm_up,
           w_out, g_ffn, w_router, b_router, w_up, b_up, w_down, b_down):
    bp, sp, d = x_prompt.shape
    bs = x_sample.shape[0]
    assert x_sample.shape[1] == 1 and w_ada.shape[0] == 1
    n_p = bp * sp
    win = cache_k.shape[2]
    n_exp = w_router.shape[2]
    de = w_down.shape[2]
    off, _ = _col_layout(d)

    w_r = _winprep(w_in, d)
    wa =w_attn_up[0].astype(BF16)
    wm = w_mlstm_up[0].astype(BF16)
    wo = w_out[0].astype(BF16)
    wr32 = jnp.pad(w_router[0], ((0, 0), (0, LANES - n_exp)))
    wrh = wr32.astype(BF16)
    wr = (wrh, (wr32 - wrh.astype(F32)).astype(BF16))
    br = jnp.pad(b_router[0], (0, LANES - n_exp)).reshape(1, LANES)
    gmix = g_mix[0].reshape(1, d)
    gffn = g_ffn[0].reshape(1, d)
    qg = q_norm_g[0].reshape(1, HD)
    kg = k_norm_g[0].reshape(1, HD)
    ng = mlstm_norm_g[0].reshape(1, MV_W)
    sinks = attn_sinks[0]
    hs = jnp.stack([jnp.broadcast_to(jnp.asarray(_SLOPES, F32)[:, None], (ATTN_HEADS, LANES)),
                    jnp.broadcast_to(sinks[:, None], (ATTN_HEADS, LANES))])

    mod = _adaln(jnp.concatenate([c_prompt, c_sample], axis=0), w_ada[0], b_ada[0])
    mod_p = mod[:bp].reshape(bp, 1, N_MOD * d)
    mod_s = mod[bp:]

    xs = x_sample.reshape(bs, d)
    sspec2 = lambda kk: pl.BlockSpec((bs, d), lambda i, j: (0, kk))
    sspec1 = lambda kk: pl.BlockSpec((bs, d), lambda i: (0, kk))
    proj_s = _inproj(xs, gmix, (sspec2(0), sspec2(1)), (mod_s, mod_s), w_r, bs, "inproj_sample")
    q3 = proj_s[:, off["aq"]:off["aq"] + AQ_W].reshape(bs, ATTN_HEADS, HD)
    kn3 = proj_s[:, off["ak"]:off["ak"] + AKV_W].reshape(bs, 1, AKV_W)
    vn3 = proj_s[:, off["av"]:off["av"] + AKV_W].reshape(bs, 1, AKV_W)
    ha_s3, k_s, v_s = _attn_sample(q3, kn3, vn3, cache_k[0].reshape(bs, win, AKV_W),
                                   cache_v[0].reshape(bs, win, AKV_W), qg, kg, hs)
    hm_s, c_s, n_state_s, m_s = _mlstm_sample(proj_s, off, bs, b_igate[0], b_fgate[0], ng,
                                              state_C[0], state_n[0], state_m[0])
    x1_s, u2_s, ti_s, tg_s = _postmix(
        ha_s3.reshape(bs, AQ_W), hm_s, proj_s, off, xs, (bs, d), lambda i, k: (0, k), mod_s,
        wa, wm, wo, gffn, wr, br, n_exp, bs, "postmix_sample")

    xp = x_prompt.reshape(n_p, d)
    tm_in = min(1024, sp)
    tps = sp // tm_in
    pspec = lambda t, kk, nargs: pl.BlockSpec(
        (None, 1, d), (lambda i, j: (i // t, 0, kk)) if nargs == 2 else (lambda i: (i // t, 0, kk)))
    proj_p = _inproj(xp, gmix, (pspec(tps, 0, 2), pspec(tps, 1, 2)), (mod_p, mod_p), w_r, tm_in, "inproj_prompt")
    ha_p, kn_p = _attn_prompt(proj_p, off, bp, sp, sinks, qg, kg)
    hm_p, c_p, n_state_p, m_p = _mlstm_prompt(proj_p, off, bp, sp, b_igate[0], b_fgate[0], ng)
    tm_pm = min(256, sp)
    tpp = sp // tm_pm
    x1_p, u2_all, ti_p, tg_p = _postmix(
        ha_p, hm_p, proj_p, off, xp, (None, 1, d), lambda i, k: (i // tpp, 0, k), mod_p,
        wa, wm, wo, gffn, wr, br, n_exp, tm_pm, "postmix_prompt", u2_tail=u2_s)

    nk = (n_p + bs) * TOP_K
    n_blocks = -(-nk // MOE_SUB) + n_exp
    top_e = jnp.concatenate([ti_p[:, :TOP_K], ti_s[:, :TOP_K]], axis=0)
    pos, tabs = _routing(top_e, n_exp, n_blocks)
    yb = _moe(u2_all, tabs, n_blocks, w_up[0], b_up[0], w_down[0], b_down[0])
    y_p = _combine(pos[:n_p * TOP_K], yb, x1_p, pspec(tpp, 5, 1), mod_p, tg_p, tm_pm, "combine_prompt")
    y_s = _combine(pos[n_p * TOP_K:], yb, x1_s, sspec1(5), mod_s, tg_s, bs, "combine_sample")

    kvshape = (1, bp, win, ATTN_KV, HD)
    k_p = kn_p.reshape(bp, sp, AKV_W)[:, sp - win:].reshape(kvshape)
    v_p = proj_p[:, off["av"]:off["av"] + AKV_W].reshape(bp, sp, AKV_W)[:, sp - win:].reshape(kvshape)
    return (y_p.reshape(bp, sp, d), y_s.reshape(bs, 1, d),
            k_p, v_p, c_p[None], n_state_p[None], m_p[:, 0, :MH][None],
            k_s.reshape(1, bs, win, ATTN_KV, HD), v_s.reshape(1, bs, win, ATTN_KV, HD),
            c_s[None], n_state_s[None], m_s[:, :MH][None])
```

```python
import functools

import numpy as np
import jax
import jax.numpy as jnp
from jax import lax
from jax.experimental import pallas as pl
from jax.experimental.pallas import tpu as pltpu

F32 = jnp.float32
BF16 = jnp.bfloat16

ATTN_HEADS = 16
ATTN_KV = 4
HD = 64
GQA = ATTN_HEADS // ATTN_KV
MH = 4
MDK = 128
MDV = 256
TOP_K = 4
N_MOD = 6
NORM_EPS = 1e-6
SWIGLU_LIMIT = 7.0
SWIGLU_ALPHA = 1.702

AQ_W = ATTN_HEADS * HD
AKV_W = ATTN_KV * HD
MQK_W = MH * MDK
MV_W = MH * MDV
GATE_PAD = 512

LANES = 128
MOE_SUB = 256
MOE_NSUB = 5
MOE_HC = 256
MOE_GATHER_GROUP = 8
MLSTM_L = 128
VMEM_LIMIT = 56 * 1024 * 1024

_SLOPES = [float(np.exp2(np.float32(-8.0 * (h + 1) / ATTN_HEADS))) for h in range(ATTN_HEADS)]


def _cparams(sem):
    return pltpu.CompilerParams(dimension_semantics=sem, vmem_limit_bytes=VMEM_LIMIT)


def _col_layout(d):
    off = {}
    o = 0
    for name, w in (("ga", d), ("gm", d), ("aq", AQ_W), ("mv", MV_W), ("mo", MV_W), ("ak", AKV_W),
                    ("av", AKV_W), ("mq", MQK_W), ("mk", MQK_W), ("gt", GATE_PAD)):
        off[name] = o
        o += w
    return off, o


def _rms(x, g):
    return x * lax.rsqrt(jnp.mean(x * x, axis=-1, keepdims=True) + NORM_EPS) * g


def _adaln_kernel(c_ref, w_ref, b_ref, o_ref):
    c = c_ref[...]
    s = (c * jax.nn.sigmoid(c)).astype(BF16)
    o_ref[...] = jnp.dot(s, w_ref[...].astype(BF16), preferred_element_type=F32) + b_ref[...]


def _adaln(c_all, w_ada, b_ada):
    r, d = c_all.shape
    w = w_ada.shape[1]
    tn = 1024
    return pl.pallas_call(
        _adaln_kernel,
        grid=(w // tn,),
        in_specs=[pl.BlockSpec((r, d), lambda j: (0, 0)),
                  pl.BlockSpec((d, tn), lambda j: (0, j)),
                  pl.BlockSpec((1, tn), lambda j: (0, j))],
        out_specs=pl.BlockSpec((r, tn), lambda j: (0, j)),
        out_shape=jax.ShapeDtypeStruct((r, w), F32),
        compiler_params=_cparams(("arbitrary",)),
        name="adaln_mod",
    )(c_all, w_ada, b_ada.reshape(1, w))


WINPREP_TR = 256


def _winprep_kernel(n_units, src_ref, nv_ref, wt_ref, o_ref, buf, sem):
    i = pl.program_id(0)
    tr = buf.shape[1]

    def fetch(step, slot):
        start = pl.multiple_of(src_ref[step], 8)
        return pltpu.make_async_copy(wt_ref.at[pl.ds(start, tr)], buf.at[slot], sem.at[slot])

    @pl.when(i == 0)
    def _():
        fetch(0, 0).start()

    @pl.when(i + 1 < n_units)
    def _():
        fetch(i + 1, (i + 1) % 2).start()

    fetch(i, i % 2).wait()
    row = lax.broadcasted_iota(jnp.int32, (tr, 1), 0)
    x = jnp.where(row < nv_ref[i], buf[i % 2], 0.0)
    o_ref[...] = x.T.astype(o_ref.dtype)


def _winprep(w_in_t, d):
    off, wtot = _col_layout(d)
    tr = WINPREP_TR
    src = dict(aq=0, ak=AQ_W, av=AQ_W + AKV_W, mq=AQ_W + 2 * AKV_W, mk=AQ_W + 2 * AKV_W + MQK_W,
               mv=AQ_W + 2 * AKV_W + 2 * MQK_W, mo=AQ_W + 2 * AKV_W + 2 * MQK_W + MV_W)
    src["gt"] = src["mo"] + MV_W
    src["ga"] = src["gt"] + 2 * MH
    src["gm"] = src["ga"] + d
    width = dict(aq=AQ_W, ak=AKV_W, av=AKV_W, mq=MQK_W, mk=MQK_W, mv=MV_W, mo=MV_W, ga=d, gm=d)
    n_units = wtot // tr
    starts = np.zeros((n_units,), np.int32)
    valid = np.zeros((n_units,), np.int32)
    for name, w in width.items():
        for u in range(w // tr):
            starts[off[name] // tr + u] = src[name] + u * tr
            valid[off[name] // tr + u] = tr
    starts[off["gt"] // tr] = src["gt"]
    valid[off["gt"] // tr] = 2 * MH
    assert int(starts.max()) + tr <= w_in_t.shape[0] and all(s % 8 == 0 for s in starts)
    grid_spec = pltpu.PrefetchScalarGridSpec(
        num_scalar_prefetch=2,
        grid=(n_units,),
        in_specs=[pl.BlockSpec(memory_space=pl.ANY)],
        out_specs=pl.BlockSpec((d, tr), lambda i, s, v: (0, i)),
        scratch_shapes=[pltpu.VMEM((2, tr, d), F32), pltpu.SemaphoreType.DMA((2,))],
    )
    return pl.pallas_call(
        functools.partial(_winprep_kernel, n_units),
        grid_spec=grid_spec,
        out_shape=jax.ShapeDtypeStruct((d, wtot), BF16),
        compiler_params=_cparams(("arbitrary",)),
        name="winprep",
    )(jnp.asarray(starts), jnp.asarray(valid), w_in_t)


def _inproj_kernel(x_ref, g_ref, sh_ref, sc_ref, w_ref, o_ref, u_scr):
    @pl.when(pl.program_id(1) == 0)
    def _():
        y = _rms(x_ref[...], g_ref[...])
        u_scr[...] = (y * (1.0 + sc_ref[...]) + sh_ref[...]).astype(BF16)

    o_ref[...] = jnp.dot(u_scr[...], w_ref[...], preferred_element_type=F32)


def _inproj(x2d, g, mod_specs, mod_args, w_r, tm, name):
    n, d = x2d.shape
    w = w_r.shape[1]
    tn = 1024
    return pl.pallas_call(
        _inproj_kernel,
        grid=(n // tm, w // tn),
        in_specs=[pl.BlockSpec((tm, d), lambda i, j: (i, 0)),
                  pl.BlockSpec((1, d), lambda i, j: (0, 0)),
                  mod_specs[0], mod_specs[1],
                  pl.BlockSpec((d, tn), lambda i, j: (0, j))],
        out_specs=pl.BlockSpec((tm, tn), lambda i, j: (i, j)),
        out_shape=jax.ShapeDtypeStruct((n, w), F32),
        scratch_shapes=[pltpu.VMEM((tm, d), BF16)],
        compiler_params=_cparams(("arbitrary", "arbitrary")),
        name=name,
    )(x2d, g, mod_args[0], mod_args[1], w_r)


def _attn_prompt_kernel(sink_ref, q_ref, kc_ref, vc_ref, vp_ref, bias_ref, bd_ref, qg_ref, kg_ref, o_ref, kn_ref,
                        kprev):
    n = pl.program_id(1)
    lq = q_ref.shape[0]

    @pl.when(n == 0)
    def _():
        kprev[...] = jnp.zeros_like(kprev)

    def head_rms(x, g):
        x2 = x * x
        hi = x2.astype(BF16)
        lo = (x2 - hi.astype(F32)).astype(BF16)
        bd = bd_ref[0:x.shape[1], 0:x.shape[1]]
        ss = jnp.dot(hi, bd, preferred_element_type=F32) + jnp.dot(lo, bd, preferred_element_type=F32)
        return x * lax.rsqrt(ss * (1.0 / HD) + NORM_EPS) * g

    qn = head_rms(q_ref[...], qg_ref[...] * (HD ** -0.5))
    kn = head_rms(kc_ref[...], kg_ref[...])
    kn_ref[...] = kn
    grp = lax.broadcasted_iota(jnp.int32, (GQA * lq, 1), 0) // lq
    ones_col = jnp.where(lax.broadcasted_iota(jnp.int32, (2 * lq, HD), 1) == 0, 1.0, 0.0)
    for h in range(ATTN_KV):
        sl = slice(h * HD, (h + 1) * HD)
        kctx = jnp.concatenate([kprev[:, sl], kn[:, sl]], axis=0).astype(BF16)
        v = jnp.concatenate([vp_ref[:, sl], vc_ref[:, sl]], axis=0)
        vext = jnp.concatenate([v, ones_col], axis=1).astype(BF16)
        q4 = jnp.concatenate([qn[:, (h * GQA + g) * HD:(h * GQA + g + 1) * HD] for g in range(GQA)],
                             axis=0).astype(BF16)
        s = lax.dot_general(q4, kctx, (((1,), (1,)), ((), ())), preferred_element_type=F32) + bias_ref[h]
        sink = jnp.zeros((GQA * lq, 1), F32)
        for g in range(GQA):
            sink = jnp.where(grp == g, sink_ref[h * GQA + g], sink)
        m = jnp.maximum(jnp.max(s, axis=-1, keepdims=True), sink)
        p = jnp.exp(s - m).astype(BF16)
        oe = jnp.dot(p, vext, preferred_element_type=F32)
        o = oe[:, :HD] / (oe[:, HD:HD + 1] + jnp.exp(sink - m))
        for g in range(GQA):
            hq = h * GQA + g
            o_ref[:, hq * HD:(hq + 1) * HD] = o[g * lq:(g + 1) * lq].astype(o_ref.dtype)
    kprev[...] = kn


def _attn_bias(lq):
    t = jnp.arange(lq)[:, None]
    j = jnp.arange(2 * lq)[None, :]
    dist = lq + t - j
    inwin = (dist >= 0) & (dist < lq)
    slopes = jnp.asarray(_SLOPES, F32).reshape(ATTN_KV, GQA, 1, 1)
    b = -slopes * dist.astype(F32)[None, None]
    variants = [jnp.where((inwin & (j >= lq))[None, None], b, -jnp.inf), jnp.where(inwin[None, None], b, -jnp.inf)]
    return jnp.stack(variants).reshape(2, ATTN_KV, GQA * lq, 2 * lq)


def _attn_prompt(proj, off, b, s, sinks, qg, kg):
    lq = 128
    nq = s // lq
    n = b * s
    aq_b, ak_b, av_b = off["aq"] // AQ_W, off["ak"] // AKV_W, off["av"] // AKV_W
    cur = lambda bi, ni: bi * nq + ni
    prv = lambda bi, ni: bi * nq + jnp.maximum(ni - 1, 0)
    hid = jnp.arange(AQ_W) // HD
    same_head = (hid[:, None] == hid[None, :]).astype(BF16)
    return pl.pallas_call(
        _attn_prompt_kernel,
        grid=(b, nq),
        in_specs=[pl.BlockSpec(memory_space=pltpu.SMEM),
                  pl.BlockSpec((lq, AQ_W), lambda bi, ni: (cur(bi, ni), aq_b)),
                  pl.BlockSpec((lq, AKV_W), lambda bi, ni: (cur(bi, ni), ak_b)),
                  pl.BlockSpec((lq, AKV_W), lambda bi, ni: (cur(bi, ni), av_b)),
                  pl.BlockSpec((lq, AKV_W), lambda bi, ni: (prv(bi, ni), av_b)),
                  pl.BlockSpec((None, ATTN_KV, GQA * lq, 2 * lq), lambda bi, ni: (jnp.minimum(ni, 1), 0, 0, 0)),
                  pl.BlockSpec((AQ_W, AQ_W), lambda bi, ni: (0, 0)),
                  pl.BlockSpec((1, AQ_W), lambda bi, ni: (0, 0)),
                  pl.BlockSpec((1, AKV_W), lambda bi, ni: (0, 0))],
        out_specs=[pl.BlockSpec((lq, AQ_W), lambda bi, ni: (cur(bi, ni), 0)),
                   pl.BlockSpec((lq, AKV_W), lambda bi, ni: (cur(bi, ni), 0))],
        out_shape=[jax.ShapeDtypeStruct((n, AQ_W), BF16), jax.ShapeDtypeStruct((n, AKV_W), F32)],
        scratch_shapes=[pltpu.VMEM((lq, AKV_W), F32)],
        compiler_params=_cparams(("arbitrary", "arbitrary")),
        name="attn_prompt",
    )(sinks, proj, proj, proj, proj, _attn_bias(lq), same_head,
      jnp.tile(qg, (1, ATTN_HEADS)), jnp.tile(kg, (1, ATTN_KV)))


def _attn_sample_kernel(q_ref, kn_ref, vn_ref, knt_ref, vnt_ref, ckt_ref, cvt_ref, qg_ref, kg_ref, kgt_ref, hs_ref,
                        o_ref, okt_ref, ovt_ref):
    tb, nh, _ = q_ref.shape
    win = ckt_ref.shape[2]
    q = _rms(q_ref[...], qg_ref[...])
    hq_i = lax.broadcasted_iota(jnp.int32, (1, nh, HD), 1)
    qbd = jnp.concatenate([jnp.where(hq_i // GQA == kv, q, 0.0) for kv in range(ATTN_KV)], axis=-1)
    kn = kn_ref[...]
    knt = knt_ref[...]
    lane = lax.broadcasted_iota(jnp.int32, (1, 1, AKV_W), 2)
    srow = lax.broadcasted_iota(jnp.int32, (1, AKV_W, 1), 1)
    rs = jnp.zeros_like(kn)
    rst = jnp.zeros_like(knt)
    for kv in range(ATTN_KV):
        msk = lane // HD == kv
        ms = jnp.sum(jnp.where(msk, kn * kn, 0.0), axis=-1, keepdims=True) * (1.0 / HD)
        rs = jnp.where(msk, lax.rsqrt(ms + NORM_EPS), rs)
        mskt = srow // HD == kv
        mst = jnp.sum(jnp.where(mskt, knt * knt, 0.0), axis=1, keepdims=True) * (1.0 / HD)
        rst = jnp.where(mskt, lax.rsqrt(mst + NORM_EPS), rst)
    knn = kn * rs * kg_ref[...][None]
    knnt = knt * rst * kgt_ref[...][None]
    vn = vn_ref[...]
    ckt = ckt_ref[...]
    cvt = cvt_ref[...]
    jj = lax.broadcasted_iota(jnp.int32, (1, 1, win), 2)
    okt_ref[...] = jnp.where(jj == win - 1, knnt, pltpu.roll(ckt, win - 1, 2))
    ovt_ref[...] = jnp.where(jj == win - 1, vnt_ref[...], pltpu.roll(cvt, win - 1, 2))
    slope = hs_ref[0][None]
    sink = hs_ref[1][None][:, :, 0:1]
    s = jnp.einsum("bhc,bcj->bhj", qbd.astype(BF16), ckt.astype(BF16), preferred_element_type=F32) * (HD ** -0.5)
    s = jnp.where(jj >= 1, s - slope * (win - jj).astype(F32), -jnp.inf)
    s_new = jnp.sum(qbd * knn, axis=-1, keepdims=True) * (HD ** -0.5)
    m = jnp.maximum(jnp.maximum(jnp.max(s, axis=-1, keepdims=True), s_new), sink)
    p = jnp.exp(s - m)
    p_new = jnp.exp(s_new - m)
    den = jnp.sum(p, axis=-1, keepdims=True) + p_new + jnp.exp(sink - m)
    of = jnp.einsum("bhj,bcj->bhc", p.astype(BF16), cvt.astype(BF16), preferred_element_type=F32) + p_new * vn
    of = of / den
    o = jnp.zeros((tb, nh, HD), F32)
    for kv in range(ATTN_KV):
        o = o + jnp.where(hq_i // GQA == kv, of[:, :, kv * HD:(kv + 1) * HD], 0.0)
    o_ref[...] = o.astype(o_ref.dtype)


def _attn_sample(q3, kn3, vn3, ckt, cvt, qg, kg, hs):
    bs, nh, _ = q3.shape
    win = ckt.shape[2]
    tb = 16
    seq3 = lambda a, b: pl.BlockSpec((tb, a, b), lambda i: (i, 0, 0))
    kg4 = jnp.tile(kg, (1, ATTN_KV))
    return pl.pallas_call(
        _attn_sample_kernel,
        grid=(bs // tb,),
        in_specs=[seq3(nh, HD), seq3(1, AKV_W), seq3(1, AKV_W), seq3(AKV_W, 1), seq3(AKV_W, 1),
                  seq3(AKV_W, win), seq3(AKV_W, win),
                  pl.BlockSpec((1, 1, HD), lambda i: (0, 0, 0)),
                  pl.BlockSpec((1, AKV_W), lambda i: (0, 0)),
                  pl.BlockSpec((AKV_W, 1), lambda i: (0, 0)),
                  pl.BlockSpec((2, nh, LANES), lambda i: (0, 0, 0))],
        out_specs=[seq3(nh, HD), seq3(AKV_W, win), seq3(AKV_W, win)],
        out_shape=[jax.ShapeDtypeStruct((bs, nh, HD), BF16),
                   jax.ShapeDtypeStruct((bs, AKV_W, win), F32),
                   jax.ShapeDtypeStruct((bs, AKV_W, win), F32)],
        compiler_params=_cparams(("arbitrary",)),
        name="attn_sample",
    )(q3, kn3, vn3, kn3.reshape(bs, AKV_W, 1), vn3.reshape(bs, AKV_W, 1), ckt, cvt,
      qg.reshape(1, 1, HD), kg4, kg4.reshape(AKV_W, 1), hs)


def _log_gates(g_pre, bi_ref, bf_ref):
    lane = lax.broadcasted_iota(jnp.int32, (1, LANES), 1)
    bias = jnp.zeros((1, LANES), F32)
    for h in range(MH):
        bias = jnp.where(lane == h, bi_ref[h], bias)
        bias = jnp.where(lane == MH + h, bf_ref[h], bias)
    pre = g_pre + bias
    logsig = jnp.minimum(pre, 0.0) - jnp.log1p(jnp.exp(-jnp.abs(pre)))
    return jnp.where(lane < MH, pre, logsig)


def _mlstm_prompt_kernel(bi_ref, bf_ref, q_ref, k_ref, v_ref, o_ref, g_ref, ng_ref,
                         hm_ref, c_ref, n_ref, m_ref, m_scr):
    ci = pl.program_id(1)
    nb, ln = q_ref.shape[0], q_ref.shape[1]

    @pl.when(ci == 0)
    def _():
        c_ref[...] = jnp.zeros_like(c_ref)
        n_ref[...] = jnp.zeros_like(n_ref)
        m_scr[...] = jnp.zeros_like(m_scr)

    row = lax.broadcasted_iota(jnp.int32, (ln, ln), 0)
    col = lax.broadcasted_iota(jnp.int32, (ln, ln), 1)
    causal = row >= col
    tril = causal.astype(F32)
    lane = lax.broadcasted_iota(jnp.int32, (1, LANES), 1)
    for j in range(nb):
        lf = _log_gates(g_ref[j], bi_ref, bf_ref)
        bc = jnp.dot(tril, lf, preferred_element_type=F32, precision=lax.Precision.HIGHEST)
        lft = lf.T
        bct = bc.T
        m_out = jnp.zeros((1, LANES), F32)
        for h in range(MH):
            i_row = lft[h:h + 1, :]
            b_row = bct[MH + h:MH + h + 1, :]
            i_col = lf[:, h:h + 1]
            b_col = bc[:, MH + h:MH + h + 1]
            m_prev = m_scr[j * MH + h][:, 0:1]
            log_d = jnp.where(causal, i_row + b_col - b_row, -jnp.inf)
            m_inter = m_prev + b_col
            m_t = jnp.maximum(m_inter, jnp.max(log_d, axis=-1, keepdims=True))
            d = jnp.exp(log_d - m_t)
            a_inter = jnp.exp(m_inter - m_t)
            q = q_ref[j, :, h * MDK:(h + 1) * MDK]
            k = k_ref[j, :, h * MDK:(h + 1) * MDK] * (MDK ** -0.5)
            qb = q.astype(BF16)
            vb = v_ref[j, :, h * MDV:(h + 1) * MDV].astype(BF16)
            w = lax.dot_general(qb, k.astype(BF16), (((1,), (1,)), ((), ())), preferred_element_type=F32) * d
            c_old = c_ref[j, h]
            n_old = n_ref[j, h:h + 1, :]
            num = (jnp.dot(w.astype(BF16), vb, preferred_element_type=F32)
                   + jnp.dot(qb, c_old.astype(BF16), preferred_element_type=F32) * a_inter)
            den = jnp.sum(w, axis=-1, keepdims=True) + a_inter * jnp.sum(q * n_old, axis=-1, keepdims=True)
            den = jnp.maximum(jnp.abs(den), jnp.exp(-m_t))
            hh = num / den
            m_new = m_t[ln - 1:ln, :]
            b_last = b_col[ln - 1:ln, :]
            decay = jnp.exp(i_col + b_last - b_col - m_new)
            carry = jnp.exp(m_prev + b_last - m_new)
            kd = k * decay
            c_ref[j, h] = carry * c_old + jnp.dot(kd.T.astype(BF16), vb, preferred_element_type=F32)
            n_ref[j, h:h + 1, :] = carry * n_old + jnp.sum(kd, axis=0, keepdims=True)
            m_scr[j * MH + h] = jnp.broadcast_to(m_new, (1, LANES))
            m_out = jnp.where(lane == h, m_new, m_out)
            hn = (_rms(hh, ng_ref[:, h * MDV:(h + 1) * MDV])
                  * jax.nn.sigmoid(o_ref[j, :, h * MDV:(h + 1) * MDV]))
            hm_ref[j, :, h * MDV:(h + 1) * MDV] = hn.astype(hm_ref.dtype)
        m_ref[j] = m_out


def _mlstm_prompt(proj, off, b, s, b_i, b_f, ng):
    ln = MLSTM_L
    nc = s // ln
    nb = 1
    mq_b, mk_b = off["mq"] // MQK_W, off["mk"] // MQK_W
    mv_b, mo_b, gt_b = off["mv"] // MV_W, off["mo"] // MV_W, off["gt"] // LANES
    proj3 = proj.reshape(b, s, proj.shape[1])
    smem = pl.BlockSpec(memory_space=pltpu.SMEM)
    col = lambda width, cb: pl.BlockSpec((nb, ln, width), lambda bi, ci: (bi, ci, cb))
    hm, c, n_state, m = pl.pallas_call(
        _mlstm_prompt_kernel,
        grid=(b // nb, nc),
        in_specs=[smem, smem, col(MQK_W, mq_b), col(MQK_W, mk_b), col(MV_W, mv_b), col(MV_W, mo_b),
                  col(LANES, gt_b), pl.BlockSpec((1, MV_W), lambda bi, ci: (0, 0))],
        out_specs=[col(MV_W, 0),
                   pl.BlockSpec((nb, MH, MDK, MDV), lambda bi, ci: (bi, 0, 0, 0)),
                   pl.BlockSpec((nb, MH, MDK), lambda bi, ci: (bi, 0, 0)),
                   pl.BlockSpec((nb, 1, LANES), lambda bi, ci: (bi, 0, 0))],
        out_shape=[jax.ShapeDtypeStruct((b, s, MV_W), BF16),
                   jax.ShapeDtypeStruct((b, MH, MDK, MDV), F32),
                   jax.ShapeDtypeStruct((b, MH, MDK), F32),
                   jax.ShapeDtypeStruct((b, 1, LANES), F32)],
        scratch_shapes=[pltpu.VMEM((nb * MH, 1, LANES), F32)],
        compiler_params=_cparams(("arbitrary", "arbitrary")),
        name="mlstm_prompt",
    )(b_i, b_f, proj3, proj3, proj3, proj3, proj3, ng)
    return hm.reshape(b * s, MV_W), c, n_state, m


def _mlstm_sample_kernel(bi_ref, bf_ref, q_ref, k_ref, v_ref, o_ref, g_ref, ng_ref, c0_ref, n0_ref, m0_ref,
                         hm_ref, c_ref, n_ref, m_ref):
    tb = q_ref.shape[0]
    lf = _log_gates(g_ref[...], bi_ref, bf_ref)
    lane = lax.broadcasted_iota(jnp.int32, (1, LANES), 1)
    m_out = jnp.zeros((tb, LANES), F32)
    for h in range(MH):
        li = lf[:, h:h + 1]
        lfg = lf[:, MH + h:MH + h + 1]
        m_prev = m0_ref[:, h:h + 1]
        m_inter = m_prev + lfg
        m_t = jnp.maximum(m_inter, li)
        d = jnp.exp(li - m_t)
        a = jnp.exp(m_inter - m_t)
        q = q_ref[:, h * MDK:(h + 1) * MDK]
        k = k_ref[:, h * MDK:(h + 1) * MDK] * (MDK ** -0.5)
        v = v_ref[:, h * MDV:(h + 1) * MDV]
        n_old = n0_ref[:, h, :]
        w = jnp.sum(q * k, axis=-1, keepdims=True) * d
        den = w + a * jnp.sum(q * n_old, axis=-1, keepdims=True)
        den = jnp.maximum(jnp.abs(den), jnp.exp(-m_t))
        dk = k * d
        qt = q.T
        dkt = dk.T
        rows = []
        for b in range(tb):
            c_old = c0_ref[b, h]
            qc = jnp.sum(c_old * qt[:, b:b + 1], axis=0, keepdims=True)
            a_b = a[b:b + 1, :]
            vrow = v[b:b + 1, :]
            rows.append((w[b:b + 1, :] * vrow + qc * a_b) / den[b:b + 1, :])
            c_ref[b, h] = a_b * c_old + dkt[:, b:b + 1] * vrow
        hh = jnp.concatenate(rows, axis=0)
        n_ref[:, h, :] = a * n_old + dk
        m_out = jnp.where(lane == h, m_t, m_out)
        hn = _rms(hh, ng_ref[:, h * MDV:(h + 1) * MDV]) * jax.nn.sigmoid(o_ref[:, h * MDV:(h + 1) * MDV])
        hm_ref[:, h * MDV:(h + 1) * MDV] = hn.astype(hm_ref.dtype)
    m_ref[...] = m_out


def _mlstm_sample(proj, off, bs, b_i, b_f, ng, c0, n0, m0):
    tb = 8
    mq_b, mk_b = off["mq"] // MQK_W, off["mk"] // MQK_W
    mv_b, mo_b, gt_b = off["mv"] // MV_W, off["mo"] // MV_W, off["gt"] // LANES
    smem = pl.BlockSpec(memory_space=pltpu.SMEM)
    return pl.pallas_call(
        _mlstm_sample_kernel,
        grid=(bs // tb,),
        in_specs=[smem, smem,
                  pl.BlockSpec((tb, MQK_W), lambda i: (i, mq_b)),
                  pl.BlockSpec((tb, MQK_W), lambda i: (i, mk_b)),
                  pl.BlockSpec((tb, MV_W), lambda i: (i, mv_b)),
                  pl.BlockSpec((tb, MV_W), lambda i: (i, mo_b)),
                  pl.BlockSpec((tb, LANES), lambda i: (i, gt_b)),
                  pl.BlockSpec((1, MV_W), lambda i: (0, 0)),
                  pl.BlockSpec((tb, MH, MDK, MDV), lambda i: (i, 0, 0, 0)),
                  pl.BlockSpec((tb, MH, MDK), lambda i: (i, 0, 0)),
                  pl.BlockSpec((tb, MH), lambda i: (i, 0))],
        out_specs=[pl.BlockSpec((tb, MV_W), lambda i: (i, 0)),
                   pl.BlockSpec((tb, MH, MDK, MDV), lambda i: (i, 0, 0, 0)),
                   pl.BlockSpec((tb, MH, MDK), lambda i: (i, 0, 0)),
                   pl.BlockSpec((tb, LANES), lambda i: (i, 0))],
        out_shape=[jax.ShapeDtypeStruct((bs, MV_W), BF16),
                   jax.ShapeDtypeStruct((bs, MH, MDK, MDV), F32),
                   jax.ShapeDtypeStruct((bs, MH, MDK), F32),
                   jax.ShapeDtypeStruct((bs, LANES), F32)],
        compiler_params=_cparams(("arbitrary",)),
        name="mlstm_sample",
    )(b_i, b_f, proj, proj, proj, proj, proj, ng, c0, n0, m0)


def _postmix_kernel(n_exp, n_main, *refs):
    if n_main is None:
        _postmix_tile(n_exp, *refs)
        return
    tile_refs, u2s_ref, out_refs = refs[:15], refs[15], refs[16:]
    i = pl.program_id(0)

    @pl.when(i < n_main)
    def _():
        _postmix_tile(n_exp, *tile_refs, *out_refs)

    @pl.when(i == n_main)
    def _():
        out_refs[1][0:u2s_ref.shape[0], :] = u2s_ref[...]


def _postmix_tile(n_exp, ha_ref, hm_ref, ga_ref, gm_ref, x_ref, g1_ref, sh2_ref, sc2_ref,
                  wa_ref, wm_ref, wo_ref, gf_ref, wrh_ref, wrl_ref, br_ref, x1_ref, u2_ref, ti_ref, tg_ref):
    a = jnp.dot(ha_ref[...], wa_ref[...], preferred_element_type=F32)
    m = jnp.dot(hm_ref[...], wm_ref[...], preferred_element_type=F32)
    merged = jax.nn.sigmoid(ga_ref[...]) * a + jax.nn.sigmoid(gm_ref[...]) * m
    y = jnp.dot(merged.astype(BF16), wo_ref[...], preferred_element_type=F32)
    x1 = x_ref[...] + g1_ref[...] * y
    x1_ref[...] = x1
    u2 = _rms(x1, gf_ref[...]) * (1.0 + sc2_ref[...]) + sh2_ref[...]
    u2_ref[...] = u2
    u2h = u2.astype(BF16)
    u2l = (u2 - u2h.astype(F32)).astype(BF16)
    logits = (jnp.dot(u2h, wrh_ref[...], preferred_element_type=F32)
              + (jnp.dot(u2l, wrh_ref[...], preferred_element_type=F32)
                 + jnp.dot(u2h, wrl_ref[...], preferred_element_type=F32))) + br_ref[...]
    lane = lax.broadcasted_iota(jnp.int32, logits.shape, 1)
    lanef = lane.astype(F32)
    work = jnp.where(lane < n_exp, logits, -jnp.inf)
    vals, idxs = [], []
    for _ in range(TOP_K):
        mx = jnp.max(work, axis=-1, keepdims=True)
        am = jnp.min(jnp.where(work == mx, lanef, float(LANES)), axis=-1, keepdims=True)
        vals.append(mx)
        idxs.append(am)
        work = jnp.where(lanef == am, -jnp.inf, work)
    es = [jnp.exp(v - vals[0]) for v in vals]
    tot = es[0] + es[1] + es[2] + es[3]
    ti = jnp.zeros(logits.shape, F32)
    tg = jnp.zeros(logits.shape, F32)
    for kk in range(TOP_K):
        ti = jnp.where(lane == kk, idxs[kk], ti)
        tg = jnp.where(lane == kk, es[kk] / tot, tg)
    ti_ref[...] = ti.astype(jnp.int32)
    tg_ref[...] = tg


def _postmix(ha, hm, proj, off, x2d, mod_block, mod_idx, mod, wa, wm, wo, gf, wr, br, n_exp, tm, name, u2_tail=None):
    n, d = x2d.shape
    nt = n // tm
    ga_b, gm_b = off["ga"] // d, off["gm"] // d
    const = lambda shape: pl.BlockSpec(shape, lambda i: (0,) * len(shape), pipeline_mode=pl.Buffered(1))
    ci = (lambda i: i) if u2_tail is None else (lambda i: jnp.minimum(i, nt - 1))
    row = lambda w, cb=0: pl.BlockSpec((tm, w), lambda i: (ci(i), cb))
    mspec = lambda k: pl.BlockSpec(mod_block, lambda i: mod_idx(ci(i), k))
    in_specs = [row(AQ_W), row(MV_W), row(d, ga_b), row(d, gm_b), row(d),
                mspec(2), mspec(3), mspec(4),
                const((AQ_W, d)), const((MV_W, d)), const((d, d)),
                const((1, d)), const((d, LANES)), const((d, LANES)), const((1, LANES))]
    args = [ha, hm, proj, proj, x2d, mod, mod, mod, wa, wm, wo, gf, wr[0], wr[1], br]
    n_u2 = n
    if u2_tail is not None:
        assert u2_tail.shape[0] <= tm
        in_specs.append(const(u2_tail.shape))
        args.append(u2_tail)
        n_u2 = n + u2_tail.shape[0]
    return pl.pallas_call(
        functools.partial(_postmix_kernel, n_exp, None if u2_tail is None else nt),
        grid=(nt if u2_tail is None else nt + 1,),
        in_specs=in_specs,
        out_specs=[row(d), pl.BlockSpec((tm, d), lambda i: (i, 0)), row(LANES), row(LANES)],
        out_shape=[jax.ShapeDtypeStruct((n, d), F32), jax.ShapeDtypeStruct((n_u2, d), F32),
                   jax.ShapeDtypeStruct((n, LANES), jnp.int32), jax.ShapeDtypeStruct((n, LANES), F32)],
        compiler_params=_cparams(("arbitrary",)),
        name=name,
    )(*args)


def _moe_kernel(n_s, n_c, se_ref, sf_ref, sn_ref, sr_ref, nu_ref, tokn_ref, u2_ref, wu_ref, wd_ref, bu_ref,
                bd_ref, yb_ref, xg, xb, acc, wub, wdp, wdb, act_carry, zbuf, gsem, wsem, zsem):
    s = pl.program_id(0)
    c = pl.program_id(1)
    sub = MOE_SUB
    ns = sn_ref[s]
    ns_prev = sn_ref[jnp.maximum(s - 1, 0)]
    fb = sf_ref[s]
    grp = MOE_GATHER_GROUP
    groups = (sr_ref[s] + grp - 1) // grp
    groups_next = jnp.where(s + 1 < n_s, (sr_ref[jnp.minimum(s + 1, n_s - 1)] + grp - 1) // grp, 0)

    def gather_group(g, carry):
        base = pl.multiple_of(g * grp, grp)
        for i in range(grp):
            j = base + i
            pltpu.make_async_copy(u2_ref.at[pl.ds(tokn_ref[0, j], 1)], xg.at[pl.ds(j, 1)], gsem).start()
        return carry

    def wb_copy(m, blk):
        return pltpu.make_async_copy(acc.at[pl.ds(pl.multiple_of(m * sub, sub), sub)],
                                     yb_ref.at[pl.ds(pl.multiple_of(blk * sub, sub), sub)], wsem.at[m])

    @pl.when((s == 0) & (c == 0))
    def _():
        xg[...] = jnp.zeros_like(xg)
        wdb[...] = jnp.zeros_like(wdb)
        act_carry[...] = jnp.zeros_like(act_carry)

    @pl.when(c == 0)
    def _():
        def wait_group(i, carry):
            pltpu.make_async_copy(u2_ref.at[pl.ds(0, grp)], xg.at[pl.ds(0, grp)], gsem).wait()
            return carry
        lax.fori_loop(0, groups, wait_group, 0)
        for m in range(MOE_NSUB):
            @pl.when((s > 0) & (m < ns_prev))
            def _():
                wb_copy(m, 0).wait()

        def prep(m, carry):
            r0 = pl.multiple_of(m * sub, sub)
            xb[pl.ds(r0, sub), :] = xg[pl.ds(r0, sub), :].astype(BF16)
            acc[pl.ds(r0, sub), :] = jnp.broadcast_to(bd_ref[...], (sub, acc.shape[1]))
            return carry
        lax.fori_loop(0, ns, prep, 0)

    def down(m, act):
        r0 = pl.multiple_of(m * sub, sub)
        pw = 4 * LANES
        for p in range(wdb.shape[1] // pw):
            cols = slice(p * pw, (p + 1) * pw)
            acc[pl.ds(r0, sub), cols] += jnp.dot(act, wdb[:, cols], preferred_element_type=F32)

    @pl.when(ns > 0)
    def _():
        down(ns - 1, act_carry[...])
        wub[...] = wu_ref[...].astype(BF16)
        half = LANES // 2
        for cb in range(wd_ref.shape[1] // LANES):
            cols = slice(cb * LANES, (cb + 1) * LANES)
            for g in range(wd_ref.shape[0] // LANES):
                wdp[cb, pl.ds(g * LANES, half, stride=2), :] = wd_ref[g * LANES:g * LANES + half, cols]
                wdp[cb, pl.ds(g * LANES + 1, half, stride=2), :] = wd_ref[g * LANES + half:(g + 1) * LANES, cols]
            wdb[:, cols] = wdp[cb].astype(BF16)

    def block(m, act_prev):
        r0 = pl.multiple_of(m * sub, sub)
        x = xb[pl.ds(r0, sub), :]
        gw = 2 * LANES
        hs = [jnp.dot(x, wub[:, g * gw:(g + 1) * gw], preferred_element_type=F32) + bu_ref[:, g * gw:(g + 1) * gw]
              for g in range(wub.shape[1] // gw)]
        if act_prev is not None:
            down(jnp.maximum(m - 1, 0), act_prev)
        even = lax.broadcasted_iota(jnp.int32, (sub, LANES), 1) % 2 == 0
        parts = []
        for h in hs:
            h0 = h[:, :LANES]
            h1 = h[:, LANES:]
            glu = jnp.where(even, h0, pltpu.roll(h1, 1, 1))
            lin = jnp.where(even, pltpu.roll(h0, LANES - 1, 1), h1)
            glu = jnp.minimum(glu, SWIGLU_LIMIT)
            lin = jnp.clip(lin, -SWIGLU_LIMIT, SWIGLU_LIMIT)
            parts.append(glu * jax.nn.sigmoid(SWIGLU_ALPHA * glu) * (lin + 1.0))
        return jnp.concatenate(parts, axis=1).astype(BF16)

    def wb_after(m):
        @pl.when((c == n_c - 1) & (m > 0))
        def _():
            wb_copy(m - 1, fb + m - 1).start()

    def pair(p, act_prev):
        act = block(2 * p + 1, block(2 * p, act_prev))
        wb_after(2 * p)
        wb_after(2 * p + 1)
        return act

    def single(m, act_prev):
        act = block(m, act_prev)
        wb_after(m)
        return act

    def first_pair():
        act = block(1, block(0, None))
        wb_after(1)
        return act

    act_last = lax.cond(ns >= 2, first_pair, lambda: jnp.zeros(act_carry.shape, BF16))
    act_last = lax.fori_loop(1, ns // 2, pair, act_last)
    act_last = lax.fori_loop(2 * (ns // 2), ns, single, act_last)

    @pl.when((ns > 0) & (c < n_c - 1))
    def _():
        act_carry[...] = act_last

    @pl.when((ns > 0) & (c == n_c - 1))
    def _():
        down(ns - 1, act_last)
        wb_copy(ns - 1, fb + ns - 1).start()
        act_carry[...] = jnp.zeros_like(act_carry)

    per_step = (groups_next + n_c - 1) // n_c
    lax.fori_loop(jnp.minimum(c * per_step, groups_next), jnp.minimum((c + 1) * per_step, groups_next),
                  gather_group, 0)

    @pl.when((s == n_s - 1) & (c == n_c - 1))
    def _():
        n_blocks = yb_ref.shape[0] // sub
        zbuf[...] = jnp.zeros_like(zbuf)

        def zstart(b, carry):
            pltpu.make_async_copy(zbuf, yb_ref.at[pl.ds(pl.multiple_of(b * sub, sub), sub)], zsem).start()
            return carry

        def zwait(b, carry):
            pltpu.make_async_copy(zbuf, yb_ref.at[pl.ds(0, sub)], zsem).wait()
            return carry
        lax.fori_loop(nu_ref[0], n_blocks, zstart, 0)
        lax.fori_loop(nu_ref[0], n_blocks, zwait, 0)


def _moe(u2, tabs, n_blocks, w_up, b_up, w_down, b_down):
    sb_e, sb_fb, sb_ns, sb_rows, n_used, sb_tok = tabs
    n_s = sb_e.shape[0]
    n_exp, d, de2 = w_up.shape
    de = de2 // 2
    hc = min(MOE_HC, de)
    n_c = de // hc
    rmax = MOE_NSUB * MOE_SUB
    assert rmax % MOE_GATHER_GROUP == 0 and hc % LANES == 0

    def chunk(s, c, sn):
        return jnp.where(sn[s] > 0, c, jnp.where(s == 0, 0, n_c - 1))

    grid_spec = pltpu.PrefetchScalarGridSpec(
        num_scalar_prefetch=5,
        grid=(n_s, n_c),
        in_specs=[
            pl.BlockSpec((None, 1, rmax), lambda s, c, se, sf, sn, sr, nu: (s + 1, 0, 0), memory_space=pltpu.SMEM),
            pl.BlockSpec(memory_space=pl.ANY),
            pl.BlockSpec((None, d, 2 * hc), lambda s, c, se, sf, sn, sr, nu: (se[s], 0, chunk(s, c, sn))),
            pl.BlockSpec((None, hc, d), lambda s, c, se, sf, sn, sr, nu: (se[s], chunk(s, c, sn), 0)),
            pl.BlockSpec((None, 1, 2 * hc), lambda s, c, se, sf, sn, sr, nu: (se[s], 0, chunk(s, c, sn))),
            pl.BlockSpec((None, 1, d), lambda s, c, se, sf, sn, sr, nu: (se[s], 0, 0)),
        ],
        out_specs=pl.BlockSpec(memory_space=pl.ANY),
        scratch_shapes=[pltpu.VMEM((rmax, d), F32), pltpu.VMEM((rmax, d), BF16), pltpu.VMEM((rmax, d), F32),
                        pltpu.VMEM((d, 2 * hc), BF16), pltpu.VMEM((d // LANES, hc, LANES), F32),
                        pltpu.VMEM((hc, d), BF16), pltpu.VMEM((MOE_SUB, hc), BF16),
                        pltpu.VMEM((MOE_SUB, d), F32),
                        pltpu.SemaphoreType.DMA, pltpu.SemaphoreType.DMA((MOE_NSUB,)), pltpu.SemaphoreType.DMA],
    )
    return pl.pallas_call(
        functools.partial(_moe_kernel, n_s, n_c),
        grid_spec=grid_spec,
        out_shape=jax.ShapeDtypeStruct((n_blocks * MOE_SUB, d), F32),
        compiler_params=_cparams(("arbitrary", "arbitrary")),
        name="moe_experts",
    )(sb_e, sb_fb, sb_ns, sb_rows, n_used, sb_tok, u2, w_up, w_down, b_up.reshape(n_exp, 1, de2),
      b_down.reshape(n_exp, 1, d))


def _combine_kernel(pc_ref, pn_ref, yb_ref, x1_ref, g2_ref, tg_ref, o_ref, buf, sem):
    s = pl.program_id(0)
    ns = pl.num_programs(0)
    tm = x1_ref.shape[0]

    def issue(p_ref, slot):
        def body(t, carry):
            for kk in range(TOP_K):
                pltpu.make_async_copy(yb_ref.at[pl.ds(p_ref[0, t * TOP_K + kk], 1)],
                                      buf.at[slot, kk, pl.ds(t, 1)], sem.at[slot]).start()
            return carry
        lax.fori_loop(0, tm, body, 0)

    def drain(slot):
        for kk in range(TOP_K):
            pltpu.make_async_copy(yb_ref.at[pl.ds(0, tm)], buf.at[slot, kk], sem.at[slot]).wait()

    slot = s % 2

    @pl.when(s == 0)
    def _():
        issue(pc_ref, 0)

    @pl.when(s + 1 < ns)
    def _():
        issue(pn_ref, 1 - slot)

    drain(slot)
    tg = tg_ref[...]
    acc = tg[:, 0:1] * buf[slot, 0]
    for kk in range(1, TOP_K):
        acc = acc + tg[:, kk:kk + 1] * buf[slot, kk]
    o_ref[...] = x1_ref[...] + g2_ref[...] * acc


def _combine(pos, yb, x1, g2_spec, g2_arg, tg, tm, name):
    n, d = x1.shape
    ns = n // tm
    pos3 = pos.reshape(ns, 1, tm * TOP_K)
    return pl.pallas_call(
        _combine_kernel,
        grid=(ns,),
        in_specs=[pl.BlockSpec((None, 1, tm * TOP_K), lambda s: (s, 0, 0), memory_space=pltpu.SMEM),
                  pl.BlockSpec((None, 1, tm * TOP_K), lambda s: (jnp.minimum(s + 1, ns - 1), 0, 0),
                               memory_space=pltpu.SMEM),
                  pl.BlockSpec(memory_space=pl.ANY),
                  pl.BlockSpec((tm, d), lambda s: (s, 0)),
                  g2_spec,
                  pl.BlockSpec((tm, LANES), lambda s: (s, 0))],
        out_specs=pl.BlockSpec((tm, d), lambda s: (s, 0)),
        out_shape=jax.ShapeDtypeStruct((n, d), F32),
        scratch_shapes=[pltpu.VMEM((2, TOP_K, tm, d), F32), pltpu.SemaphoreType.DMA((2,))],
        compiler_params=_cparams(("arbitrary",)),
        name=name,
    )(pos3, pos3, yb, x1, g2_arg, tg)


def _routing(top_e, n_exp, n_blocks):
    sub, nsub = MOE_SUB, MOE_NSUB
    rmax = sub * nsub
    i32 = jnp.int32
    nk = top_e.size
    flat_e = top_e.reshape(nk)
    oh = (flat_e[:, None] == jnp.arange(n_exp, dtype=i32)[None, :]).astype(i32)
    csum = jnp.cumsum(oh, axis=0)
    rank = jnp.sum((csum - oh) * oh, axis=1)
    counts = csum[-1]
    nblk = (counts + sub - 1) // sub
    blk_end = jnp.cumsum(nblk)
    blk_start = blk_end - nblk
    pos = jnp.sum(oh * blk_start[None, :], axis=1) * sub + rank
    n_used = blk_end[-1]
    nsb = (nblk + nsub - 1) // nsub
    sb_end = jnp.cumsum(nsb)
    sb_start = sb_end - nsb
    n_sb = sb_end[-1]
    n_real = n_exp + n_blocks // nsub + 1
    sidx = jnp.arange(n_real, dtype=i32)
    e_of = jnp.minimum(jnp.sum((sidx[:, None] >= sb_end[None, :]).astype(i32), axis=1), n_exp - 1)
    oh_s = (e_of[:, None] == jnp.arange(n_exp, dtype=i32)[None, :]).astype(i32)
    k_in = sidx - jnp.sum(oh_s * sb_start[None, :], axis=1)
    fb = jnp.sum(oh_s * blk_start[None, :], axis=1) + nsub * k_in
    ns = jnp.clip(jnp.sum(oh_s * nblk[None, :], axis=1) - nsub * k_in, 0, nsub) * (sidx < n_sb)
    last_e = jnp.sum(jnp.where(sidx == n_sb - 1, e_of, 0))
    e_of = jnp.where(sidx < n_sb, e_of, last_e)
    zero = jnp.zeros((1,), i32)
    sb_e = jnp.concatenate([e_of[:1], e_of, last_e.reshape(1)]).astype(i32)
    sb_fb = jnp.concatenate([zero, fb, zero]).astype(i32)
    sb_ns = jnp.concatenate([zero, ns, zero]).astype(i32)
    rows = jnp.clip(jnp.sum(oh_s * counts[None, :], axis=1) - rmax * k_in, 0, rmax) * (sidx < n_sb)
    sb_rows = jnp.concatenate([zero, rows, zero]).astype(i32)
    slot = 1 + jnp.sum(oh * sb_start[None, :], axis=1) + rank // rmax
    tok = (jnp.arange(nk, dtype=i32) // TOP_K).astype(i32)
    flat = (slot * rmax + rank % rmax).astype(i32)
    sb_tok = jnp.zeros(((n_real + 3) * rmax,), i32).at[flat].set(tok, unique_indices=True)
    tabs = (sb_e, sb_fb, sb_ns, sb_rows, n_used.reshape(1).astype(i32), sb_tok.reshape(n_real + 3, 1, rmax))
    return pos.astype(i32), tabs


def kernel(x_prompt, x_sample, cache_k, cache_v, state_C, state_n, state_m, c_prompt, c_sample, w_ada, b_ada,
           g_mix, w_in, b_igate, b_fgate, q_norm_g, k_norm_g, attn_sinks, mlstm_norm_g, w_attn_up, w_mlstm_up,
           w_out, g_ffn, w_router, b_router, w_up, b_up, w_down, b_down):
    bp, sp, d = x_prompt.shape
    bs = x_sample.shape[0]
    assert x_sample.shape[1] == 1 and w_ada.shape[0] == 1
    n_p = bp * sp
    win = cache_k.shape[2]
    n_exp = w_router.shape[2]
    de = w_down.shape[2]
    off, _ = _col_layout(d)

    w_r = _winprep(jnp.transpose(w_in[0]), d)
    wa =w_attn_up[0].astype(BF16)
    wm = w_mlstm_up[0].astype(BF16)
    wo = w_out[0].astype(BF16)
    wr32 = jnp.pad(w_router[0], ((0, 0), (0, LANES - n_exp)))
    wrh = wr32.astype(BF16)
    wr = (wrh, (wr32 - wrh.astype(F32)).astype(BF16))
    br = jnp.pad(b_router[0], (0, LANES - n_exp)).reshape(1, LANES)
    gmix = g_mix[0].reshape(1, d)
    gffn = g_ffn[0].reshape(1, d)
    qg = q_norm_g[0].reshape(1, HD)
    kg = k_norm_g[0].reshape(1, HD)
    ng = mlstm_norm_g[0].reshape(1, MV_W)
    sinks = attn_sinks[0]
    hs = jnp.stack([jnp.broadcast_to(jnp.asarray(_SLOPES, F32)[:, None], (ATTN_HEADS, LANES)),
                    jnp.broadcast_to(sinks[:, None], (ATTN_HEADS, LANES))])

    mod = _adaln(jnp.concatenate([c_prompt, c_sample], axis=0), w_ada[0], b_ada[0])
    mod_p = mod[:bp].reshape(bp, 1, N_MOD * d)
    mod_s = mod[bp:]

    xs = x_sample.reshape(bs, d)
    sspec2 = lambda kk: pl.BlockSpec((bs, d), lambda i, j: (0, kk))
    sspec1 = lambda kk: pl.BlockSpec((bs, d), lambda i: (0, kk))
    proj_s = _inproj(xs, gmix, (sspec2(0), sspec2(1)), (mod_s, mod_s), w_r, bs, "inproj_sample")
    q3 = proj_s[:, off["aq"]:off["aq"] + AQ_W].reshape(bs, ATTN_HEADS, HD)
    kn3 = proj_s[:, off["ak"]:off["ak"] + AKV_W].reshape(bs, 1, AKV_W)
    vn3 = proj_s[:, off["av"]:off["av"] + AKV_W].reshape(bs, 1, AKV_W)
    feature_major = lambda cache: jnp.swapaxes(cache[0].reshape(bs, win, AKV_W), 1, 2)
    ha_s3, kt_s, vt_s = _attn_sample(q3, kn3, vn3, feature_major(cache_k), feature_major(cache_v), qg, kg, hs)
    k_s, v_s = jnp.swapaxes(kt_s, 1, 2), jnp.swapaxes(vt_s, 1, 2)
    hm_s, c_s, n_state_s, m_s = _mlstm_sample(proj_s, off, bs, b_igate[0], b_fgate[0], ng,
                                              state_C[0], state_n[0], state_m[0])
    x1_s, u2_s, ti_s, tg_s = _postmix(
        ha_s3.reshape(bs, AQ_W), hm_s, proj_s, off, xs, (bs, d), lambda i, k: (0, k), mod_s,
        wa, wm, wo, gffn, wr, br, n_exp, bs, "postmix_sample")

    xp = x_prompt.reshape(n_p, d)
    tm_in = min(1024, sp)
    tps = sp // tm_in
    pspec = lambda t, kk, nargs: pl.BlockSpec(
        (None, 1, d), (lambda i, j: (i // t, 0, kk)) if nargs == 2 else (lambda i: (i // t, 0, kk)))
    proj_p = _inproj(xp, gmix, (pspec(tps, 0, 2), pspec(tps, 1, 2)), (mod_p, mod_p), w_r, tm_in, "inproj_prompt")
    ha_p, kn_p = _attn_prompt(proj_p, off, bp, sp, sinks, qg, kg)
    hm_p, c_p, n_state_p, m_p = _mlstm_prompt(proj_p, off, bp, sp, b_igate[0], b_fgate[0], ng)
    tm_pm = min(256, sp)
    tpp = sp // tm_pm
    x1_p, u2_all, ti_p, tg_p = _postmix(
        ha_p, hm_p, proj_p, off, xp, (None, 1, d), lambda i, k: (i // tpp, 0, k), mod_p,
        wa, wm, wo, gffn, wr, br, n_exp, tm_pm, "postmix_prompt", u2_tail=u2_s)

    nk = (n_p + bs) * TOP_K
    n_blocks = -(-nk // MOE_SUB) + n_exp
    top_e = jnp.concatenate([ti_p[:, :TOP_K], ti_s[:, :TOP_K]], axis=0)
    pos, tabs = _routing(top_e, n_exp, n_blocks)
    yb = _moe(u2_all, tabs, n_blocks, w_up[0], b_up[0], w_down[0], b_down[0])
    y_p = _combine(pos[:n_p * TOP_K], yb, x1_p, pspec(tpp, 5, 1), mod_p, tg_p, tm_pm, "combine_prompt")
    y_s = _combine(pos[n_p * TOP_K:], yb, x1_s, sspec1(5), mod_s, tg_s, bs, "combine_sample")

    kvshape = (1, bp, win, ATTN_KV, HD)
    k_p = kn_p.reshape(bp, sp, AKV_W)[:, sp - win:].reshape(kvshape)
    v_p = proj_p[:, off["av"]:off["av"] + AKV_W].reshape(bp, sp, AKV_W)[:, sp - win:].reshape(kvshape)
    return (y_p.reshape(bp, sp, d), y_s.reshape(bs, 1, d),
            k_p, v_p, c_p[None], n_state_p[None], m_p[:, 0, :MH][None],
            k_s.reshape(1, bs, win, ATTN_KV, HD), v_s.reshape(1, bs, win, ATTN_KV, HD),
            c_s[None], n_state_s[None], m_s[:, :MH][None])
```

```python
import functools

import numpy as np
import jax
import jax.numpy as jnp
from jax import lax
from jax.experimental import pallas as pl
from jax.experimental.pallas import tpu as pltpu

F32 = jnp.float32
BF16 = jnp.bfloat16

ATTN_HEADS = 16
ATTN_KV = 4
HD = 64
GQA = ATTN_HEADS // ATTN_KV
MH = 4
MDK = 128
MDV = 256
TOP_K = 4
N_MOD = 6
NORM_EPS = 1e-6
SWIGLU_LIMIT = 7.0
SWIGLU_ALPHA = 1.702

AQ_W = ATTN_HEADS * HD
AKV_W = ATTN_KV * HD
MQK_W = MH * MDK
MV_W = MH * MDV
GATE_PAD = 512

LANES = 128
MOE_SUB = 256
MOE_NSUB = 5
MOE_HC = 256
MOE_GATHER_GROUP = 8
MLSTM_L = 128
VMEM_LIMIT = 56 * 1024 * 1024

_SLOPES = [float(np.exp2(np.float32(-8.0 * (h + 1) / ATTN_HEADS))) for h in range(ATTN_HEADS)]


def _cparams(sem):
    return pltpu.CompilerParams(dimension_semantics=sem, vmem_limit_bytes=VMEM_LIMIT)


def _col_layout(d):
    off = {}
    o = 0
    for name, w in (("ga", d), ("gm", d), ("aq", AQ_W), ("mv", MV_W), ("mo", MV_W), ("ak", AKV_W),
                    ("av", AKV_W), ("mq", MQK_W), ("mk", MQK_W), ("gt", GATE_PAD)):
        off[name] = o
        o += w
    return off, o


def _rms(x, g):
    return x * lax.rsqrt(jnp.mean(x * x, axis=-1, keepdims=True) + NORM_EPS) * g


def _adaln_kernel(c_ref, w_ref, b_ref, o_ref):
    c = c_ref[...]
    s = (c * jax.nn.sigmoid(c)).astype(BF16)
    o_ref[...] = jnp.dot(s, w_ref[...].astype(BF16), preferred_element_type=F32) + b_ref[...]


def _adaln(c_all, w_ada, b_ada):
    r, d = c_all.shape
    w = w_ada.shape[1]
    tn = 1024
    return pl.pallas_call(
        _adaln_kernel,
        grid=(w // tn,),
        in_specs=[pl.BlockSpec((r, d), lambda j: (0, 0)),
                  pl.BlockSpec((d, tn), lambda j: (0, j)),
                  pl.BlockSpec((1, tn), lambda j: (0, j))],
        out_specs=pl.BlockSpec((r, tn), lambda j: (0, j)),
        out_shape=jax.ShapeDtypeStruct((r, w), F32),
        compiler_params=_cparams(("arbitrary",)),
        name="adaln_mod",
    )(c_all, w_ada, b_ada.reshape(1, w))


WINPREP_TR = 256


def _winprep_kernel(n_units, src_ref, nv_ref, wt_ref, o_ref, buf, sem):
    i = pl.program_id(0)
    tr = buf.shape[1]

    def fetch(step, slot):
        start = pl.multiple_of(src_ref[step], 8)
        return pltpu.make_async_copy(wt_ref.at[pl.ds(start, tr)], buf.at[slot], sem.at[slot])

    @pl.when(i == 0)
    def _():
        fetch(0, 0).start()

    @pl.when(i + 1 < n_units)
    def _():
        fetch(i + 1, (i + 1) % 2).start()

    fetch(i, i % 2).wait()
    row = lax.broadcasted_iota(jnp.int32, (tr, 1), 0)
    x = jnp.where(row < nv_ref[i], buf[i % 2], 0.0)
    o_ref[...] = x.T.astype(o_ref.dtype)


def _winprep(w_in_t, d):
    off, wtot = _col_layout(d)
    tr = WINPREP_TR
    src = dict(aq=0, ak=AQ_W, av=AQ_W + AKV_W, mq=AQ_W + 2 * AKV_W, mk=AQ_W + 2 * AKV_W + MQK_W,
               mv=AQ_W + 2 * AKV_W + 2 * MQK_W, mo=AQ_W + 2 * AKV_W + 2 * MQK_W + MV_W)
    src["gt"] = src["mo"] + MV_W
    src["ga"] = src["gt"] + 2 * MH
    src["gm"] = src["ga"] + d
    width = dict(aq=AQ_W, ak=AKV_W, av=AKV_W, mq=MQK_W, mk=MQK_W, mv=MV_W, mo=MV_W, ga=d, gm=d)
    n_units = wtot // tr
    starts = np.zeros((n_units,), np.int32)
    valid = np.zeros((n_units,), np.int32)
    for name, w in width.items():
        for u in range(w // tr):
            starts[off[name] // tr + u] = src[name] + u * tr
            valid[off[name] // tr + u] = tr
    starts[off["gt"] // tr] = src["gt"]
    valid[off["gt"] // tr] = 2 * MH
    assert int(starts.max()) + tr <= w_in_t.shape[0] and all(s % 8 == 0 for s in starts)
    grid_spec = pltpu.PrefetchScalarGridSpec(
        num_scalar_prefetch=2,
        grid=(n_units,),
        in_specs=[pl.BlockSpec(memory_space=pl.ANY)],
        out_specs=pl.BlockSpec((d, tr), lambda i, s, v: (0, i)),
        scratch_shapes=[pltpu.VMEM((2, tr, d), F32), pltpu.SemaphoreType.DMA((2,))],
    )
    return pl.pallas_call(
        functools.partial(_winprep_kernel, n_units),
        grid_spec=grid_spec,
        out_shape=jax.ShapeDtypeStruct((d, wtot), BF16),
        compiler_params=_cparams(("arbitrary",)),
        name="winprep",
    )(jnp.asarray(starts), jnp.asarray(valid), w_in_t)


def _inproj_kernel(x_ref, g_ref, sh_ref, sc_ref, w_ref, o_ref, u_scr):
    @pl.when(pl.program_id(1) == 0)
    def _():
        y = _rms(x_ref[...], g_ref[...])
        u_scr[...] = (y * (1.0 + sc_ref[...]) + sh_ref[...]).astype(BF16)

    o_ref[...] = jnp.dot(u_scr[...], w_ref[...], preferred_element_type=F32)


def _inproj(x2d, g, mod_specs, mod_args, w_r, tm, name):
    n, d = x2d.shape
    w = w_r.shape[1]
    tn = 1024
    return pl.pallas_call(
        _inproj_kernel,
        grid=(n // tm, w // tn),
        in_specs=[pl.BlockSpec((tm, d), lambda i, j: (i, 0)),
                  pl.BlockSpec((1, d), lambda i, j: (0, 0)),
                  mod_specs[0], mod_specs[1],
                  pl.BlockSpec((d, tn), lambda i, j: (0, j))],
        out_specs=pl.BlockSpec((tm, tn), lambda i, j: (i, j)),
        out_shape=jax.ShapeDtypeStruct((n, w), F32),
        scratch_shapes=[pltpu.VMEM((tm, d), BF16)],
        compiler_params=_cparams(("arbitrary", "arbitrary")),
        name=name,
    )(x2d, g, mod_args[0], mod_args[1], w_r)


def _attn_prompt_kernel(sink_ref, q_ref, kc_ref, vc_ref, vp_ref, bias_ref, bd_ref, qg_ref, kg_ref, o_ref, kn_ref,
                        kprev):
    n = pl.program_id(1)
    lq = q_ref.shape[0]

    @pl.when(n == 0)
    def _():
        kprev[...] = jnp.zeros_like(kprev)

    def head_rms(x, g):
        x2 = x * x
        hi = x2.astype(BF16)
        lo = (x2 - hi.astype(F32)).astype(BF16)
        bd = bd_ref[0:x.shape[1], 0:x.shape[1]]
        ss = jnp.dot(hi, bd, preferred_element_type=F32) + jnp.dot(lo, bd, preferred_element_type=F32)
        return x * lax.rsqrt(ss * (1.0 / HD) + NORM_EPS) * g

    qn = head_rms(q_ref[...], qg_ref[...] * (HD ** -0.5))
    kn = head_rms(kc_ref[...], kg_ref[...])
    kn_ref[...] = kn
    grp = lax.broadcasted_iota(jnp.int32, (GQA * lq, 1), 0) // lq
    ones_col = jnp.where(lax.broadcasted_iota(jnp.int32, (2 * lq, HD), 1) == 0, 1.0, 0.0)
    for h in range(ATTN_KV):
        sl = slice(h * HD, (h + 1) * HD)
        kctx = jnp.concatenate([kprev[:, sl], kn[:, sl]], axis=0).astype(BF16)
        v = jnp.concatenate([vp_ref[:, sl], vc_ref[:, sl]], axis=0)
        vext = jnp.concatenate([v, ones_col], axis=1).astype(BF16)
        q4 = jnp.concatenate([qn[:, (h * GQA + g) * HD:(h * GQA + g + 1) * HD] for g in range(GQA)],
                             axis=0).astype(BF16)
        s = lax.dot_general(q4, kctx, (((1,), (1,)), ((), ())), preferred_element_type=F32) + bias_ref[h]
        sink = jnp.zeros((GQA * lq, 1), F32)
        for g in range(GQA):
            sink = jnp.where(grp == g, sink_ref[h * GQA + g], sink)
        m = jnp.maximum(jnp.max(s, axis=-1, keepdims=True), sink)
        p = jnp.exp(s - m).astype(BF16)
        oe = jnp.dot(p, vext, preferred_element_type=F32)
        o = oe[:, :HD] / (oe[:, HD:HD + 1] + jnp.exp(sink - m))
        for g in range(GQA):
            hq = h * GQA + g
            o_ref[:, hq * HD:(hq + 1) * HD] = o[g * lq:(g + 1) * lq].astype(o_ref.dtype)
    kprev[...] = kn


def _attn_bias(lq):
    t = jnp.arange(lq)[:, None]
    j = jnp.arange(2 * lq)[None, :]
    dist = lq + t - j
    inwin = (dist >= 0) & (dist < lq)
    slopes = jnp.asarray(_SLOPES, F32).reshape(ATTN_KV, GQA, 1, 1)
    b = -slopes * dist.astype(F32)[None, None]
    variants = [jnp.where((inwin & (j >= lq))[None, None], b, -jnp.inf), jnp.where(inwin[None, None], b, -jnp.inf)]
    return jnp.stack(variants).reshape(2, ATTN_KV, GQA * lq, 2 * lq)


def _attn_prompt(proj, off, b, s, sinks, qg, kg):
    lq = 128
    nq = s // lq
    n = b * s
    aq_b, ak_b, av_b = off["aq"] // AQ_W, off["ak"] // AKV_W, off["av"] // AKV_W
    cur = lambda bi, ni: bi * nq + ni
    prv = lambda bi, ni: bi * nq + jnp.maximum(ni - 1, 0)
    hid = jnp.arange(AQ_W) // HD
    same_head = (hid[:, None] == hid[None, :]).astype(BF16)
    return pl.pallas_call(
        _attn_prompt_kernel,
        grid=(b, nq),
        in_specs=[pl.BlockSpec(memory_space=pltpu.SMEM),
                  pl.BlockSpec((lq, AQ_W), lambda bi, ni: (cur(bi, ni), aq_b)),
                  pl.BlockSpec((lq, AKV_W), lambda bi, ni: (cur(bi, ni), ak_b)),
                  pl.BlockSpec((lq, AKV_W), lambda bi, ni: (cur(bi, ni), av_b)),
                  pl.BlockSpec((lq, AKV_W), lambda bi, ni: (prv(bi, ni), av_b)),
                  pl.BlockSpec((None, ATTN_KV, GQA * lq, 2 * lq), lambda bi, ni: (jnp.minimum(ni, 1), 0, 0, 0)),
                  pl.BlockSpec((AQ_W, AQ_W), lambda bi, ni: (0, 0)),
                  pl.BlockSpec((1, AQ_W), lambda bi, ni: (0, 0)),
                  pl.BlockSpec((1, AKV_W), lambda bi, ni: (0, 0))],
        out_specs=[pl.BlockSpec((lq, AQ_W), lambda bi, ni: (cur(bi, ni), 0)),
                   pl.BlockSpec((lq, AKV_W), lambda bi, ni: (cur(bi, ni), 0))],
        out_shape=[jax.ShapeDtypeStruct((n, AQ_W), BF16), jax.ShapeDtypeStruct((n, AKV_W), F32)],
        scratch_shapes=[pltpu.VMEM((lq, AKV_W), F32)],
        compiler_params=_cparams(("arbitrary", "arbitrary")),
        name="attn_prompt",
    )(sinks, proj, proj, proj, proj, _attn_bias(lq), same_head,
      jnp.tile(qg, (1, ATTN_HEADS)), jnp.tile(kg, (1, ATTN_KV)))


def _attn_sample_kernel(q_ref, kn_ref, vn_ref, knt_ref, vnt_ref, ckt_ref, cvt_ref, qg_ref, kg_ref, kgt_ref, hs_ref,
                        o_ref, okt_ref, ovt_ref):
    tb, nh, _ = q_ref.shape
    win = ckt_ref.shape[2]
    q = _rms(q_ref[...], qg_ref[...])
    hq_i = lax.broadcasted_iota(jnp.int32, (1, nh, HD), 1)
    qbd = jnp.concatenate([jnp.where(hq_i // GQA == kv, q, 0.0) for kv in range(ATTN_KV)], axis=-1)
    kn = kn_ref[...]
    knt = knt_ref[...]
    lane = lax.broadcasted_iota(jnp.int32, (1, 1, AKV_W), 2)
    srow = lax.broadcasted_iota(jnp.int32, (1, AKV_W, 1), 1)
    rs = jnp.zeros_like(kn)
    rst = jnp.zeros_like(knt)
    for kv in range(ATTN_KV):
        msk = lane // HD == kv
        ms = jnp.sum(jnp.where(msk, kn * kn, 0.0), axis=-1, keepdims=True) * (1.0 / HD)
        rs = jnp.where(msk, lax.rsqrt(ms + NORM_EPS), rs)
        mskt = srow // HD == kv
        mst = jnp.sum(jnp.where(mskt, knt * knt, 0.0), axis=1, keepdims=True) * (1.0 / HD)
        rst = jnp.where(mskt, lax.rsqrt(mst + NORM_EPS), rst)
    knn = kn * rs * kg_ref[...][None]
    knnt = knt * rst * kgt_ref[...][None]
    vn = vn_ref[...]
    ckt = ckt_ref[...]
    cvt = cvt_ref[...]
    jj = lax.broadcasted_iota(jnp.int32, (1, 1, win), 2)
    okt_ref[...] = jnp.where(jj == win - 1, knnt, pltpu.roll(ckt, win - 1, 2))
    ovt_ref[...] = jnp.where(jj == win - 1, vnt_ref[...], pltpu.roll(cvt, win - 1, 2))
    slope = hs_ref[0][None]
    sink = hs_ref[1][None][:, :, 0:1]
    s = jnp.einsum("bhc,bcj->bhj", qbd.astype(BF16), ckt.astype(BF16), preferred_element_type=F32) * (HD ** -0.5)
    s = jnp.where(jj >= 1, s - slope * (win - jj).astype(F32), -jnp.inf)
    s_new = jnp.sum(qbd * knn, axis=-1, keepdims=True) * (HD ** -0.5)
    m = jnp.maximum(jnp.maximum(jnp.max(s, axis=-1, keepdims=True), s_new), sink)
    p = jnp.exp(s - m)
    p_new = jnp.exp(s_new - m)
    den = jnp.sum(p, axis=-1, keepdims=True) + p_new + jnp.exp(sink - m)
    of = jnp.einsum("bhj,bcj->bhc", p.astype(BF16), cvt.astype(BF16), preferred_element_type=F32) + p_new * vn
    of = of / den
    o = jnp.zeros((tb, nh, HD), F32)
    for kv in range(ATTN_KV):
        o = o + jnp.where(hq_i // GQA == kv, of[:, :, kv * HD:(kv + 1) * HD], 0.0)
    o_ref[...] = o.astype(o_ref.dtype)


def _attn_sample(q3, kn3, vn3, ckt, cvt, qg, kg, hs):
    bs, nh, _ = q3.shape
    win = ckt.shape[2]
    tb = 16
    seq3 = lambda a, b: pl.BlockSpec((tb, a, b), lambda i: (i, 0, 0))
    kg4 = jnp.tile(kg, (1, ATTN_KV))
    return pl.pallas_call(
        _attn_sample_kernel,
        grid=(bs // tb,),
        in_specs=[seq3(nh, HD), seq3(1, AKV_W), seq3(1, AKV_W), seq3(AKV_W, 1), seq3(AKV_W, 1),
                  seq3(AKV_W, win), seq3(AKV_W, win),
                  pl.BlockSpec((1, 1, HD), lambda i: (0, 0, 0)),
                  pl.BlockSpec((1, AKV_W), lambda i: (0, 0)),
                  pl.BlockSpec((AKV_W, 1), lambda i: (0, 0)),
                  pl.BlockSpec((2, nh, LANES), lambda i: (0, 0, 0))],
        out_specs=[seq3(nh, HD), seq3(AKV_W, win), seq3(AKV_W, win)],
        out_shape=[jax.ShapeDtypeStruct((bs, nh, HD), BF16),
                   jax.ShapeDtypeStruct((bs, AKV_W, win), F32),
                   jax.ShapeDtypeStruct((bs, AKV_W, win), F32)],
        compiler_params=_cparams(("arbitrary",)),
        name="attn_sample",
    )(q3, kn3, vn3, kn3.reshape(bs, AKV_W, 1), vn3.reshape(bs, AKV_W, 1), ckt, cvt,
      qg.reshape(1, 1, HD), kg4, kg4.reshape(AKV_W, 1), hs)


def _log_gates(g_pre, bi_ref, bf_ref):
    lane = lax.broadcasted_iota(jnp.int32, (1, LANES), 1)
    bias = jnp.zeros((1, LANES), F32)
    for h in range(MH):
        bias = jnp.where(lane == h, bi_ref[h], bias)
        bias = jnp.where(lane == MH + h, bf_ref[h], bias)
    pre = g_pre + bias
    logsig = jnp.minimum(pre, 0.0) - jnp.log1p(jnp.exp(-jnp.abs(pre)))
    return jnp.where(lane < MH, pre, logsig)


def _mlstm_prompt_kernel(bi_ref, bf_ref, q_ref, k_ref, v_ref, o_ref, g_ref, ng_ref,
                         hm_ref, c_ref, n_ref, m_ref, m_scr):
    ci = pl.program_id(1)
    nb, ln = q_ref.shape[0], q_ref.shape[1]

    @pl.when(ci == 0)
    def _():
        c_ref[...] = jnp.zeros_like(c_ref)
        n_ref[...] = jnp.zeros_like(n_ref)
        m_scr[...] = jnp.zeros_like(m_scr)

    row = lax.broadcasted_iota(jnp.int32, (ln, ln), 0)
    col = lax.broadcasted_iota(jnp.int32, (ln, ln), 1)
    causal = row >= col
    tril = causal.astype(F32)
    lane = lax.broadcasted_iota(jnp.int32, (1, LANES), 1)
    for j in range(nb):
        lf = _log_gates(g_ref[j], bi_ref, bf_ref)
        bc = jnp.dot(tril, lf, preferred_element_type=F32, precision=lax.Precision.HIGHEST)
        lft = lf.T
        bct = bc.T
        m_out = jnp.zeros((1, LANES), F32)
        for h in range(MH):
            i_row = lft[h:h + 1, :]
            b_row = bct[MH + h:MH + h + 1, :]
            i_col = lf[:, h:h + 1]
            b_col = bc[:, MH + h:MH + h + 1]
            m_prev = m_scr[j * MH + h][:, 0:1]
            log_d = jnp.where(causal, i_row + b_col - b_row, -jnp.inf)
            m_inter = m_prev + b_col
            m_t = jnp.maximum(m_inter, jnp.max(log_d, axis=-1, keepdims=True))
            d = jnp.exp(log_d - m_t)
            a_inter = jnp.exp(m_inter - m_t)
            q = q_ref[j, :, h * MDK:(h + 1) * MDK]
            k = k_ref[j, :, h * MDK:(h + 1) * MDK] * (MDK ** -0.5)
            qb = q.astype(BF16)
            vb = v_ref[j, :, h * MDV:(h + 1) * MDV].astype(BF16)
            w = lax.dot_general(qb, k.astype(BF16), (((1,), (1,)), ((), ())), preferred_element_type=F32) * d
            c_old = c_ref[j, h]
            n_old = n_ref[j, h:h + 1, :]
            num = (jnp.dot(w.astype(BF16), vb, preferred_element_type=F32)
                   + jnp.dot(qb, c_old.astype(BF16), preferred_element_type=F32) * a_inter)
            den = jnp.sum(w, axis=-1, keepdims=True) + a_inter * jnp.sum(q * n_old, axis=-1, keepdims=True)
            den = jnp.maximum(jnp.abs(den), jnp.exp(-m_t))
            hh = num / den
            m_new = m_t[ln - 1:ln, :]
            b_last = b_col[ln - 1:ln, :]
            decay = jnp.exp(i_col + b_last - b_col - m_new)
            carry = jnp.exp(m_prev + b_last - m_new)
            kd = k * decay
            c_ref[j, h] = carry * c_old + jnp.dot(kd.T.astype(BF16), vb, preferred_element_type=F32)
            n_ref[j, h:h + 1, :] = carry * n_old + jnp.sum(kd, axis=0, keepdims=True)
            m_scr[j * MH + h] = jnp.broadcast_to(m_new, (1, LANES))
            m_out = jnp.where(lane == h, m_new, m_out)
            hn = (_rms(hh, ng_ref[:, h * MDV:(h + 1) * MDV])
                  * jax.nn.sigmoid(o_ref[j, :, h * MDV:(h + 1) * MDV]))
            hm_ref[j, :, h * MDV:(h + 1) * MDV] = hn.astype(hm_ref.dtype)
        m_ref[j] = m_out


def _mlstm_prompt(proj, off, b, s, b_i, b_f, ng):
    ln = MLSTM_L
    nc = s // ln
    nb = 1
    mq_b, mk_b = off["mq"] // MQK_W, off["mk"] // MQK_W
    mv_b, mo_b, gt_b = off["mv"] // MV_W, off["mo"] // MV_W, off["gt"] // LANES
    proj3 = proj.reshape(b, s, proj.shape[1])
    smem = pl.BlockSpec(memory_space=pltpu.SMEM)
    col = lambda width, cb: pl.BlockSpec((nb, ln, width), lambda bi, ci: (bi, ci, cb))
    hm, c, n_state, m = pl.pallas_call(
        _mlstm_prompt_kernel,
        grid=(b // nb, nc),
        in_specs=[smem, smem, col(MQK_W, mq_b), col(MQK_W, mk_b), col(MV_W, mv_b), col(MV_W, mo_b),
                  col(LANES, gt_b), pl.BlockSpec((1, MV_W), lambda bi, ci: (0, 0))],
        out_specs=[col(MV_W, 0),
                   pl.BlockSpec((nb, MH, MDK, MDV), lambda bi, ci: (bi, 0, 0, 0)),
                   pl.BlockSpec((nb, MH, MDK), lambda bi, ci: (bi, 0, 0)),
                   pl.BlockSpec((nb, 1, LANES), lambda bi, ci: (bi, 0, 0))],
        out_shape=[jax.ShapeDtypeStruct((b, s, MV_W), BF16),
                   jax.ShapeDtypeStruct((b, MH, MDK, MDV), F32),
                   jax.ShapeDtypeStruct((b, MH, MDK), F32),
                   jax.ShapeDtypeStruct((b, 1, LANES), F32)],
        scratch_shapes=[pltpu.VMEM((nb * MH, 1, LANES), F32)],
        compiler_params=_cparams(("arbitrary", "arbitrary")),
        name="mlstm_prompt",
    )(b_i, b_f, proj3, proj3, proj3, proj3, proj3, ng)
    return hm.reshape(b * s, MV_W), c, n_state, m


def _mlstm_sample_kernel(bi_ref, bf_ref, q_ref, k_ref, v_ref, o_ref, g_ref, ng_ref, c0_ref, n0_ref, m0_ref,
                         hm_ref, c_ref, n_ref, m_ref):
    tb = q_ref.shape[0]
    lf = _log_gates(g_ref[...], bi_ref, bf_ref)
    lane = lax.broadcasted_iota(jnp.int32, (1, LANES), 1)
    m_out = jnp.zeros((tb, LANES), F32)
    for h in range(MH):
        li = lf[:, h:h + 1]
        lfg = lf[:, MH + h:MH + h + 1]
        m_prev = m0_ref[:, h:h + 1]
        m_inter = m_prev + lfg
        m_t = jnp.maximum(m_inter, li)
        d = jnp.exp(li - m_t)
        a = jnp.exp(m_inter - m_t)
        q = q_ref[:, h * MDK:(h + 1) * MDK]
        k = k_ref[:, h * MDK:(h + 1) * MDK] * (MDK ** -0.5)
        v = v_ref[:, h * MDV:(h + 1) * MDV]
        n_old = n0_ref[:, h, :]
        w = jnp.sum(q * k, axis=-1, keepdims=True) * d
        den = w + a * jnp.sum(q * n_old, axis=-1, keepdims=True)
        den = jnp.maximum(jnp.abs(den), jnp.exp(-m_t))
        dk = k * d
        qt = q.T
        dkt = dk.T
        rows = []
        for b in range(tb):
            c_old = c0_ref[b, h]
            qc = jnp.sum(c_old * qt[:, b:b + 1], axis=0, keepdims=True)
            a_b = a[b:b + 1, :]
            vrow = v[b:b + 1, :]
            rows.append((w[b:b + 1, :] * vrow + qc * a_b) / den[b:b + 1, :])
            c_ref[b, h] = a_b * c_old + dkt[:, b:b + 1] * vrow
        hh = jnp.concatenate(rows, axis=0)
        n_ref[:, h, :] = a * n_old + dk
        m_out = jnp.where(lane == h, m_t, m_out)
        hn = _rms(hh, ng_ref[:, h * MDV:(h + 1) * MDV]) * jax.nn.sigmoid(o_ref[:, h * MDV:(h + 1) * MDV])
        hm_ref[:, h * MDV:(h + 1) * MDV] = hn.astype(hm_ref.dtype)
    m_ref[...] = m_out


def _mlstm_sample(proj, off, bs, b_i, b_f, ng, c0, n0, m0):
    tb = 8
    mq_b, mk_b = off["mq"] // MQK_W, off["mk"] // MQK_W
    mv_b, mo_b, gt_b = off["mv"] // MV_W, off["mo"] // MV_W, off["gt"] // LANES
    smem = pl.BlockSpec(memory_space=pltpu.SMEM)
    return pl.pallas_call(
        _mlstm_sample_kernel,
        grid=(bs // tb,),
        in_specs=[smem, smem,
                  pl.BlockSpec((tb, MQK_W), lambda i: (i, mq_b)),
                  pl.BlockSpec((tb, MQK_W), lambda i: (i, mk_b)),
                  pl.BlockSpec((tb, MV_W), lambda i: (i, mv_b)),
                  pl.BlockSpec((tb, MV_W), lambda i: (i, mo_b)),
                  pl.BlockSpec((tb, LANES), lambda i: (i, gt_b)),
                  pl.BlockSpec((1, MV_W), lambda i: (0, 0)),
                  pl.BlockSpec((tb, MH, MDK, MDV), lambda i: (i, 0, 0, 0)),
                  pl.BlockSpec((tb, MH, MDK), lambda i: (i, 0, 0)),
                  pl.BlockSpec((tb, MH), lambda i: (i, 0))],
        out_specs=[pl.BlockSpec((tb, MV_W), lambda i: (i, 0)),
                   pl.BlockSpec((tb, MH, MDK, MDV), lambda i: (i, 0, 0, 0)),
                   pl.BlockSpec((tb, MH, MDK), lambda i: (i, 0, 0)),
                   pl.BlockSpec((tb, LANES), lambda i: (i, 0))],
        out_shape=[jax.ShapeDtypeStruct((bs, MV_W), BF16),
                   jax.ShapeDtypeStruct((bs, MH, MDK, MDV), F32),
                   jax.ShapeDtypeStruct((bs, MH, MDK), F32),
                   jax.ShapeDtypeStruct((bs, LANES), F32)],
        compiler_params=_cparams(("arbitrary",)),
        name="mlstm_sample",
    )(b_i, b_f, proj, proj, proj, proj, proj, ng, c0, n0, m0)


def _postmix_kernel(n_exp, n_main, *refs):
    if n_main is None:
        _postmix_tile(n_exp, *refs)
        return
    tile_refs, u2s_ref, out_refs = refs[:15], refs[15], refs[16:]
    i = pl.program_id(0)

    @pl.when(i < n_main)
    def _():
        _postmix_tile(n_exp, *tile_refs, *out_refs)

    @pl.when(i == n_main)
    def _():
        out_refs[1][0:u2s_ref.shape[0], :] = u2s_ref[...]


def _postmix_tile(n_exp, ha_ref, hm_ref, ga_ref, gm_ref, x_ref, g1_ref, sh2_ref, sc2_ref,
                  wa_ref, wm_ref, wo_ref, gf_ref, wrh_ref, wrl_ref, br_ref, x1_ref, u2_ref, ti_ref, tg_ref):
    a = jnp.dot(ha_ref[...], wa_ref[...], preferred_element_type=F32)
    m = jnp.dot(hm_ref[...], wm_ref[...], preferred_element_type=F32)
    merged = jax.nn.sigmoid(ga_ref[...]) * a + jax.nn.sigmoid(gm_ref[...]) * m
    y = jnp.dot(merged.astype(BF16), wo_ref[...], preferred_element_type=F32)
    x1 = x_ref[...] + g1_ref[...] * y
    x1_ref[...] = x1
    u2 = _rms(x1, gf_ref[...]) * (1.0 + sc2_ref[...]) + sh2_ref[...]
    u2_ref[...] = u2
    u2h = u2.astype(BF16)
    u2l = (u2 - u2h.astype(F32)).astype(BF16)
    logits = (jnp.dot(u2h, wrh_ref[...], preferred_element_type=F32)
              + (jnp.dot(u2l, wrh_ref[...], preferred_element_type=F32)
                 + jnp.dot(u2h, wrl_ref[...], preferred_element_type=F32))) + br_ref[...]
    lane = lax.broadcasted_iota(jnp.int32, logits.shape, 1)
    lanef = lane.astype(F32)
    work = jnp.where(lane < n_exp, logits, -jnp.inf)
    vals, idxs = [], []
    for _ in range(TOP_K):
        mx = jnp.max(work, axis=-1, keepdims=True)
        am = jnp.min(jnp.where(work == mx, lanef, float(LANES)), axis=-1, keepdims=True)
        vals.append(mx)
        idxs.append(am)
        work = jnp.where(lanef == am, -jnp.inf, work)
    es = [jnp.exp(v - vals[0]) for v in vals]
    tot = es[0] + es[1] + es[2] + es[3]
    ti = jnp.zeros(logits.shape, F32)
    tg = jnp.zeros(logits.shape, F32)
    for kk in range(TOP_K):
        ti = jnp.where(lane == kk, idxs[kk], ti)
        tg = jnp.where(lane == kk, es[kk] / tot, tg)
    ti_ref[...] = ti.astype(jnp.int32)
    tg_ref[...] = tg


def _postmix(ha, hm, proj, off, x2d, mod_block, mod_idx, mod, wa, wm, wo, gf, wr, br, n_exp, tm, name, u2_tail=None):
    n, d = x2d.shape
    nt = n // tm
    ga_b, gm_b = off["ga"] // d, off["gm"] // d
    const = lambda shape: pl.BlockSpec(shape, lambda i: (0,) * len(shape), pipeline_mode=pl.Buffered(1))
    ci = (lambda i: i) if u2_tail is None else (lambda i: jnp.minimum(i, nt - 1))
    row = lambda w, cb=0: pl.BlockSpec((tm, w), lambda i: (ci(i), cb))
    mspec = lambda k: pl.BlockSpec(mod_block, lambda i: mod_idx(ci(i), k))
    in_specs = [row(AQ_W), row(MV_W), row(d, ga_b), row(d, gm_b), row(d),
                mspec(2), mspec(3), mspec(4),
                const((AQ_W, d)), const((MV_W, d)), const((d, d)),
                const((1, d)), const((d, LANES)), const((d, LANES)), const((1, LANES))]
    args = [ha, hm, proj, proj, x2d, mod, mod, mod, wa, wm, wo, gf, wr[0], wr[1], br]
    n_u2 = n
    if u2_tail is not None:
        assert u2_tail.shape[0] <= tm
        in_specs.append(const(u2_tail.shape))
        args.append(u2_tail)
        n_u2 = n + u2_tail.shape[0]
    return pl.pallas_call(
        functools.partial(_postmix_kernel, n_exp, None if u2_tail is None else nt),
        grid=(nt if u2_tail is None else nt + 1,),
        in_specs=in_specs,
        out_specs=[row(d), pl.BlockSpec((tm, d), lambda i: (i, 0)), row(LANES), row(LANES)],
        out_shape=[jax.ShapeDtypeStruct((n, d), F32), jax.ShapeDtypeStruct((n_u2, d), F32),
                   jax.ShapeDtypeStruct((n, LANES), jnp.int32), jax.ShapeDtypeStruct((n, LANES), F32)],
        compiler_params=_cparams(("arbitrary",)),
        name=name,
    )(*args)


def _moe_kernel(n_s, n_c, se_ref, sf_ref, sn_ref, sr_ref, nu_ref, tokn_ref, u2_ref, wu_ref, wd_ref, bu_ref,
                bd_ref, yb_ref, xg, xb, acc, wub, wdp, wdb, act_carry, zbuf, gsem, wsem, zsem):
    s = pl.program_id(0)
    c = pl.program_id(1)
    sub = MOE_SUB
    ns = sn_ref[s]
    ns_prev = sn_ref[jnp.maximum(s - 1, 0)]
    fb = sf_ref[s]
    grp = MOE_GATHER_GROUP
    groups = (sr_ref[s] + grp - 1) // grp
    groups_next = jnp.where(s + 1 < n_s, (sr_ref[jnp.minimum(s + 1, n_s - 1)] + grp - 1) // grp, 0)

    def gather_group(g, carry):
        base = pl.multiple_of(g * grp, grp)
        for i in range(grp):
            j = base + i
            pltpu.make_async_copy(u2_ref.at[pl.ds(tokn_ref[0, j], 1)], xg.at[pl.ds(j, 1)], gsem).start()
        return carry

    def wb_copy(m, blk):
        return pltpu.make_async_copy(acc.at[pl.ds(pl.multiple_of(m * sub, sub), sub)],
                                     yb_ref.at[pl.ds(pl.multiple_of(blk * sub, sub), sub)], wsem.at[m])

    @pl.when((s == 0) & (c == 0))
    def _():
        xg[...] = jnp.zeros_like(xg)
        wdb[...] = jnp.zeros_like(wdb)
        act_carry[...] = jnp.zeros_like(act_carry)

    @pl.when(c == 0)
    def _():
        def wait_group(i, carry):
            pltpu.make_async_copy(u2_ref.at[pl.ds(0, grp)], xg.at[pl.ds(0, grp)], gsem).wait()
            return carry
        lax.fori_loop(0, groups, wait_group, 0)
        for m in range(MOE_NSUB):
            @pl.when((s > 0) & (m < ns_prev))
            def _():
                wb_copy(m, 0).wait()

        def prep(m, carry):
            r0 = pl.multiple_of(m * sub, sub)
            xb[pl.ds(r0, sub), :] = xg[pl.ds(r0, sub), :].astype(BF16)
            acc[pl.ds(r0, sub), :] = jnp.broadcast_to(bd_ref[...], (sub, acc.shape[1]))
            return carry
        lax.fori_loop(0, ns, prep, 0)

    def down(m, act):
        r0 = pl.multiple_of(m * sub, sub)
        pw = 4 * LANES
        for p in range(wdb.shape[1] // pw):
            cols = slice(p * pw, (p + 1) * pw)
            acc[pl.ds(r0, sub), cols] += jnp.dot(act, wdb[:, cols], preferred_element_type=F32)

    @pl.when(ns > 0)
    def _():
        down(ns - 1, act_carry[...])
        wub[...] = wu_ref[...].astype(BF16)
        half = LANES // 2
        for cb in range(wd_ref.shape[1] // LANES):
            cols = slice(cb * LANES, (cb + 1) * LANES)
            for g in range(wd_ref.shape[0] // LANES):
                wdp[cb, pl.ds(g * LANES, half, stride=2), :] = wd_ref[g * LANES:g * LANES + half, cols]
                wdp[cb, pl.ds(g * LANES + 1, half, stride=2), :] = wd_ref[g * LANES + half:(g + 1) * LANES, cols]
            wdb[:, cols] = wdp[cb].astype(BF16)

    def block(m, act_prev):
        r0 = pl.multiple_of(m * sub, sub)
        x = xb[pl.ds(r0, sub), :]
        gw = 2 * LANES
        hs = [jnp.dot(x, wub[:, g * gw:(g + 1) * gw], preferred_element_type=F32) + bu_ref[:, g * gw:(g + 1) * gw]
              for g in range(wub.shape[1] // gw)]
        if act_prev is not None:
            down(jnp.maximum(m - 1, 0), act_prev)
        even = lax.broadcasted_iota(jnp.int32, (sub, LANES), 1) % 2 == 0
        parts = []
        for h in hs:
            h0 = h[:, :LANES]
            h1 = h[:, LANES:]
            glu = jnp.where(even, h0, pltpu.roll(h1, 1, 1))
            lin = jnp.where(even, pltpu.roll(h0, LANES - 1, 1), h1)
            glu = jnp.minimum(glu, SWIGLU_LIMIT)
            lin = jnp.clip(lin, -SWIGLU_LIMIT, SWIGLU_LIMIT)
            parts.append(glu * jax.nn.sigmoid(SWIGLU_ALPHA * glu) * (lin + 1.0))
        return jnp.concatenate(parts, axis=1).astype(BF16)

    def wb_after(m):
        @pl.when((c == n_c - 1) & (m > 0))
        def _():
            wb_copy(m - 1, fb + m - 1).start()

    def single(m, act_prev):
        act = block(m, act_prev)
        wb_after(m)
        return act

    def first_pair():
        act = block(1, block(0, None))
        wb_after(1)
        return act

    def triple(i, act_prev):
        m0 = 2 + 3 * i
        act = block(m0 + 2, block(m0 + 1, block(m0, act_prev)))
        wb_after(m0)
        wb_after(m0 + 1)
        wb_after(m0 + 2)
        return act

    def pair_at(m0):
        def body(i, act_prev):
            act = block(m0 + 1, block(m0, act_prev))
            wb_after(m0)
            wb_after(m0 + 1)
            return act
        return body

    act_last = lax.cond(ns >= 2, first_pair, lambda: jnp.zeros(act_carry.shape, BF16))
    rest = jnp.maximum(ns - 2, 0)
    n3 = rest // 3
    after3 = 2 + 3 * n3
    n2 = (rest - 3 * n3) // 2
    act_last = lax.fori_loop(0, n3, triple, act_last)
    act_last = lax.fori_loop(0, n2, pair_at(after3), act_last)
    act_last = lax.fori_loop(jnp.where(ns >= 2, after3 + 2 * n2, 0), ns, single, act_last)

    @pl.when((ns > 0) & (c < n_c - 1))
    def _():
        act_carry[...] = act_last

    @pl.when((ns > 0) & (c == n_c - 1))
    def _():
        down(ns - 1, act_last)
        wb_copy(ns - 1, fb + ns - 1).start()
        act_carry[...] = jnp.zeros_like(act_carry)

    per_step = (groups_next + n_c - 1) // n_c
    lax.fori_loop(jnp.minimum(c * per_step, groups_next), jnp.minimum((c + 1) * per_step, groups_next),
                  gather_group, 0)

    @pl.when((s == n_s - 1) & (c == n_c - 1))
    def _():
        n_blocks = yb_ref.shape[0] // sub
        zbuf[...] = jnp.zeros_like(zbuf)

        def zstart(b, carry):
            pltpu.make_async_copy(zbuf, yb_ref.at[pl.ds(pl.multiple_of(b * sub, sub), sub)], zsem).start()
            return carry

        def zwait(b, carry):
            pltpu.make_async_copy(zbuf, yb_ref.at[pl.ds(0, sub)], zsem).wait()
            return carry
        lax.fori_loop(nu_ref[0], n_blocks, zstart, 0)
        lax.fori_loop(nu_ref[0], n_blocks, zwait, 0)


def _moe(u2, tabs, n_blocks, w_up, b_up, w_down, b_down):
    sb_e, sb_fb, sb_ns, sb_rows, n_used, sb_tok = tabs
    n_s = sb_e.shape[0]
    n_exp, d, de2 = w_up.shape
    de = de2 // 2
    hc = min(MOE_HC, de)
    n_c = de // hc
    rmax = MOE_NSUB * MOE_SUB
    assert rmax % MOE_GATHER_GROUP == 0 and hc % LANES == 0

    def chunk(s, c, sn):
        return jnp.where(sn[s] > 0, c, jnp.where(s == 0, 0, n_c - 1))

    grid_spec = pltpu.PrefetchScalarGridSpec(
        num_scalar_prefetch=5,
        grid=(n_s, n_c),
        in_specs=[
            pl.BlockSpec((None, 1, rmax), lambda s, c, se, sf, sn, sr, nu: (s + 1, 0, 0), memory_space=pltpu.SMEM),
            pl.BlockSpec(memory_space=pl.ANY),
            pl.BlockSpec((None, d, 2 * hc), lambda s, c, se, sf, sn, sr, nu: (se[s], 0, chunk(s, c, sn))),
            pl.BlockSpec((None, hc, d), lambda s, c, se, sf, sn, sr, nu: (se[s], chunk(s, c, sn), 0)),
            pl.BlockSpec((None, 1, 2 * hc), lambda s, c, se, sf, sn, sr, nu: (se[s], 0, chunk(s, c, sn))),
            pl.BlockSpec((None, 1, d), lambda s, c, se, sf, sn, sr, nu: (se[s], 0, 0)),
        ],
        out_specs=pl.BlockSpec(memory_space=pl.ANY),
        scratch_shapes=[pltpu.VMEM((rmax, d), F32), pltpu.VMEM((rmax, d), BF16), pltpu.VMEM((rmax, d), F32),
                        pltpu.VMEM((d, 2 * hc), BF16), pltpu.VMEM((d // LANES, hc, LANES), F32),
                        pltpu.VMEM((hc, d), BF16), pltpu.VMEM((MOE_SUB, hc), BF16),
                        pltpu.VMEM((MOE_SUB, d), F32),
                        pltpu.SemaphoreType.DMA, pltpu.SemaphoreType.DMA((MOE_NSUB,)), pltpu.SemaphoreType.DMA],
    )
    return pl.pallas_call(
        functools.partial(_moe_kernel, n_s, n_c),
        grid_spec=grid_spec,
        out_shape=jax.ShapeDtypeStruct((n_blocks * MOE_SUB, d), F32),
        compiler_params=_cparams(("arbitrary", "arbitrary")),
        name="moe_experts",
    )(sb_e, sb_fb, sb_ns, sb_rows, n_used, sb_tok, u2, w_up, w_down, b_up.reshape(n_exp, 1, de2),
      b_down.reshape(n_exp, 1, d))


def _combine_kernel(pc_ref, pn_ref, yb_ref, x1_ref, g2_ref, tg_ref, o_ref, buf, sem):
    s = pl.program_id(0)
    ns = pl.num_programs(0)
    tm = x1_ref.shape[0]

    def issue(p_ref, slot):
        def body(t, carry):
            for kk in range(TOP_K):
                pltpu.make_async_copy(yb_ref.at[pl.ds(p_ref[0, t * TOP_K + kk], 1)],
                                      buf.at[slot, kk, pl.ds(t, 1)], sem.at[slot]).start()
            return carry
        lax.fori_loop(0, tm, body, 0)

    def drain(slot):
        for kk in range(TOP_K):
            pltpu.make_async_copy(yb_ref.at[pl.ds(0, tm)], buf.at[slot, kk], sem.at[slot]).wait()

    slot = s % 2

    @pl.when(s == 0)
    def _():
        issue(pc_ref, 0)

    @pl.when(s + 1 < ns)
    def _():
        issue(pn_ref, 1 - slot)

    drain(slot)
    tg = tg_ref[...]
    acc = tg[:, 0:1] * buf[slot, 0]
    for kk in range(1, TOP_K):
        acc = acc + tg[:, kk:kk + 1] * buf[slot, kk]
    o_ref[...] = x1_ref[...] + g2_ref[...] * acc


def _combine(pos, yb, x1, g2_spec, g2_arg, tg, tm, name):
    n, d = x1.shape
    ns = n // tm
    pos3 = pos.reshape(ns, 1, tm * TOP_K)
    return pl.pallas_call(
        _combine_kernel,
        grid=(ns,),
        in_specs=[pl.BlockSpec((None, 1, tm * TOP_K), lambda s: (s, 0, 0), memory_space=pltpu.SMEM),
                  pl.BlockSpec((None, 1, tm * TOP_K), lambda s: (jnp.minimum(s + 1, ns - 1), 0, 0),
                               memory_space=pltpu.SMEM),
                  pl.BlockSpec(memory_space=pl.ANY),
                  pl.BlockSpec((tm, d), lambda s: (s, 0)),
                  g2_spec,
                  pl.BlockSpec((tm, LANES), lambda s: (s, 0))],
        out_specs=pl.BlockSpec((tm, d), lambda s: (s, 0)),
        out_shape=jax.ShapeDtypeStruct((n, d), F32),
        scratch_shapes=[pltpu.VMEM((2, TOP_K, tm, d), F32), pltpu.SemaphoreType.DMA((2,))],
        compiler_params=_cparams(("arbitrary",)),
        name=name,
    )(pos3, pos3, yb, x1, g2_arg, tg)


def _routing(top_e, n_exp, n_blocks):
    sub, nsub = MOE_SUB, MOE_NSUB
    rmax = sub * nsub
    i32 = jnp.int32
    nk = top_e.size
    flat_e = top_e.reshape(nk)
    oh = (flat_e[:, None] == jnp.arange(n_exp, dtype=i32)[None, :]).astype(i32)
    csum = jnp.cumsum(oh, axis=0)
    rank = jnp.sum((csum - oh) * oh, axis=1)
    counts = csum[-1]
    nblk = (counts + sub - 1) // sub
    blk_end = jnp.cumsum(nblk)
    blk_start = blk_end - nblk
    pos = jnp.sum(oh * blk_start[None, :], axis=1) * sub + rank
    n_used = blk_end[-1]
    nsb = (nblk + nsub - 1) // nsub
    sb_end = jnp.cumsum(nsb)
    sb_start = sb_end - nsb
    n_sb = sb_end[-1]
    n_real = n_exp + n_blocks // nsub + 1
    sidx = jnp.arange(n_real, dtype=i32)
    e_of = jnp.minimum(jnp.sum((sidx[:, None] >= sb_end[None, :]).astype(i32), axis=1), n_exp - 1)
    oh_s = (e_of[:, None] == jnp.arange(n_exp, dtype=i32)[None, :]).astype(i32)
    k_in = sidx - jnp.sum(oh_s * sb_start[None, :], axis=1)
    fb = jnp.sum(oh_s * blk_start[None, :], axis=1) + nsub * k_in
    ns = jnp.clip(jnp.sum(oh_s * nblk[None, :], axis=1) - nsub * k_in, 0, nsub) * (sidx < n_sb)
    last_e = jnp.sum(jnp.where(sidx == n_sb - 1, e_of, 0))
    e_of = jnp.where(sidx < n_sb, e_of, last_e)
    zero = jnp.zeros((1,), i32)
    sb_e = jnp.concatenate([e_of[:1], e_of, last_e.reshape(1)]).astype(i32)
    sb_fb = jnp.concatenate([zero, fb, zero]).astype(i32)
    sb_ns = jnp.concatenate([zero, ns, zero]).astype(i32)
    rows = jnp.clip(jnp.sum(oh_s * counts[None, :], axis=1) - rmax * k_in, 0, rmax) * (sidx < n_sb)
    sb_rows = jnp.concatenate([zero, rows, zero]).astype(i32)
    slot = 1 + jnp.sum(oh * sb_start[None, :], axis=1) + rank // rmax
    tok = (jnp.arange(nk, dtype=i32) // TOP_K).astype(i32)
    flat = (slot * rmax + rank % rmax).astype(i32)
    sb_tok = jnp.zeros(((n_real + 3) * rmax,), i32).at[flat].set(tok, unique_indices=True)
    tabs = (sb_e, sb_fb, sb_ns, sb_rows, n_used.reshape(1).astype(i32), sb_tok.reshape(n_real + 3, 1, rmax))
    return pos.astype(i32), tabs


def kernel(x_prompt, x_sample, cache_k, cache_v, state_C, state_n, state_m, c_prompt, c_sample, w_ada, b_ada,
           g_mix, w_in, b_igate, b_fgate, q_norm_g, k_norm_g, attn_sinks, mlstm_norm_g, w_attn_up, w_mlstm_up,
           w_out, g_ffn, w_router, b_router, w_up, b_up, w_down, b_down):
    bp, sp, d = x_prompt.shape
    bs = x_sample.shape[0]
    assert x_sample.shape[1] == 1 and w_ada.shape[0] == 1
    n_p = bp * sp
    win = cache_k.shape[2]
    n_exp = w_router.shape[2]
    de = w_down.shape[2]
    off, _ = _col_layout(d)

    w_r = _winprep(jnp.transpose(w_in[0]), d)
    wa =w_attn_up[0].astype(BF16)
    wm = w_mlstm_up[0].astype(BF16)
    wo = w_out[0].astype(BF16)
    wr32 = jnp.pad(w_router[0], ((0, 0), (0, LANES - n_exp)))
    wrh = wr32.astype(BF16)
    wr = (wrh, (wr32 - wrh.astype(F32)).astype(BF16))
    br = jnp.pad(b_router[0], (0, LANES - n_exp)).reshape(1, LANES)
    gmix = g_mix[0].reshape(1, d)
    gffn = g_ffn[0].reshape(1, d)
    qg = q_norm_g[0].reshape(1, HD)
    kg = k_norm_g[0].reshape(1, HD)
    ng = mlstm_norm_g[0].reshape(1, MV_W)
    sinks = attn_sinks[0]
    hs = jnp.stack([jnp.broadcast_to(jnp.asarray(_SLOPES, F32)[:, None], (ATTN_HEADS, LANES)),
                    jnp.broadcast_to(sinks[:, None], (ATTN_HEADS, LANES))])

    mod = _adaln(jnp.concatenate([c_prompt, c_sample], axis=0), w_ada[0], b_ada[0])
    mod_p = mod[:bp].reshape(bp, 1, N_MOD * d)
    mod_s = mod[bp:]

    xs = x_sample.reshape(bs, d)
    sspec2 = lambda kk: pl.BlockSpec((bs, d), lambda i, j: (0, kk))
    sspec1 = lambda kk: pl.BlockSpec((bs, d), lambda i: (0, kk))
    proj_s = _inproj(xs, gmix, (sspec2(0), sspec2(1)), (mod_s, mod_s), w_r, bs, "inproj_sample")
    q3 = proj_s[:, off["aq"]:off["aq"] + AQ_W].reshape(bs, ATTN_HEADS, HD)
    kn3 = proj_s[:, off["ak"]:off["ak"] + AKV_W].reshape(bs, 1, AKV_W)
    vn3 = proj_s[:, off["av"]:off["av"] + AKV_W].reshape(bs, 1, AKV_W)
    feature_major = lambda cache: jnp.swapaxes(cache[0].reshape(bs, win, AKV_W), 1, 2)
    ha_s3, kt_s, vt_s = _attn_sample(q3, kn3, vn3, feature_major(cache_k), feature_major(cache_v), qg, kg, hs)
    k_s, v_s = jnp.swapaxes(kt_s, 1, 2), jnp.swapaxes(vt_s, 1, 2)
    hm_s, c_s, n_state_s, m_s = _mlstm_sample(proj_s, off, bs, b_igate[0], b_fgate[0], ng,
                                              state_C[0], state_n[0], state_m[0])
    x1_s, u2_s, ti_s, tg_s = _postmix(
        ha_s3.reshape(bs, AQ_W), hm_s, proj_s, off, xs, (bs, d), lambda i, k: (0, k), mod_s,
        wa, wm, wo, gffn, wr, br, n_exp, bs, "postmix_sample")

    xp = x_prompt.reshape(n_p, d)
    tm_in = min(1024, sp)
    tps = sp // tm_in
    pspec = lambda t, kk, nargs: pl.BlockSpec(
        (None, 1, d), (lambda i, j: (i // t, 0, kk)) if nargs == 2 else (lambda i: (i // t, 0, kk)))
    proj_p = _inproj(xp, gmix, (pspec(tps, 0, 2), pspec(tps, 1, 2)), (mod_p, mod_p), w_r, tm_in, "inproj_prompt")
    ha_p, kn_p = _attn_prompt(proj_p, off, bp, sp, sinks, qg, kg)
    hm_p, c_p, n_state_p, m_p = _mlstm_prompt(proj_p, off, bp, sp, b_igate[0], b_fgate[0], ng)
    tm_pm = min(256, sp)
    tpp = sp // tm_pm
    x1_p, u2_all, ti_p, tg_p = _postmix(
        ha_p, hm_p, proj_p, off, xp, (None, 1, d), lambda i, k: (i // tpp, 0, k), mod_p,
        wa, wm, wo, gffn, wr, br, n_exp, tm_pm, "postmix_prompt", u2_tail=u2_s)

    nk = (n_p + bs) * TOP_K
    n_blocks = -(-nk // MOE_SUB) + n_exp
    top_e = jnp.concatenate([ti_p[:, :TOP_K], ti_s[:, :TOP_K]], axis=0)
    pos, tabs = _routing(top_e, n_exp, n_blocks)
    yb = _moe(u2_all, tabs, n_blocks, w_up[0], b_up[0], w_down[0], b_down[0])
    y_p = _combine(pos[:n_p * TOP_K], yb, x1_p, pspec(tpp, 5, 1), mod_p, tg_p, tm_pm, "combine_prompt")
    y_s = _combine(pos[n_p * TOP_K:], yb, x1_s, sspec1(5), mod_s, tg_s, bs, "combine_sample")

    kvshape = (1, bp, win, ATTN_KV, HD)
    k_p = kn_p.reshape(bp, sp, AKV_W)[:, sp - win:].reshape(kvshape)
    v_p = proj_p[:, off["av"]:off["av"] + AKV_W].reshape(bp, sp, AKV_W)[:, sp - win:].reshape(kvshape)
    return (y_p.reshape(bp, sp, d), y_s.reshape(bs, 1, d),
            k_p, v_p, c_p[None], n_state_p[None], m_p[:, 0, :MH][None],
            k_s.reshape(1, bs, win, ATTN_KV, HD), v_s.reshape(1, bs, win, ATTN_KV, HD),
            c_s[None], n_state_s[None], m_s[:, :MH][None])
```

```python
import functools

import numpy as np
import jax
import jax.numpy as jnp
from jax import lax
from jax.experimental import pallas as pl
from jax.experimental.pallas import tpu as pltpu

F32 = jnp.float32
BF16 = jnp.bfloat16

ATTN_HEADS = 16
ATTN_KV = 4
HD = 64
GQA = ATTN_HEADS // ATTN_KV
MH = 4
MDK = 128
MDV = 256
TOP_K = 4
N_MOD = 6
NORM_EPS = 1e-6
SWIGLU_LIMIT = 7.0
SWIGLU_ALPHA = 1.702

AQ_W = ATTN_HEADS * HD
AKV_W = ATTN_KV * HD
MQK_W = MH * MDK
MV_W = MH * MDV
GATE_PAD = 512

LANES = 128
MOE_SUB = 256
MOE_NSUB = 5
MOE_HC = 256
MOE_GATHER_GROUP = 8
MLSTM_L = 128
VMEM_LIMIT = 56 * 1024 * 1024

_SLOPES = [float(np.exp2(np.float32(-8.0 * (h + 1) / ATTN_HEADS))) for h in range(ATTN_HEADS)]


def _cparams(sem):
    return pltpu.CompilerParams(dimension_semantics=sem, vmem_limit_bytes=VMEM_LIMIT)


def _col_layout(d):
    off = {}
    o = 0
    for name, w in (("ga", d), ("gm", d), ("aq", AQ_W), ("mv", MV_W), ("mo", MV_W), ("ak", AKV_W),
                    ("av", AKV_W), ("mq", MQK_W), ("mk", MQK_W), ("gt", GATE_PAD)):
        off[name] = o
        o += w
    return off, o


def _rms(x, g):
    return x * lax.rsqrt(jnp.mean(x * x, axis=-1, keepdims=True) + NORM_EPS) * g


def _adaln_kernel(c_ref, w_ref, b_ref, o_ref):
    c = c_ref[...]
    s = (c * jax.nn.sigmoid(c)).astype(BF16)
    o_ref[...] = jnp.dot(s, w_ref[...].astype(BF16), preferred_element_type=F32) + b_ref[...]


def _adaln(c_all, w_ada, b_ada):
    r, d = c_all.shape
    w = w_ada.shape[1]
    tn = 1024
    return pl.pallas_call(
        _adaln_kernel,
        grid=(w // tn,),
        in_specs=[pl.BlockSpec((r, d), lambda j: (0, 0)),
                  pl.BlockSpec((d, tn), lambda j: (0, j)),
                  pl.BlockSpec((1, tn), lambda j: (0, j))],
        out_specs=pl.BlockSpec((r, tn), lambda j: (0, j)),
        out_shape=jax.ShapeDtypeStruct((r, w), F32),
        compiler_params=_cparams(("arbitrary",)),
        name="adaln_mod",
    )(c_all, w_ada, b_ada.reshape(1, w))


WINPREP_TR = 256


def _winprep_kernel(n_units, src_ref, nv_ref, wt_ref, o_ref, buf, sem):
    i = pl.program_id(0)
    tr = buf.shape[1]

    def fetch(step, slot):
        start = pl.multiple_of(src_ref[step], 8)
        return pltpu.make_async_copy(wt_ref.at[pl.ds(start, tr)], buf.at[slot], sem.at[slot])

    @pl.when(i == 0)
    def _():
        fetch(0, 0).start()

    @pl.when(i + 1 < n_units)
    def _():
        fetch(i + 1, (i + 1) % 2).start()

    fetch(i, i % 2).wait()
    row = lax.broadcasted_iota(jnp.int32, (tr, 1), 0)
    x = jnp.where(row < nv_ref[i], buf[i % 2], 0.0)
    o_ref[...] = x.T.astype(o_ref.dtype)


def _winprep(w_in_t, d):
    off, wtot = _col_layout(d)
    tr = WINPREP_TR
    src = dict(aq=0, ak=AQ_W, av=AQ_W + AKV_W, mq=AQ_W + 2 * AKV_W, mk=AQ_W + 2 * AKV_W + MQK_W,
               mv=AQ_W + 2 * AKV_W + 2 * MQK_W, mo=AQ_W + 2 * AKV_W + 2 * MQK_W + MV_W)
    src["gt"] = src["mo"] + MV_W
    src["ga"] = src["gt"] + 2 * MH
    src["gm"] = src["ga"] + d
    width = dict(aq=AQ_W, ak=AKV_W, av=AKV_W, mq=MQK_W, mk=MQK_W, mv=MV_W, mo=MV_W, ga=d, gm=d)
    n_units = wtot // tr
    starts = np.zeros((n_units,), np.int32)
    valid = np.zeros((n_units,), np.int32)
    for name, w in width.items():
        for u in range(w // tr):
            starts[off[name] // tr + u] = src[name] + u * tr
            valid[off[name] // tr + u] = tr
    starts[off["gt"] // tr] = src["gt"]
    valid[off["gt"] // tr] = 2 * MH
    assert int(starts.max()) + tr <= w_in_t.shape[0] and all(s % 8 == 0 for s in starts)
    grid_spec = pltpu.PrefetchScalarGridSpec(
        num_scalar_prefetch=2,
        grid=(n_units,),
        in_specs=[pl.BlockSpec(memory_space=pl.ANY)],
        out_specs=pl.BlockSpec((d, tr), lambda i, s, v: (0, i)),
        scratch_shapes=[pltpu.VMEM((2, tr, d), F32), pltpu.SemaphoreType.DMA((2,))],
    )
    return pl.pallas_call(
        functools.partial(_winprep_kernel, n_units),
        grid_spec=grid_spec,
        out_shape=jax.ShapeDtypeStruct((d, wtot), BF16),
        compiler_params=_cparams(("arbitrary",)),
        name="winprep",
    )(jnp.asarray(starts), jnp.asarray(valid), w_in_t)


def _inproj_kernel(x_ref, g_ref, sh_ref, sc_ref, w_ref, o_ref, u_scr):
    @pl.when(pl.program_id(1) == 0)
    def _():
        y = _rms(x_ref[...], g_ref[...])
        u_scr[...] = (y * (1.0 + sc_ref[...]) + sh_ref[...]).astype(BF16)

    o_ref[...] = jnp.dot(u_scr[...], w_ref[...], preferred_element_type=F32)


def _inproj(x2d, g, mod_specs, mod_args, w_r, tm, name):
    n, d = x2d.shape
    w = w_r.shape[1]
    tn = 1024
    return pl.pallas_call(
        _inproj_kernel,
        grid=(n // tm, w // tn),
        in_specs=[pl.BlockSpec((tm, d), lambda i, j: (i, 0)),
                  pl.BlockSpec((1, d), lambda i, j: (0, 0)),
                  mod_specs[0], mod_specs[1],
                  pl.BlockSpec((d, tn), lambda i, j: (0, j))],
        out_specs=pl.BlockSpec((tm, tn), lambda i, j: (i, j)),
        out_shape=jax.ShapeDtypeStruct((n, w), F32),
        scratch_shapes=[pltpu.VMEM((tm, d), BF16)],
        compiler_params=_cparams(("arbitrary", "arbitrary")),
        name=name,
    )(x2d, g, mod_args[0], mod_args[1], w_r)


def _attn_prompt_kernel(sink_ref, q_ref, kc_ref, vc_ref, vp_ref, bias_ref, bd_ref, qg_ref, kg_ref, o_ref, kn_ref,
                        kprev):
    n = pl.program_id(1)
    lq = q_ref.shape[0]

    @pl.when(n == 0)
    def _():
        kprev[...] = jnp.zeros_like(kprev)

    def head_rms(x, g):
        x2 = x * x
        hi = x2.astype(BF16)
        lo = (x2 - hi.astype(F32)).astype(BF16)
        bd = bd_ref[0:x.shape[1], 0:x.shape[1]]
        ss = jnp.dot(hi, bd, preferred_element_type=F32) + jnp.dot(lo, bd, preferred_element_type=F32)
        return x * lax.rsqrt(ss * (1.0 / HD) + NORM_EPS) * g

    qn = head_rms(q_ref[...], qg_ref[...] * (HD ** -0.5))
    kn = head_rms(kc_ref[...], kg_ref[...])
    kn_ref[...] = kn
    grp = lax.broadcasted_iota(jnp.int32, (GQA * lq, 1), 0) // lq
    ones_col = jnp.where(lax.broadcasted_iota(jnp.int32, (2 * lq, HD), 1) == 0, 1.0, 0.0)
    for h in range(ATTN_KV):
        sl = slice(h * HD, (h + 1) * HD)
        kctx = jnp.concatenate([kprev[:, sl], kn[:, sl]], axis=0).astype(BF16)
        v = jnp.concatenate([vp_ref[:, sl], vc_ref[:, sl]], axis=0)
        vext = jnp.concatenate([v, ones_col], axis=1).astype(BF16)
        q4 = jnp.concatenate([qn[:, (h * GQA + g) * HD:(h * GQA + g + 1) * HD] for g in range(GQA)],
                             axis=0).astype(BF16)
        s = lax.dot_general(q4, kctx, (((1,), (1,)), ((), ())), preferred_element_type=F32) + bias_ref[h]
        sink = jnp.zeros((GQA * lq, 1), F32)
        for g in range(GQA):
            sink = jnp.where(grp == g, sink_ref[h * GQA + g], sink)
        m = jnp.maximum(jnp.max(s, axis=-1, keepdims=True), sink)
        p = jnp.exp(s - m).astype(BF16)
        oe = jnp.dot(p, vext, preferred_element_type=F32)
        o = oe[:, :HD] / (oe[:, HD:HD + 1] + jnp.exp(sink - m))
        for g in range(GQA):
            hq = h * GQA + g
            o_ref[:, hq * HD:(hq + 1) * HD] = o[g * lq:(g + 1) * lq].astype(o_ref.dtype)
    kprev[...] = kn


def _attn_bias(lq):
    t = jnp.arange(lq)[:, None]
    j = jnp.arange(2 * lq)[None, :]
    dist = lq + t - j
    inwin = (dist >= 0) & (dist < lq)
    slopes = jnp.asarray(_SLOPES, F32).reshape(ATTN_KV, GQA, 1, 1)
    b = -slopes * dist.astype(F32)[None, None]
    variants = [jnp.where((inwin & (j >= lq))[None, None], b, -jnp.inf), jnp.where(inwin[None, None], b, -jnp.inf)]
    return jnp.stack(variants).reshape(2, ATTN_KV, GQA * lq, 2 * lq)


def _attn_prompt(proj, off, b, s, sinks, qg, kg):
    lq = 128
    nq = s // lq
    n = b * s
    aq_b, ak_b, av_b = off["aq"] // AQ_W, off["ak"] // AKV_W, off["av"] // AKV_W
    cur = lambda bi, ni: bi * nq + ni
    prv = lambda bi, ni: bi * nq + jnp.maximum(ni - 1, 0)
    hid = jnp.arange(AQ_W) // HD
    same_head = (hid[:, None] == hid[None, :]).astype(BF16)
    return pl.pallas_call(
        _attn_prompt_kernel,
        grid=(b, nq),
        in_specs=[pl.BlockSpec(memory_space=pltpu.SMEM),
                  pl.BlockSpec((lq, AQ_W), lambda bi, ni: (cur(bi, ni), aq_b)),
                  pl.BlockSpec((lq, AKV_W), lambda bi, ni: (cur(bi, ni), ak_b)),
                  pl.BlockSpec((lq, AKV_W), lambda bi, ni: (cur(bi, ni), av_b)),
                  pl.BlockSpec((lq, AKV_W), lambda bi, ni: (prv(bi, ni), av_b)),
                  pl.BlockSpec((None, ATTN_KV, GQA * lq, 2 * lq), lambda bi, ni: (jnp.minimum(ni, 1), 0, 0, 0)),
                  pl.BlockSpec((AQ_W, AQ_W), lambda bi, ni: (0, 0)),
                  pl.BlockSpec((1, AQ_W), lambda bi, ni: (0, 0)),
                  pl.BlockSpec((1, AKV_W), lambda bi, ni: (0, 0))],
        out_specs=[pl.BlockSpec((lq, AQ_W), lambda bi, ni: (cur(bi, ni), 0)),
                   pl.BlockSpec((lq, AKV_W), lambda bi, ni: (cur(bi, ni), 0))],
        out_shape=[jax.ShapeDtypeStruct((n, AQ_W), BF16), jax.ShapeDtypeStruct((n, AKV_W), F32)],
        scratch_shapes=[pltpu.VMEM((lq, AKV_W), F32)],
        compiler_params=_cparams(("arbitrary", "arbitrary")),
        name="attn_prompt",
    )(sinks, proj, proj, proj, proj, _attn_bias(lq), same_head,
      jnp.tile(qg, (1, ATTN_HEADS)), jnp.tile(kg, (1, ATTN_KV)))


def _attn_sample_kernel(q_ref, kn_ref, vn_ref, knt_ref, vnt_ref, ckt_ref, cvt_ref, qg_ref, kg_ref, kgt_ref, hs_ref,
                        o_ref, okt_ref, ovt_ref):
    tb, nh, _ = q_ref.shape
    win = ckt_ref.shape[2]
    q = _rms(q_ref[...], qg_ref[...])
    hq_i = lax.broadcasted_iota(jnp.int32, (1, nh, HD), 1)
    qbd = jnp.concatenate([jnp.where(hq_i // GQA == kv, q, 0.0) for kv in range(ATTN_KV)], axis=-1)
    kn = kn_ref[...]
    knt = knt_ref[...]
    lane = lax.broadcasted_iota(jnp.int32, (1, 1, AKV_W), 2)
    srow = lax.broadcasted_iota(jnp.int32, (1, AKV_W, 1), 1)
    rs = jnp.zeros_like(kn)
    rst = jnp.zeros_like(knt)
    for kv in range(ATTN_KV):
        msk = lane // HD == kv
        ms = jnp.sum(jnp.where(msk, kn * kn, 0.0), axis=-1, keepdims=True) * (1.0 / HD)
        rs = jnp.where(msk, lax.rsqrt(ms + NORM_EPS), rs)
        mskt = srow // HD == kv
        mst = jnp.sum(jnp.where(mskt, knt * knt, 0.0), axis=1, keepdims=True) * (1.0 / HD)
        rst = jnp.where(mskt, lax.rsqrt(mst + NORM_EPS), rst)
    knn = kn * rs * kg_ref[...][None]
    knnt = knt * rst * kgt_ref[...][None]
    vn = vn_ref[...]
    ckt = ckt_ref[...]
    cvt = cvt_ref[...]
    jj = lax.broadcasted_iota(jnp.int32, (1, 1, win), 2)
    okt_ref[...] = jnp.where(jj == win - 1, knnt, pltpu.roll(ckt, win - 1, 2))
    ovt_ref[...] = jnp.where(jj == win - 1, vnt_ref[...], pltpu.roll(cvt, win - 1, 2))
    slope = hs_ref[0][None]
    sink = hs_ref[1][None][:, :, 0:1]
    s = jnp.einsum("bhc,bcj->bhj", qbd.astype(BF16), ckt.astype(BF16), preferred_element_type=F32) * (HD ** -0.5)
    s = jnp.where(jj >= 1, s - slope * (win - jj).astype(F32), -jnp.inf)
    s_new = jnp.sum(qbd * knn, axis=-1, keepdims=True) * (HD ** -0.5)
    m = jnp.maximum(jnp.maximum(jnp.max(s, axis=-1, keepdims=True), s_new), sink)
    p = jnp.exp(s - m)
    p_new = jnp.exp(s_new - m)
    den = jnp.sum(p, axis=-1, keepdims=True) + p_new + jnp.exp(sink - m)
    of = jnp.einsum("bhj,bcj->bhc", p.astype(BF16), cvt.astype(BF16), preferred_element_type=F32) + p_new * vn
    of = of / den
    o = jnp.zeros((tb, nh, HD), F32)
    for kv in range(ATTN_KV):
        o = o + jnp.where(hq_i // GQA == kv, of[:, :, kv * HD:(kv + 1) * HD], 0.0)
    o_ref[...] = o.astype(o_ref.dtype)


def _attn_sample(q3, kn3, vn3, ckt, cvt, qg, kg, hs):
    bs, nh, _ = q3.shape
    win = ckt.shape[2]
    tb = 16
    seq3 = lambda a, b: pl.BlockSpec((tb, a, b), lambda i: (i, 0, 0))
    kg4 = jnp.tile(kg, (1, ATTN_KV))
    return pl.pallas_call(
        _attn_sample_kernel,
        grid=(bs // tb,),
        in_specs=[seq3(nh, HD), seq3(1, AKV_W), seq3(1, AKV_W), seq3(AKV_W, 1), seq3(AKV_W, 1),
                  seq3(AKV_W, win), seq3(AKV_W, win),
                  pl.BlockSpec((1, 1, HD), lambda i: (0, 0, 0)),
                  pl.BlockSpec((1, AKV_W), lambda i: (0, 0)),
                  pl.BlockSpec((AKV_W, 1), lambda i: (0, 0)),
                  pl.BlockSpec((2, nh, LANES), lambda i: (0, 0, 0))],
        out_specs=[seq3(nh, HD), seq3(AKV_W, win), seq3(AKV_W, win)],
        out_shape=[jax.ShapeDtypeStruct((bs, nh, HD), BF16),
                   jax.ShapeDtypeStruct((bs, AKV_W, win), F32),
                   jax.ShapeDtypeStruct((bs, AKV_W, win), F32)],
        compiler_params=_cparams(("arbitrary",)),
        name="attn_sample",
    )(q3, kn3, vn3, kn3.reshape(bs, AKV_W, 1), vn3.reshape(bs, AKV_W, 1), ckt, cvt,
      qg.reshape(1, 1, HD), kg4, kg4.reshape(AKV_W, 1), hs)


def _log_gates(g_pre, bi_ref, bf_ref):
    lane = lax.broadcasted_iota(jnp.int32, (1, LANES), 1)
    bias = jnp.zeros((1, LANES), F32)
    for h in range(MH):
        bias = jnp.where(lane == h, bi_ref[h], bias)
        bias = jnp.where(lane == MH + h, bf_ref[h], bias)
    pre = g_pre + bias
    logsig = jnp.minimum(pre, 0.0) - jnp.log1p(jnp.exp(-jnp.abs(pre)))
    return jnp.where(lane < MH, pre, logsig)


def _mlstm_prompt_kernel(bi_ref, bf_ref, q_ref, k_ref, v_ref, o_ref, g_ref, ng_ref,
                         hm_ref, c_ref, n_ref, m_ref, m_scr):
    ci = pl.program_id(1)
    nb, ln = q_ref.shape[0], q_ref.shape[1]

    @pl.when(ci == 0)
    def _():
        c_ref[...] = jnp.zeros_like(c_ref)
        n_ref[...] = jnp.zeros_like(n_ref)
        m_scr[...] = jnp.zeros_like(m_scr)

    row = lax.broadcasted_iota(jnp.int32, (ln, ln), 0)
    col = lax.broadcasted_iota(jnp.int32, (ln, ln), 1)
    causal = row >= col
    tril = causal.astype(F32)
    lane = lax.broadcasted_iota(jnp.int32, (1, LANES), 1)
    for j in range(nb):
        lf = _log_gates(g_ref[j], bi_ref, bf_ref)
        bc = jnp.dot(tril, lf, preferred_element_type=F32, precision=lax.Precision.HIGHEST)
        lft = lf.T
        bct = bc.T
        m_out = jnp.zeros((1, LANES), F32)
        for h in range(MH):
            i_row = lft[h:h + 1, :]
            b_row = bct[MH + h:MH + h + 1, :]
            i_col = lf[:, h:h + 1]
            b_col = bc[:, MH + h:MH + h + 1]
            m_prev = m_scr[j * MH + h][:, 0:1]
            log_d = jnp.where(causal, i_row + b_col - b_row, -jnp.inf)
            m_inter = m_prev + b_col
            m_t = jnp.maximum(m_inter, jnp.max(log_d, axis=-1, keepdims=True))
            d = jnp.exp(log_d - m_t)
            a_inter = jnp.exp(m_inter - m_t)
            q = q_ref[j, :, h * MDK:(h + 1) * MDK]
            k = k_ref[j, :, h * MDK:(h + 1) * MDK] * (MDK ** -0.5)
            qb = q.astype(BF16)
            vb = v_ref[j, :, h * MDV:(h + 1) * MDV].astype(BF16)
            w = lax.dot_general(qb, k.astype(BF16), (((1,), (1,)), ((), ())), preferred_element_type=F32) * d
            c_old = c_ref[j, h]
            n_old = n_ref[j, h:h + 1, :]
            num = (jnp.dot(w.astype(BF16), vb, preferred_element_type=F32)
                   + jnp.dot(qb, c_old.astype(BF16), preferred_element_type=F32) * a_inter)
            den = jnp.sum(w, axis=-1, keepdims=True) + a_inter * jnp.sum(q * n_old, axis=-1, keepdims=True)
            den = jnp.maximum(jnp.abs(den), jnp.exp(-m_t))
            hh = num / den
            m_new = m_t[ln - 1:ln, :]
            b_last = b_col[ln - 1:ln, :]
            decay = jnp.exp(i_col + b_last - b_col - m_new)
            carry = jnp.exp(m_prev + b_last - m_new)
            kd = k * decay
            c_ref[j, h] = carry * c_old + jnp.dot(kd.T.astype(BF16), vb, preferred_element_type=F32)
            n_ref[j, h:h + 1, :] = carry * n_old + jnp.sum(kd, axis=0, keepdims=True)
            m_scr[j * MH + h] = jnp.broadcast_to(m_new, (1, LANES))
            m_out = jnp.where(lane == h, m_new, m_out)
            hn = (_rms(hh, ng_ref[:, h * MDV:(h + 1) * MDV])
                  * jax.nn.sigmoid(o_ref[j, :, h * MDV:(h + 1) * MDV]))
            hm_ref[j, :, h * MDV:(h + 1) * MDV] = hn.astype(hm_ref.dtype)
        m_ref[j] = m_out


def _mlstm_prompt(proj, off, b, s, b_i, b_f, ng):
    ln = MLSTM_L
    nc = s // ln
    nb = 1
    mq_b, mk_b = off["mq"] // MQK_W, off["mk"] // MQK_W
    mv_b, mo_b, gt_b = off["mv"] // MV_W, off["mo"] // MV_W, off["gt"] // LANES
    proj3 = proj.reshape(b, s, proj.shape[1])
    smem = pl.BlockSpec(memory_space=pltpu.SMEM)
    col = lambda width, cb: pl.BlockSpec((nb, ln, width), lambda bi, ci: (bi, ci, cb))
    hm, c, n_state, m = pl.pallas_call(
        _mlstm_prompt_kernel,
        grid=(b // nb, nc),
        in_specs=[smem, smem, col(MQK_W, mq_b), col(MQK_W, mk_b), col(MV_W, mv_b), col(MV_W, mo_b),
                  col(LANES, gt_b), pl.BlockSpec((1, MV_W), lambda bi, ci: (0, 0))],
        out_specs=[col(MV_W, 0),
                   pl.BlockSpec((nb, MH, MDK, MDV), lambda bi, ci: (bi, 0, 0, 0)),
                   pl.BlockSpec((nb, MH, MDK), lambda bi, ci: (bi, 0, 0)),
                   pl.BlockSpec((nb, 1, LANES), lambda bi, ci: (bi, 0, 0))],
        out_shape=[jax.ShapeDtypeStruct((b, s, MV_W), BF16),
                   jax.ShapeDtypeStruct((b, MH, MDK, MDV), F32),
                   jax.ShapeDtypeStruct((b, MH, MDK), F32),
                   jax.ShapeDtypeStruct((b, 1, LANES), F32)],
        scratch_shapes=[pltpu.VMEM((nb * MH, 1, LANES), F32)],
        compiler_params=_cparams(("arbitrary", "arbitrary")),
        name="mlstm_prompt",
    )(b_i, b_f, proj3, proj3, proj3, proj3, proj3, ng)
    return hm.reshape(b * s, MV_W), c, n_state, m


def _mlstm_sample_kernel(bi_ref, bf_ref, q_ref, k_ref, v_ref, o_ref, g_ref, ng_ref, c0_ref, n0_ref, m0_ref,
                         hm_ref, c_ref, n_ref, m_ref):
    tb = q_ref.shape[0]
    lf = _log_gates(g_ref[...], bi_ref, bf_ref)
    lane = lax.broadcasted_iota(jnp.int32, (1, LANES), 1)
    m_out = jnp.zeros((tb, LANES), F32)
    for h in range(MH):
        li = lf[:, h:h + 1]
        lfg = lf[:, MH + h:MH + h + 1]
        m_prev = m0_ref[:, h:h + 1]
        m_inter = m_prev + lfg
        m_t = jnp.maximum(m_inter, li)
        d = jnp.exp(li - m_t)
        a = jnp.exp(m_inter - m_t)
        q = q_ref[:, h * MDK:(h + 1) * MDK]
        k = k_ref[:, h * MDK:(h + 1) * MDK] * (MDK ** -0.5)
        v = v_ref[:, h * MDV:(h + 1) * MDV]
        n_old = n0_ref[:, h, :]
        w = jnp.sum(q * k, axis=-1, keepdims=True) * d
        den = w + a * jnp.sum(q * n_old, axis=-1, keepdims=True)
        den = jnp.maximum(jnp.abs(den), jnp.exp(-m_t))
        dk = k * d
        qt = q.T
        dkt = dk.T
        rows = []
        for b in range(tb):
            c_old = c0_ref[b, h]
            qc = jnp.sum(c_old * qt[:, b:b + 1], axis=0, keepdims=True)
            a_b = a[b:b + 1, :]
            vrow = v[b:b + 1, :]
            rows.append((w[b:b + 1, :] * vrow + qc * a_b) / den[b:b + 1, :])
            c_ref[b, h] = a_b * c_old + dkt[:, b:b + 1] * vrow
        hh = jnp.concatenate(rows, axis=0)
        n_ref[:, h, :] = a * n_old + dk
        m_out = jnp.where(lane == h, m_t, m_out)
        hn = _rms(hh, ng_ref[:, h * MDV:(h + 1) * MDV]) * jax.nn.sigmoid(o_ref[:, h * MDV:(h + 1) * MDV])
        hm_ref[:, h * MDV:(h + 1) * MDV] = hn.astype(hm_ref.dtype)
    m_ref[...] = m_out


def _mlstm_sample(proj, off, bs, b_i, b_f, ng, c0, n0, m0):
    tb = 8
    mq_b, mk_b = off["mq"] // MQK_W, off["mk"] // MQK_W
    mv_b, mo_b, gt_b = off["mv"] // MV_W, off["mo"] // MV_W, off["gt"] // LANES
    smem = pl.BlockSpec(memory_space=pltpu.SMEM)
    return pl.pallas_call(
        _mlstm_sample_kernel,
        grid=(bs // tb,),
        in_specs=[smem, smem,
                  pl.BlockSpec((tb, MQK_W), lambda i: (i, mq_b)),
                  pl.BlockSpec((tb, MQK_W), lambda i: (i, mk_b)),
                  pl.BlockSpec((tb, MV_W), lambda i: (i, mv_b)),
                  pl.BlockSpec((tb, MV_W), lambda i: (i, mo_b)),
                  pl.BlockSpec((tb, LANES), lambda i: (i, gt_b)),
                  pl.BlockSpec((1, MV_W), lambda i: (0, 0)),
                  pl.BlockSpec((tb, MH, MDK, MDV), lambda i: (i, 0, 0, 0)),
                  pl.BlockSpec((tb, MH, MDK), lambda i: (i, 0, 0)),
                  pl.BlockSpec((tb, MH), lambda i: (i, 0))],
        out_specs=[pl.BlockSpec((tb, MV_W), lambda i: (i, 0)),
                   pl.BlockSpec((tb, MH, MDK, MDV), lambda i: (i, 0, 0, 0)),
                   pl.BlockSpec((tb, MH, MDK), lambda i: (i, 0, 0)),
                   pl.BlockSpec((tb, LANES), lambda i: (i, 0))],
        out_shape=[jax.ShapeDtypeStruct((bs, MV_W), BF16),
                   jax.ShapeDtypeStruct((bs, MH, MDK, MDV), F32),
                   jax.ShapeDtypeStruct((bs, MH, MDK), F32),
                   jax.ShapeDtypeStruct((bs, LANES), F32)],
        compiler_params=_cparams(("arbitrary",)),
        name="mlstm_sample",
    )(b_i, b_f, proj, proj, proj, proj, proj, ng, c0, n0, m0)


def _postmix_kernel(n_exp, n_main, *refs):
    if n_main is None:
        _postmix_tile(n_exp, *refs)
        return
    tile_refs, u2s_ref, out_refs = refs[:15], refs[15], refs[16:]
    i = pl.program_id(0)

    @pl.when(i < n_main)
    def _():
        _postmix_tile(n_exp, *tile_refs, *out_refs)

    @pl.when(i == n_main)
    def _():
        out_refs[1][0:u2s_ref.shape[0], :] = u2s_ref[...]


def _postmix_tile(n_exp, ha_ref, hm_ref, ga_ref, gm_ref, x_ref, g1_ref, sh2_ref, sc2_ref,
                  wa_ref, wm_ref, wo_ref, gf_ref, wrh_ref, wrl_ref, br_ref, x1_ref, u2_ref, ti_ref, tg_ref):
    a = jnp.dot(ha_ref[...], wa_ref[...], preferred_element_type=F32)
    m = jnp.dot(hm_ref[...], wm_ref[...], preferred_element_type=F32)
    merged = jax.nn.sigmoid(ga_ref[...]) * a + jax.nn.sigmoid(gm_ref[...]) * m
    y = jnp.dot(merged.astype(BF16), wo_ref[...], preferred_element_type=F32)
    x1 = x_ref[...] + g1_ref[...] * y
    x1_ref[...] = x1
    u2 = _rms(x1, gf_ref[...]) * (1.0 + sc2_ref[...]) + sh2_ref[...]
    u2_ref[...] = u2
    u2h = u2.astype(BF16)
    u2l = (u2 - u2h.astype(F32)).astype(BF16)
    logits = (jnp.dot(u2h, wrh_ref[...], preferred_element_type=F32)
              + (jnp.dot(u2l, wrh_ref[...], preferred_element_type=F32)
                 + jnp.dot(u2h, wrl_ref[...], preferred_element_type=F32))) + br_ref[...]
    lane = lax.broadcasted_iota(jnp.int32, logits.shape, 1)
    lanef = lane.astype(F32)
    work = jnp.where(lane < n_exp, logits, -jnp.inf)
    vals, idxs = [], []
    for _ in range(TOP_K):
        mx = jnp.max(work, axis=-1, keepdims=True)
        am = jnp.min(jnp.where(work == mx, lanef, float(LANES)), axis=-1, keepdims=True)
        vals.append(mx)
        idxs.append(am)
        work = jnp.where(lanef == am, -jnp.inf, work)
    es = [jnp.exp(v - vals[0]) for v in vals]
    tot = es[0] + es[1] + es[2] + es[3]
    ti = jnp.zeros(logits.shape, F32)
    tg = jnp.zeros(logits.shape, F32)
    for kk in range(TOP_K):
        ti = jnp.where(lane == kk, idxs[kk], ti)
        tg = jnp.where(lane == kk, es[kk] / tot, tg)
    ti_ref[...] = ti.astype(jnp.int32)
    tg_ref[...] = tg


def _postmix(ha, hm, proj, off, x2d, mod_block, mod_idx, mod, wa, wm, wo, gf, wr, br, n_exp, tm, name, u2_tail=None):
    n, d = x2d.shape
    nt = n // tm
    ga_b, gm_b = off["ga"] // d, off["gm"] // d
    const = lambda shape: pl.BlockSpec(shape, lambda i: (0,) * len(shape), pipeline_mode=pl.Buffered(1))
    ci = (lambda i: i) if u2_tail is None else (lambda i: jnp.minimum(i, nt - 1))
    row = lambda w, cb=0: pl.BlockSpec((tm, w), lambda i: (ci(i), cb))
    mspec = lambda k: pl.BlockSpec(mod_block, lambda i: mod_idx(ci(i), k))
    in_specs = [row(AQ_W), row(MV_W), row(d, ga_b), row(d, gm_b), row(d),
                mspec(2), mspec(3), mspec(4),
                const((AQ_W, d)), const((MV_W, d)), const((d, d)),
                const((1, d)), const((d, LANES)), const((d, LANES)), const((1, LANES))]
    args = [ha, hm, proj, proj, x2d, mod, mod, mod, wa, wm, wo, gf, wr[0], wr[1], br]
    n_u2 = n
    if u2_tail is not None:
        assert u2_tail.shape[0] <= tm
        in_specs.append(const(u2_tail.shape))
        args.append(u2_tail)
        n_u2 = n + u2_tail.shape[0]
    return pl.pallas_call(
        functools.partial(_postmix_kernel, n_exp, None if u2_tail is None else nt),
        grid=(nt if u2_tail is None else nt + 1,),
        in_specs=in_specs,
        out_specs=[row(d), pl.BlockSpec((tm, d), lambda i: (i, 0)), row(LANES), row(LANES)],
        out_shape=[jax.ShapeDtypeStruct((n, d), F32), jax.ShapeDtypeStruct((n_u2, d), F32),
                   jax.ShapeDtypeStruct((n, LANES), jnp.int32), jax.ShapeDtypeStruct((n, LANES), F32)],
        compiler_params=_cparams(("arbitrary",)),
        name=name,
    )(*args)


def _moe_kernel(n_s, n_c, se_ref, sf_ref, sn_ref, sr_ref, nu_ref, tokn_ref, u2_ref, wu_ref, wd_ref, bu_ref,
                bd_ref, yb_ref, xg, xb, acc, wub, wdp, wdb, act_carry, zbuf, gsem, wsem, zsem):
    s = pl.program_id(0)
    c = pl.program_id(1)
    sub = MOE_SUB
    ns = sn_ref[s]
    ns_prev = sn_ref[jnp.maximum(s - 1, 0)]
    fb = sf_ref[s]
    grp = MOE_GATHER_GROUP
    groups = (sr_ref[s] + grp - 1) // grp
    groups_next = jnp.where(s + 1 < n_s, (sr_ref[jnp.minimum(s + 1, n_s - 1)] + grp - 1) // grp, 0)

    def gather_group(g, carry):
        base = pl.multiple_of(g * grp, grp)
        for i in range(grp):
            j = base + i
            pltpu.make_async_copy(u2_ref.at[pl.ds(tokn_ref[0, j], 1)], xg.at[pl.ds(j, 1)], gsem).start()
        return carry

    def wb_copy(m, blk):
        return pltpu.make_async_copy(acc.at[pl.ds(pl.multiple_of(m * sub, sub), sub)],
                                     yb_ref.at[pl.ds(pl.multiple_of(blk * sub, sub), sub)], wsem.at[m])

    @pl.when((s == 0) & (c == 0))
    def _():
        xg[...] = jnp.zeros_like(xg)
        wdb[...] = jnp.zeros_like(wdb)
        act_carry[...] = jnp.zeros_like(act_carry)

    @pl.when(c == 0)
    def _():
        def wait_group(i, carry):
            pltpu.make_async_copy(u2_ref.at[pl.ds(0, grp)], xg.at[pl.ds(0, grp)], gsem).wait()
            return carry
        lax.fori_loop(0, groups, wait_group, 0)
        for m in range(MOE_NSUB):
            @pl.when((s > 0) & (m < ns_prev))
            def _():
                wb_copy(m, 0).wait()

        def prep(m, carry):
            r0 = pl.multiple_of(m * sub, sub)
            xb[pl.ds(r0, sub), :] = xg[pl.ds(r0, sub), :].astype(BF16)
            acc[pl.ds(r0, sub), :] = jnp.broadcast_to(bd_ref[...], (sub, acc.shape[1]))
            return carry
        lax.fori_loop(0, ns, prep, 0)

    def down(m, act):
        r0 = pl.multiple_of(m * sub, sub)
        pw = 4 * LANES
        for p in range(wdb.shape[1] // pw):
            cols = slice(p * pw, (p + 1) * pw)
            acc[pl.ds(r0, sub), cols] += jnp.dot(act, wdb[:, cols], preferred_element_type=F32)

    @pl.when(ns > 0)
    def _():
        down(ns - 1, act_carry[...])
        wub[...] = wu_ref[...].astype(BF16)
        half = LANES // 2
        for cb in range(wd_ref.shape[1] // LANES):
            cols = slice(cb * LANES, (cb + 1) * LANES)
            for g in range(wd_ref.shape[0] // LANES):
                wdp[cb, pl.ds(g * LANES, half, stride=2), :] = wd_ref[g * LANES:g * LANES + half, cols]
                wdp[cb, pl.ds(g * LANES + 1, half, stride=2), :] = wd_ref[g * LANES + half:(g + 1) * LANES, cols]
            wdb[:, cols] = wdp[cb].astype(BF16)

    def block(m, act_prev):
        r0 = pl.multiple_of(m * sub, sub)
        x = xb[pl.ds(r0, sub), :]
        gw = 2 * LANES
        hs = [jnp.dot(x, wub[:, g * gw:(g + 1) * gw], preferred_element_type=F32) + bu_ref[:, g * gw:(g + 1) * gw]
              for g in range(wub.shape[1] // gw)]
        if act_prev is not None:
            down(jnp.maximum(m - 1, 0), act_prev)
        even = lax.broadcasted_iota(jnp.int32, (sub, LANES), 1) % 2 == 0
        parts = []
        for h in hs:
            h0 = h[:, :LANES]
            h1 = h[:, LANES:]
            glu = jnp.where(even, h0, pltpu.roll(h1, 1, 1))
            lin = jnp.where(even, pltpu.roll(h0, LANES - 1, 1), h1)
            glu = jnp.minimum(glu, SWIGLU_LIMIT)
            lin = jnp.clip(lin, -SWIGLU_LIMIT, SWIGLU_LIMIT)
            parts.append(glu * jax.nn.sigmoid(SWIGLU_ALPHA * glu) * (lin + 1.0))
        return jnp.concatenate(parts, axis=1).astype(BF16)

    def wb_after(m):
        @pl.when((c == n_c - 1) & (m > 0))
        def _():
            wb_copy(m - 1, fb + m - 1).start()

    def pair(p, act_prev):
        act = block(2 * p + 1, block(2 * p, act_prev))
        wb_after(2 * p)
        wb_after(2 * p + 1)
        return act

    def single(m, act_prev):
        act = block(m, act_prev)
        wb_after(m)
        return act

    def first_pair():
        act = block(1, block(0, None))
        wb_after(1)
        return act

    act_last = lax.cond(ns >= 2, first_pair, lambda: jnp.zeros(act_carry.shape, BF16))
    act_last = lax.fori_loop(1, ns // 2, pair, act_last)
    act_last = lax.fori_loop(2 * (ns // 2), ns, single, act_last)

    @pl.when((ns > 0) & (c < n_c - 1))
    def _():
        act_carry[...] = act_last

    @pl.when((ns > 0) & (c == n_c - 1))
    def _():
        down(ns - 1, act_last)
        wb_copy(ns - 1, fb + ns - 1).start()
        act_carry[...] = jnp.zeros_like(act_carry)

    per_step = (groups_next + n_c - 1) // n_c
    lax.fori_loop(jnp.minimum(c * per_step, groups_next), jnp.minimum((c + 1) * per_step, groups_next),
                  gather_group, 0)

    @pl.when((s == n_s - 1) & (c == n_c - 1))
    def _():
        n_blocks = yb_ref.shape[0] // sub
        zbuf[...] = jnp.zeros_like(zbuf)

        def zstart(b, carry):
            pltpu.make_async_copy(zbuf, yb_ref.at[pl.ds(pl.multiple_of(b * sub, sub), sub)], zsem).start()
            return carry

        def zwait(b, carry):
            pltpu.make_async_copy(zbuf, yb_ref.at[pl.ds(0, sub)], zsem).wait()
            return carry
        lax.fori_loop(nu_ref[0], n_blocks, zstart, 0)
        lax.fori_loop(nu_ref[0], n_blocks, zwait, 0)


def _moe(u2, tabs, n_blocks, w_up, b_up, w_down, b_down):
    sb_e, sb_fb, sb_ns, sb_rows, n_used, sb_tok = tabs
    n_s = sb_e.shape[0]
    n_exp, d, de2 = w_up.shape
    de = de2 // 2
    hc = min(MOE_HC, de)
    n_c = de // hc
    rmax = MOE_NSUB * MOE_SUB
    assert rmax % MOE_GATHER_GROUP == 0 and hc % LANES == 0

    def chunk(s, c, sn):
        return jnp.where(sn[s] > 0, c, jnp.where(s == 0, 0, n_c - 1))

    grid_spec = pltpu.PrefetchScalarGridSpec(
        num_scalar_prefetch=5,
        grid=(n_s, n_c),
        in_specs=[
            pl.BlockSpec((None, 1, rmax), lambda s, c, se, sf, sn, sr, nu: (s + 1, 0, 0), memory_space=pltpu.SMEM),
            pl.BlockSpec(memory_space=pl.ANY),
            pl.BlockSpec((None, d, 2 * hc), lambda s, c, se, sf, sn, sr, nu: (se[s], 0, chunk(s, c, sn))),
            pl.BlockSpec((None, hc, d), lambda s, c, se, sf, sn, sr, nu: (se[s], chunk(s, c, sn), 0)),
            pl.BlockSpec((None, 1, 2 * hc), lambda s, c, se, sf, sn, sr, nu: (se[s], 0, chunk(s, c, sn))),
            pl.BlockSpec((None, 1, d), lambda s, c, se, sf, sn, sr, nu: (se[s], 0, 0)),
        ],
        out_specs=pl.BlockSpec(memory_space=pl.ANY),
        scratch_shapes=[pltpu.VMEM((rmax, d), F32), pltpu.VMEM((rmax, d), BF16), pltpu.VMEM((rmax, d), F32),
                        pltpu.VMEM((d, 2 * hc), BF16), pltpu.VMEM((d // LANES, hc, LANES), F32),
                        pltpu.VMEM((hc, d), BF16), pltpu.VMEM((MOE_SUB, hc), BF16),
                        pltpu.VMEM((MOE_SUB, d), F32),
                        pltpu.SemaphoreType.DMA, pltpu.SemaphoreType.DMA((MOE_NSUB,)), pltpu.SemaphoreType.DMA],
    )
    return pl.pallas_call(
        functools.partial(_moe_kernel, n_s, n_c),
        grid_spec=grid_spec,
        out_shape=jax.ShapeDtypeStruct((n_blocks * MOE_SUB, d), F32),
        compiler_params=_cparams(("arbitrary", "arbitrary")),
        name="moe_experts",
    )(sb_e, sb_fb, sb_ns, sb_rows, n_used, sb_tok, u2, w_up, w_down, b_up.reshape(n_exp, 1, de2),
      b_down.reshape(n_exp, 1, d))


def _combine_kernel(pc_ref, pn_ref, yb_ref, x1_ref, g2_ref, tg_ref, o_ref, buf, sem):
    s = pl.program_id(0)
    ns = pl.num_programs(0)
    tm = x1_ref.shape[0]

    rows = 8

    def issue_token(p_ref, slot, t):
        for kk in range(TOP_K):
            pltpu.make_async_copy(yb_ref.at[pl.ds(p_ref[0, t * TOP_K + kk], 1)],
                                  buf.at[slot, kk, pl.ds(t, 1)], sem.at[slot]).start()

    def issue(p_ref, slot):
        def body(t, carry):
            issue_token(p_ref, slot, t)
            return carry
        lax.fori_loop(0, tm, body, 0)

    def drain(slot):
        for kk in range(TOP_K):
            pltpu.make_async_copy(yb_ref.at[pl.ds(0, tm)], buf.at[slot, kk], sem.at[slot]).wait()

    slot = s % 2

    def out_rows(i):
        rs = pl.ds(pl.multiple_of(i * rows, rows), rows)
        tg = tg_ref[rs, :]
        acc = tg[:, 0:1] * buf[slot, 0, rs, :]
        for kk in range(1, TOP_K):
            acc = acc + tg[:, kk:kk + 1] * buf[slot, kk, rs, :]
        g2 = g2_ref[...] if g2_ref.shape[0] == 1 else g2_ref[rs, :]
        o_ref[rs, :] = x1_ref[rs, :] + g2 * acc

    @pl.when(s == 0)
    def _():
        issue(pc_ref, 0)

    drain(slot)

    @pl.when(s + 1 < ns)
    def _():
        def body(i, carry):
            for j in range(rows):
                issue_token(pn_ref, 1 - slot, i * rows + j)
            out_rows(i)
            return carry
        lax.fori_loop(0, tm // rows, body, 0)

    @pl.when(s + 1 == ns)
    def _():
        def body(i, carry):
            out_rows(i)
            return carry
        lax.fori_loop(0, tm // rows, body, 0)


def _combine(pos, yb, x1, g2_spec, g2_arg, tg, tm, name):
    n, d = x1.shape
    ns = n // tm
    pos3 = pos.reshape(ns, 1, tm * TOP_K)
    return pl.pallas_call(
        _combine_kernel,
        grid=(ns,),
        in_specs=[pl.BlockSpec((None, 1, tm * TOP_K), lambda s: (s, 0, 0), memory_space=pltpu.SMEM),
                  pl.BlockSpec((None, 1, tm * TOP_K), lambda s: (jnp.minimum(s + 1, ns - 1), 0, 0),
                               memory_space=pltpu.SMEM),
                  pl.BlockSpec(memory_space=pl.ANY),
                  pl.BlockSpec((tm, d), lambda s: (s, 0)),
                  g2_spec,
                  pl.BlockSpec((tm, LANES), lambda s: (s, 0))],
        out_specs=pl.BlockSpec((tm, d), lambda s: (s, 0)),
        out_shape=jax.ShapeDtypeStruct((n, d), F32),
        scratch_shapes=[pltpu.VMEM((2, TOP_K, tm, d), F32), pltpu.SemaphoreType.DMA((2,))],
        compiler_params=_cparams(("arbitrary",)),
        name=name,
    )(pos3, pos3, yb, x1, g2_arg, tg)


def _routing(top_e, n_exp, n_blocks):
    sub, nsub = MOE_SUB, MOE_NSUB
    rmax = sub * nsub
    i32 = jnp.int32
    nk = top_e.size
    flat_e = top_e.reshape(nk)
    oh = (flat_e[:, None] == jnp.arange(n_exp, dtype=i32)[None, :]).astype(i32)
    csum = jnp.cumsum(oh, axis=0)
    rank = jnp.sum((csum - oh) * oh, axis=1)
    counts = csum[-1]
    nblk = (counts + sub - 1) // sub
    blk_end = jnp.cumsum(nblk)
    blk_start = blk_end - nblk
    pos = jnp.sum(oh * blk_start[None, :], axis=1) * sub + rank
    n_used = blk_end[-1]
    nsb = (nblk + nsub - 1) // nsub
    sb_end = jnp.cumsum(nsb)
    sb_start = sb_end - nsb
    n_sb = sb_end[-1]
    n_real = n_exp + n_blocks // nsub + 1
    sidx = jnp.arange(n_real, dtype=i32)
    e_of = jnp.minimum(jnp.sum((sidx[:, None] >= sb_end[None, :]).astype(i32), axis=1), n_exp - 1)
    oh_s = (e_of[:, None] == jnp.arange(n_exp, dtype=i32)[None, :]).astype(i32)
    k_in = sidx - jnp.sum(oh_s * sb_start[None, :], axis=1)
    fb = jnp.sum(oh_s * blk_start[None, :], axis=1) + nsub * k_in
    ns = jnp.clip(jnp.sum(oh_s * nblk[None, :], axis=1) - nsub * k_in, 0, nsub) * (sidx < n_sb)
    last_e = jnp.sum(jnp.where(sidx == n_sb - 1, e_of, 0))
    e_of = jnp.where(sidx < n_sb, e_of, last_e)
    zero = jnp.zeros((1,), i32)
    sb_e = jnp.concatenate([e_of[:1], e_of, last_e.reshape(1)]).astype(i32)
    sb_fb = jnp.concatenate([zero, fb, zero]).astype(i32)
    sb_ns = jnp.concatenate([zero, ns, zero]).astype(i32)
    rows = jnp.clip(jnp.sum(oh_s * counts[None, :], axis=1) - rmax * k_in, 0, rmax) * (sidx < n_sb)
    sb_rows = jnp.concatenate([zero, rows, zero]).astype(i32)
    slot = 1 + jnp.sum(oh * sb_start[None, :], axis=1) + rank // rmax
    tok = (jnp.arange(nk, dtype=i32) // TOP_K).astype(i32)
    flat = (slot * rmax + rank % rmax).astype(i32)
    sb_tok = jnp.zeros(((n_real + 3) * rmax,), i32).at[flat].set(tok, unique_indices=True)
    tabs = (sb_e, sb_fb, sb_ns, sb_rows, n_used.reshape(1).astype(i32), sb_tok.reshape(n_real + 3, 1, rmax))
    return pos.astype(i32), tabs


def kernel(x_prompt, x_sample, cache_k, cache_v, state_C, state_n, state_m, c_prompt, c_sample, w_ada, b_ada,
           g_mix, w_in, b_igate, b_fgate, q_norm_g, k_norm_g, attn_sinks, mlstm_norm_g, w_attn_up, w_mlstm_up,
           w_out, g_ffn, w_router, b_router, w_up, b_up, w_down, b_down):
    bp, sp, d = x_prompt.shape
    bs = x_sample.shape[0]
    assert x_sample.shape[1] == 1 and w_ada.shape[0] == 1
    n_p = bp * sp
    win = cache_k.shape[2]
    n_exp = w_router.shape[2]
    de = w_down.shape[2]
    off, _ = _col_layout(d)

    w_r = _winprep(jnp.transpose(w_in[0]), d)
    wa =w_attn_up[0].astype(BF16)
    wm = w_mlstm_up[0].astype(BF16)
    wo = w_out[0].astype(BF16)
    wr32 = jnp.pad(w_router[0], ((0, 0), (0, LANES - n_exp)))
    wrh = wr32.astype(BF16)
    wr = (wrh, (wr32 - wrh.astype(F32)).astype(BF16))
    br = jnp.pad(b_router[0], (0, LANES - n_exp)).reshape(1, LANES)
    gmix = g_mix[0].reshape(1, d)
    gffn = g_ffn[0].reshape(1, d)
    qg = q_norm_g[0].reshape(1, HD)
    kg = k_norm_g[0].reshape(1, HD)
    ng = mlstm_norm_g[0].reshape(1, MV_W)
    sinks = attn_sinks[0]
    hs = jnp.stack([jnp.broadcast_to(jnp.asarray(_SLOPES, F32)[:, None], (ATTN_HEADS, LANES)),
                    jnp.broadcast_to(sinks[:, None], (ATTN_HEADS, LANES))])

    mod = _adaln(jnp.concatenate([c_prompt, c_sample], axis=0), w_ada[0], b_ada[0])
    mod_p = mod[:bp].reshape(bp, 1, N_MOD * d)
    mod_s = mod[bp:]

    xs = x_sample.reshape(bs, d)
    sspec2 = lambda kk: pl.BlockSpec((bs, d), lambda i, j: (0, kk))
    sspec1 = lambda kk: pl.BlockSpec((bs, d), lambda i: (0, kk))
    proj_s = _inproj(xs, gmix, (sspec2(0), sspec2(1)), (mod_s, mod_s), w_r, bs, "inproj_sample")
    q3 = proj_s[:, off["aq"]:off["aq"] + AQ_W].reshape(bs, ATTN_HEADS, HD)
    kn3 = proj_s[:, off["ak"]:off["ak"] + AKV_W].reshape(bs, 1, AKV_W)
    vn3 = proj_s[:, off["av"]:off["av"] + AKV_W].reshape(bs, 1, AKV_W)
    feature_major = lambda cache: jnp.swapaxes(cache[0].reshape(bs, win, AKV_W), 1, 2)
    ha_s3, kt_s, vt_s = _attn_sample(q3, kn3, vn3, feature_major(cache_k), feature_major(cache_v), qg, kg, hs)
    k_s, v_s = jnp.swapaxes(kt_s, 1, 2), jnp.swapaxes(vt_s, 1, 2)
    hm_s, c_s, n_state_s, m_s = _mlstm_sample(proj_s, off, bs, b_igate[0], b_fgate[0], ng,
                                              state_C[0], state_n[0], state_m[0])
    x1_s, u2_s, ti_s, tg_s = _postmix(
        ha_s3.reshape(bs, AQ_W), hm_s, proj_s, off, xs, (bs, d), lambda i, k: (0, k), mod_s,
        wa, wm, wo, gffn, wr, br, n_exp, bs, "postmix_sample")

    xp = x_prompt.reshape(n_p, d)
    tm_in = min(1024, sp)
    tps = sp // tm_in
    pspec = lambda t, kk, nargs: pl.BlockSpec(
        (None, 1, d), (lambda i, j: (i // t, 0, kk)) if nargs == 2 else (lambda i: (i // t, 0, kk)))
    proj_p = _inproj(xp, gmix, (pspec(tps, 0, 2), pspec(tps, 1, 2)), (mod_p, mod_p), w_r, tm_in, "inproj_prompt")
    ha_p, kn_p = _attn_prompt(proj_p, off, bp, sp, sinks, qg, kg)
    hm_p, c_p, n_state_p, m_p = _mlstm_prompt(proj_p, off, bp, sp, b_igate[0], b_fgate[0], ng)
    tm_pm = min(256, sp)
    tpp = sp // tm_pm
    x1_p, u2_all, ti_p, tg_p = _postmix(
        ha_p, hm_p, proj_p, off, xp, (None, 1, d), lambda i, k: (i // tpp, 0, k), mod_p,
        wa, wm, wo, gffn, wr, br, n_exp, tm_pm, "postmix_prompt", u2_tail=u2_s)

    nk = (n_p + bs) * TOP_K
    n_blocks = -(-nk // MOE_SUB) + n_exp
    top_e = jnp.concatenate([ti_p[:, :TOP_K], ti_s[:, :TOP_K]], axis=0)
    pos, tabs = _routing(top_e, n_exp, n_blocks)
    yb = _moe(u2_all, tabs, n_blocks, w_up[0], b_up[0], w_down[0], b_down[0])
    y_p = _combine(pos[:n_p * TOP_K], yb, x1_p, pspec(tpp, 5, 1), mod_p, tg_p, tm_pm, "combine_prompt")
    y_s = _combine(pos[n_p * TOP_K:], yb, x1_s, sspec1(5), mod_s, tg_s, bs, "combine_sample")

    kvshape = (1, bp, win, ATTN_KV, HD)
    k_p = kn_p.reshape(bp, sp, AKV_W)[:, sp - win:].reshape(kvshape)
    v_p = proj_p[:, off["av"]:off["av"] + AKV_W].reshape(bp, sp, AKV_W)[:, sp - win:].reshape(kvshape)
    return (y_p.reshape(bp, sp, d), y_s.reshape(bs, 1, d),
            k_p, v_p, c_p[None], n_state_p[None], m_p[:, 0, :MH][None],
            k_s.reshape(1, bs, win, ATTN_KV, HD), v_s.reshape(1, bs, win, ATTN_KV, HD),
            c_s[None], n_state_s[None], m_s[:, :MH][None])
```

```python
import functools

import numpy as np
import jax
import jax.numpy as jnp
from jax import lax
from jax.experimental import pallas as pl
from jax.experimental.pallas import tpu as pltpu

F32 = jnp.float32
BF16 = jnp.bfloat16

ATTN_HEADS = 16
ATTN_KV = 4
HD = 64
GQA = ATTN_HEADS // ATTN_KV
MH = 4
MDK = 128
MDV = 256
TOP_K = 4
N_MOD = 6
NORM_EPS = 1e-6
SWIGLU_LIMIT = 7.0
SWIGLU_ALPHA = 1.702

AQ_W = ATTN_HEADS * HD
AKV_W = ATTN_KV * HD
MQK_W = MH * MDK
MV_W = MH * MDV
GATE_PAD = 512

LANES = 128
MOE_SUB = 256
MOE_NSUB = 5
MOE_HC = 256
MOE_GATHER_GROUP = 8
MLSTM_L = 128
VMEM_LIMIT = 56 * 1024 * 1024

_SLOPES = [float(np.exp2(np.float32(-8.0 * (h + 1) / ATTN_HEADS))) for h in range(ATTN_HEADS)]


def _cparams(sem):
    return pltpu.CompilerParams(dimension_semantics=sem, vmem_limit_bytes=VMEM_LIMIT)


def _col_layout(d):
    off = {}
    o = 0
    for name, w in (("ga", d), ("gm", d), ("aq", AQ_W), ("mv", MV_W), ("mo", MV_W), ("ak", AKV_W),
                    ("av", AKV_W), ("mq", MQK_W), ("mk", MQK_W), ("gt", GATE_PAD)):
        off[name] = o
        o += w
    return off, o


def _rms(x, g):
    return x * lax.rsqrt(jnp.mean(x * x, axis=-1, keepdims=True) + NORM_EPS) * g


def _adaln_kernel(c_ref, w_ref, b_ref, o_ref):
    c = c_ref[...]
    s = (c * jax.nn.sigmoid(c)).astype(BF16)
    o_ref[...] = jnp.dot(s, w_ref[...].astype(BF16), preferred_element_type=F32) + b_ref[...]


def _adaln(c_all, w_ada, b_ada):
    r, d = c_all.shape
    w = w_ada.shape[1]
    tn = 1024
    return pl.pallas_call(
        _adaln_kernel,
        grid=(w // tn,),
        in_specs=[pl.BlockSpec((r, d), lambda j: (0, 0)),
                  pl.BlockSpec((d, tn), lambda j: (0, j)),
                  pl.BlockSpec((1, tn), lambda j: (0, j))],
        out_specs=pl.BlockSpec((r, tn), lambda j: (0, j)),
        out_shape=jax.ShapeDtypeStruct((r, w), F32),
        compiler_params=_cparams(("arbitrary",)),
        name="adaln_mod",
    )(c_all, w_ada, b_ada.reshape(1, w))


WINPREP_TR = 256


def _winprep_kernel(n_units, src_ref, nv_ref, wt_ref, o_ref, buf, sem):
    i = pl.program_id(0)
    tr = buf.shape[1]

    def fetch(step, slot):
        start = pl.multiple_of(src_ref[step], 8)
        return pltpu.make_async_copy(wt_ref.at[pl.ds(start, tr)], buf.at[slot], sem.at[slot])

    @pl.when(i == 0)
    def _():
        fetch(0, 0).start()

    @pl.when(i + 1 < n_units)
    def _():
        fetch(i + 1, (i + 1) % 2).start()

    fetch(i, i % 2).wait()
    row = lax.broadcasted_iota(jnp.int32, (tr, 1), 0)
    x = jnp.where(row < nv_ref[i], buf[i % 2], 0.0)
    o_ref[...] = x.T.astype(o_ref.dtype)


def _winprep(w_in_t, d):
    off, wtot = _col_layout(d)
    tr = WINPREP_TR
    src = dict(aq=0, ak=AQ_W, av=AQ_W + AKV_W, mq=AQ_W + 2 * AKV_W, mk=AQ_W + 2 * AKV_W + MQK_W,
               mv=AQ_W + 2 * AKV_W + 2 * MQK_W, mo=AQ_W + 2 * AKV_W + 2 * MQK_W + MV_W)
    src["gt"] = src["mo"] + MV_W
    src["ga"] = src["gt"] + 2 * MH
    src["gm"] = src["ga"] + d
    width = dict(aq=AQ_W, ak=AKV_W, av=AKV_W, mq=MQK_W, mk=MQK_W, mv=MV_W, mo=MV_W, ga=d, gm=d)
    n_units = wtot // tr
    starts = np.zeros((n_units,), np.int32)
    valid = np.zeros((n_units,), np.int32)
    for name, w in width.items():
        for u in range(w // tr):
            starts[off[name] // tr + u] = src[name] + u * tr
            valid[off[name] // tr + u] = tr
    starts[off["gt"] // tr] = src["gt"]
    valid[off["gt"] // tr] = 2 * MH
    assert int(starts.max()) + tr <= w_in_t.shape[0] and all(s % 8 == 0 for s in starts)
    grid_spec = pltpu.PrefetchScalarGridSpec(
        num_scalar_prefetch=2,
        grid=(n_units,),
        in_specs=[pl.BlockSpec(memory_space=pl.ANY)],
        out_specs=pl.BlockSpec((d, tr), lambda i, s, v: (0, i)),
        scratch_shapes=[pltpu.VMEM((2, tr, d), F32), pltpu.SemaphoreType.DMA((2,))],
    )
    return pl.pallas_call(
        functools.partial(_winprep_kernel, n_units),
        grid_spec=grid_spec,
        out_shape=jax.ShapeDtypeStruct((d, wtot), BF16),
        compiler_params=_cparams(("arbitrary",)),
        name="winprep",
    )(jnp.asarray(starts), jnp.asarray(valid), w_in_t)


def _inproj_kernel(x_ref, g_ref, sh_ref, sc_ref, w_ref, o_ref, u_scr):
    @pl.when(pl.program_id(1) == 0)
    def _():
        y = _rms(x_ref[...], g_ref[...])
        u_scr[...] = (y * (1.0 + sc_ref[...]) + sh_ref[...]).astype(BF16)

    o_ref[...] = jnp.dot(u_scr[...], w_ref[...], preferred_element_type=F32)


def _inproj(x2d, g, mod_specs, mod_args, w_r, tm, name):
    n, d = x2d.shape
    w = w_r.shape[1]
    tn = 1024
    return pl.pallas_call(
        _inproj_kernel,
        grid=(n // tm, w // tn),
        in_specs=[pl.BlockSpec((tm, d), lambda i, j: (i, 0)),
                  pl.BlockSpec((1, d), lambda i, j: (0, 0)),
                  mod_specs[0], mod_specs[1],
                  pl.BlockSpec((d, tn), lambda i, j: (0, j))],
        out_specs=pl.BlockSpec((tm, tn), lambda i, j: (i, j)),
        out_shape=jax.ShapeDtypeStruct((n, w), F32),
        scratch_shapes=[pltpu.VMEM((tm, d), BF16)],
        compiler_params=_cparams(("arbitrary", "arbitrary")),
        name=name,
    )(x2d, g, mod_args[0], mod_args[1], w_r)


def _attn_prompt_kernel(sink_ref, q_ref, kc_ref, vc_ref, vp_ref, bias_ref, bd_ref, qg_ref, kg_ref, o_ref, kn_ref,
                        kprev):
    n = pl.program_id(1)
    lq = q_ref.shape[0]

    @pl.when(n == 0)
    def _():
        kprev[...] = jnp.zeros_like(kprev)

    def head_rms(x, g):
        x2 = x * x
        hi = x2.astype(BF16)
        lo = (x2 - hi.astype(F32)).astype(BF16)
        bd = bd_ref[0:x.shape[1], 0:x.shape[1]]
        ss = jnp.dot(hi, bd, preferred_element_type=F32) + jnp.dot(lo, bd, preferred_element_type=F32)
        return x * lax.rsqrt(ss * (1.0 / HD) + NORM_EPS) * g

    qn = head_rms(q_ref[...], qg_ref[...] * (HD ** -0.5))
    kn = head_rms(kc_ref[...], kg_ref[...])
    kn_ref[...] = kn
    grp = lax.broadcasted_iota(jnp.int32, (GQA * lq, 1), 0) // lq
    ones_col = jnp.where(lax.broadcasted_iota(jnp.int32, (2 * lq, HD), 1) == 0, 1.0, 0.0)
    for h in range(ATTN_KV):
        sl = slice(h * HD, (h + 1) * HD)
        kctx = jnp.concatenate([kprev[:, sl], kn[:, sl]], axis=0).astype(BF16)
        v = jnp.concatenate([vp_ref[:, sl], vc_ref[:, sl]], axis=0)
        vext = jnp.concatenate([v, ones_col], axis=1).astype(BF16)
        q4 = jnp.concatenate([qn[:, (h * GQA + g) * HD:(h * GQA + g + 1) * HD] for g in range(GQA)],
                             axis=0).astype(BF16)
        s = lax.dot_general(q4, kctx, (((1,), (1,)), ((), ())), preferred_element_type=F32) + bias_ref[h]
        sink = jnp.zeros((GQA * lq, 1), F32)
        for g in range(GQA):
            sink = jnp.where(grp == g, sink_ref[h * GQA + g], sink)
        m = jnp.maximum(jnp.max(s, axis=-1, keepdims=True), sink)
        p = jnp.exp(s - m).astype(BF16)
        oe = jnp.dot(p, vext, preferred_element_type=F32)
        o = oe[:, :HD] / (oe[:, HD:HD + 1] + jnp.exp(sink - m))
        for g in range(GQA):
            hq = h * GQA + g
            o_ref[:, hq * HD:(hq + 1) * HD] = o[g * lq:(g + 1) * lq].astype(o_ref.dtype)
    kprev[...] = kn


def _attn_bias(lq):
    t = jnp.arange(lq)[:, None]
    j = jnp.arange(2 * lq)[None, :]
    dist = lq + t - j
    inwin = (dist >= 0) & (dist < lq)
    slopes = jnp.asarray(_SLOPES, F32).reshape(ATTN_KV, GQA, 1, 1)
    b = -slopes * dist.astype(F32)[None, None]
    variants = [jnp.where((inwin & (j >= lq))[None, None], b, -jnp.inf), jnp.where(inwin[None, None], b, -jnp.inf)]
    return jnp.stack(variants).reshape(2, ATTN_KV, GQA * lq, 2 * lq)


def _attn_prompt(proj, off, b, s, sinks, qg, kg):
    lq = 128
    nq = s // lq
    n = b * s
    aq_b, ak_b, av_b = off["aq"] // AQ_W, off["ak"] // AKV_W, off["av"] // AKV_W
    cur = lambda bi, ni: bi * nq + ni
    prv = lambda bi, ni: bi * nq + jnp.maximum(ni - 1, 0)
    hid = jnp.arange(AQ_W) // HD
    same_head = (hid[:, None] == hid[None, :]).astype(BF16)
    return pl.pallas_call(
        _attn_prompt_kernel,
        grid=(b, nq),
        in_specs=[pl.BlockSpec(memory_space=pltpu.SMEM),
                  pl.BlockSpec((lq, AQ_W), lambda bi, ni: (cur(bi, ni), aq_b)),
                  pl.BlockSpec((lq, AKV_W), lambda bi, ni: (cur(bi, ni), ak_b)),
                  pl.BlockSpec((lq, AKV_W), lambda bi, ni: (cur(bi, ni), av_b)),
                  pl.BlockSpec((lq, AKV_W), lambda bi, ni: (prv(bi, ni), av_b)),
                  pl.BlockSpec((None, ATTN_KV, GQA * lq, 2 * lq), lambda bi, ni: (jnp.minimum(ni, 1), 0, 0, 0)),
                  pl.BlockSpec((AQ_W, AQ_W), lambda bi, ni: (0, 0)),
                  pl.BlockSpec((1, AQ_W), lambda bi, ni: (0, 0)),
                  pl.BlockSpec((1, AKV_W), lambda bi, ni: (0, 0))],
        out_specs=[pl.BlockSpec((lq, AQ_W), lambda bi, ni: (cur(bi, ni), 0)),
                   pl.BlockSpec((lq, AKV_W), lambda bi, ni: (cur(bi, ni), 0))],
        out_shape=[jax.ShapeDtypeStruct((n, AQ_W), BF16), jax.ShapeDtypeStruct((n, AKV_W), F32)],
        scratch_shapes=[pltpu.VMEM((lq, AKV_W), F32)],
        compiler_params=_cparams(("arbitrary", "arbitrary")),
        name="attn_prompt",
    )(sinks, proj, proj, proj, proj, _attn_bias(lq), same_head,
      jnp.tile(qg, (1, ATTN_HEADS)), jnp.tile(kg, (1, ATTN_KV)))


def _attn_sample_kernel(q_ref, kn_ref, vn_ref, knt_ref, vnt_ref, ckt_ref, cvt_ref, qg_ref, kg_ref, kgt_ref, hs_ref,
                        o_ref, okt_ref, ovt_ref):
    tb, nh, _ = q_ref.shape
    win = ckt_ref.shape[2]
    q = _rms(q_ref[...], qg_ref[...])
    hq_i = lax.broadcasted_iota(jnp.int32, (1, nh, HD), 1)
    qbd = jnp.concatenate([jnp.where(hq_i // GQA == kv, q, 0.0) for kv in range(ATTN_KV)], axis=-1)
    kn = kn_ref[...]
    knt = knt_ref[...]
    lane = lax.broadcasted_iota(jnp.int32, (1, 1, AKV_W), 2)
    srow = lax.broadcasted_iota(jnp.int32, (1, AKV_W, 1), 1)
    rs = jnp.zeros_like(kn)
    rst = jnp.zeros_like(knt)
    for kv in range(ATTN_KV):
        msk = lane // HD == kv
        ms = jnp.sum(jnp.where(msk, kn * kn, 0.0), axis=-1, keepdims=True) * (1.0 / HD)
        rs = jnp.where(msk, lax.rsqrt(ms + NORM_EPS), rs)
        mskt = srow // HD == kv
        mst = jnp.sum(jnp.where(mskt, knt * knt, 0.0), axis=1, keepdims=True) * (1.0 / HD)
        rst = jnp.where(mskt, lax.rsqrt(mst + NORM_EPS), rst)
    knn = kn * rs * kg_ref[...][None]
    knnt = knt * rst * kgt_ref[...][None]
    vn = vn_ref[...]
    ckt = ckt_ref[...]
    cvt = cvt_ref[...]
    jj = lax.broadcasted_iota(jnp.int32, (1, 1, win), 2)
    okt_ref[...] = jnp.where(jj == win - 1, knnt, pltpu.roll(ckt, win - 1, 2))
    ovt_ref[...] = jnp.where(jj == win - 1, vnt_ref[...], pltpu.roll(cvt, win - 1, 2))
    slope = hs_ref[0][None]
    sink = hs_ref[1][None][:, :, 0:1]
    s = jnp.einsum("bhc,bcj->bhj", qbd.astype(BF16), ckt.astype(BF16), preferred_element_type=F32) * (HD ** -0.5)
    s = jnp.where(jj >= 1, s - slope * (win - jj).astype(F32), -jnp.inf)
    s_new = jnp.sum(qbd * knn, axis=-1, keepdims=True) * (HD ** -0.5)
    m = jnp.maximum(jnp.maximum(jnp.max(s, axis=-1, keepdims=True), s_new), sink)
    p = jnp.exp(s - m)
    p_new = jnp.exp(s_new - m)
    den = jnp.sum(p, axis=-1, keepdims=True) + p_new + jnp.exp(sink - m)
    of = jnp.einsum("bhj,bcj->bhc", p.astype(BF16), cvt.astype(BF16), preferred_element_type=F32) + p_new * vn
    of = of / den
    o = jnp.zeros((tb, nh, HD), F32)
    for kv in range(ATTN_KV):
        o = o + jnp.where(hq_i // GQA == kv, of[:, :, kv * HD:(kv + 1) * HD], 0.0)
    o_ref[...] = o.astype(o_ref.dtype)


def _attn_sample(q3, kn3, vn3, ckt, cvt, qg, kg, hs):
    bs, nh, _ = q3.shape
    win = ckt.shape[2]
    tb = 16
    seq3 = lambda a, b: pl.BlockSpec((tb, a, b), lambda i: (i, 0, 0))
    kg4 = jnp.tile(kg, (1, ATTN_KV))
    return pl.pallas_call(
        _attn_sample_kernel,
        grid=(bs // tb,),
        in_specs=[seq3(nh, HD), seq3(1, AKV_W), seq3(1, AKV_W), seq3(AKV_W, 1), seq3(AKV_W, 1),
                  seq3(AKV_W, win), seq3(AKV_W, win),
                  pl.BlockSpec((1, 1, HD), lambda i: (0, 0, 0)),
                  pl.BlockSpec((1, AKV_W), lambda i: (0, 0)),
                  pl.BlockSpec((AKV_W, 1), lambda i: (0, 0)),
                  pl.BlockSpec((2, nh, LANES), lambda i: (0, 0, 0))],
        out_specs=[seq3(nh, HD), seq3(AKV_W, win), seq3(AKV_W, win)],
        out_shape=[jax.ShapeDtypeStruct((bs, nh, HD), BF16),
                   jax.ShapeDtypeStruct((bs, AKV_W, win), F32),
                   jax.ShapeDtypeStruct((bs, AKV_W, win), F32)],
        compiler_params=_cparams(("arbitrary",)),
        name="attn_sample",
    )(q3, kn3, vn3, kn3.reshape(bs, AKV_W, 1), vn3.reshape(bs, AKV_W, 1), ckt, cvt,
      qg.reshape(1, 1, HD), kg4, kg4.reshape(AKV_W, 1), hs)


def _log_gates(g_pre, bi_ref, bf_ref):
    lane = lax.broadcasted_iota(jnp.int32, (1, LANES), 1)
    bias = jnp.zeros((1, LANES), F32)
    for h in range(MH):
        bias = jnp.where(lane == h, bi_ref[h], bias)
        bias = jnp.where(lane == MH + h, bf_ref[h], bias)
    pre = g_pre + bias
    logsig = jnp.minimum(pre, 0.0) - jnp.log1p(jnp.exp(-jnp.abs(pre)))
    return jnp.where(lane < MH, pre, logsig)


def _mlstm_prompt_kernel(bi_ref, bf_ref, q_ref, k_ref, v_ref, o_ref, g_ref, ng_ref,
                         hm_ref, c_ref, n_ref, m_ref, m_scr):
    ci = pl.program_id(1)
    nb, ln = q_ref.shape[0], q_ref.shape[1]

    @pl.when(ci == 0)
    def _():
        c_ref[...] = jnp.zeros_like(c_ref)
        n_ref[...] = jnp.zeros_like(n_ref)
        m_scr[...] = jnp.zeros_like(m_scr)

    row = lax.broadcasted_iota(jnp.int32, (ln, ln), 0)
    col = lax.broadcasted_iota(jnp.int32, (ln, ln), 1)
    causal = row >= col
    tril = causal.astype(F32)
    lane = lax.broadcasted_iota(jnp.int32, (1, LANES), 1)
    for j in range(nb):
        lf = _log_gates(g_ref[j], bi_ref, bf_ref)
        bc = jnp.dot(tril, lf, preferred_element_type=F32, precision=lax.Precision.HIGHEST)
        lft = lf.T
        bct = bc.T
        m_out = jnp.zeros((1, LANES), F32)
        for h in range(MH):
            i_row = lft[h:h + 1, :]
            b_row = bct[MH + h:MH + h + 1, :]
            i_col = lf[:, h:h + 1]
            b_col = bc[:, MH + h:MH + h + 1]
            m_prev = m_scr[j * MH + h][:, 0:1]
            log_d = jnp.where(causal, i_row + b_col - b_row, -jnp.inf)
            m_inter = m_prev + b_col
            m_t = jnp.maximum(m_inter, jnp.max(log_d, axis=-1, keepdims=True))
            d = jnp.exp(log_d - m_t)
            a_inter = jnp.exp(m_inter - m_t)
            q = q_ref[j, :, h * MDK:(h + 1) * MDK]
            k = k_ref[j, :, h * MDK:(h + 1) * MDK] * (MDK ** -0.5)
            qb = q.astype(BF16)
            vb = v_ref[j, :, h * MDV:(h + 1) * MDV].astype(BF16)
            w = lax.dot_general(qb, k.astype(BF16), (((1,), (1,)), ((), ())), preferred_element_type=F32) * d
            c_old = c_ref[j, h]
            n_old = n_ref[j, h:h + 1, :]
            num = (jnp.dot(w.astype(BF16), vb, preferred_element_type=F32)
                   + jnp.dot(qb, c_old.astype(BF16), preferred_element_type=F32) * a_inter)
            den = jnp.sum(w, axis=-1, keepdims=True) + a_inter * jnp.sum(q * n_old, axis=-1, keepdims=True)
            den = jnp.maximum(jnp.abs(den), jnp.exp(-m_t))
            hh = num / den
            m_new = m_t[ln - 1:ln, :]
            b_last = b_col[ln - 1:ln, :]
            decay = jnp.exp(i_col + b_last - b_col - m_new)
            carry = jnp.exp(m_prev + b_last - m_new)
            kd = k * decay
            c_ref[j, h] = carry * c_old + jnp.dot(kd.T.astype(BF16), vb, preferred_element_type=F32)
            n_ref[j, h:h + 1, :] = carry * n_old + jnp.sum(kd, axis=0, keepdims=True)
            m_scr[j * MH + h] = jnp.broadcast_to(m_new, (1, LANES))
            m_out = jnp.where(lane == h, m_new, m_out)
            hn = (_rms(hh, ng_ref[:, h * MDV:(h + 1) * MDV])
                  * jax.nn.sigmoid(o_ref[j, :, h * MDV:(h + 1) * MDV]))
            hm_ref[j, :, h * MDV:(h + 1) * MDV] = hn.astype(hm_ref.dtype)
        m_ref[j] = m_out


def _mlstm_prompt(proj, off, b, s, b_i, b_f, ng):
    ln = MLSTM_L
    nc = s // ln
    nb = 1
    mq_b, mk_b = off["mq"] // MQK_W, off["mk"] // MQK_W
    mv_b, mo_b, gt_b = off["mv"] // MV_W, off["mo"] // MV_W, off["gt"] // LANES
    proj3 = proj.reshape(b, s, proj.shape[1])
    smem = pl.BlockSpec(memory_space=pltpu.SMEM)
    col = lambda width, cb: pl.BlockSpec((nb, ln, width), lambda bi, ci: (bi, ci, cb))
    hm, c, n_state, m = pl.pallas_call(
        _mlstm_prompt_kernel,
        grid=(b // nb, nc),
        in_specs=[smem, smem, col(MQK_W, mq_b), col(MQK_W, mk_b), col(MV_W, mv_b), col(MV_W, mo_b),
                  col(LANES, gt_b), pl.BlockSpec((1, MV_W), lambda bi, ci: (0, 0))],
        out_specs=[col(MV_W, 0),
                   pl.BlockSpec((nb, MH, MDK, MDV), lambda bi, ci: (bi, 0, 0, 0)),
                   pl.BlockSpec((nb, MH, MDK), lambda bi, ci: (bi, 0, 0)),
                   pl.BlockSpec((nb, 1, LANES), lambda bi, ci: (bi, 0, 0))],
        out_shape=[jax.ShapeDtypeStruct((b, s, MV_W), BF16),
                   jax.ShapeDtypeStruct((b, MH, MDK, MDV), F32),
                   jax.ShapeDtypeStruct((b, MH, MDK), F32),
                   jax.ShapeDtypeStruct((b, 1, LANES), F32)],
        scratch_shapes=[pltpu.VMEM((nb * MH, 1, LANES), F32)],
        compiler_params=_cparams(("arbitrary", "arbitrary")),
        name="mlstm_prompt",
    )(b_i, b_f, proj3, proj3, proj3, proj3, proj3, ng)
    return hm.reshape(b * s, MV_W), c, n_state, m


def _mlstm_sample_kernel(bi_ref, bf_ref, q_ref, k_ref, v_ref, o_ref, g_ref, ng_ref, c0_ref, n0_ref, m0_ref,
                         hm_ref, c_ref, n_ref, m_ref):
    tb = q_ref.shape[0]
    lf = _log_gates(g_ref[...], bi_ref, bf_ref)
    lane = lax.broadcasted_iota(jnp.int32, (1, LANES), 1)
    m_out = jnp.zeros((tb, LANES), F32)
    for h in range(MH):
        li = lf[:, h:h + 1]
        lfg = lf[:, MH + h:MH + h + 1]
        m_prev = m0_ref[:, h:h + 1]
        m_inter = m_prev + lfg
        m_t = jnp.maximum(m_inter, li)
        d = jnp.exp(li - m_t)
        a = jnp.exp(m_inter - m_t)
        q = q_ref[:, h * MDK:(h + 1) * MDK]
        k = k_ref[:, h * MDK:(h + 1) * MDK] * (MDK ** -0.5)
        v = v_ref[:, h * MDV:(h + 1) * MDV]
        n_old = n0_ref[:, h, :]
        w = jnp.sum(q * k, axis=-1, keepdims=True) * d
        den = w + a * jnp.sum(q * n_old, axis=-1, keepdims=True)
        den = jnp.maximum(jnp.abs(den), jnp.exp(-m_t))
        dk = k * d
        qt = q.T
        dkt = dk.T
        rows = []
        for b in range(tb):
            c_old = c0_ref[b, h]
            qc = jnp.sum(c_old * qt[:, b:b + 1], axis=0, keepdims=True)
            a_b = a[b:b + 1, :]
            vrow = v[b:b + 1, :]
            rows.append((w[b:b + 1, :] * vrow + qc * a_b) / den[b:b + 1, :])
            c_ref[b, h] = a_b * c_old + dkt[:, b:b + 1] * vrow
        hh = jnp.concatenate(rows, axis=0)
        n_ref[:, h, :] = a * n_old + dk
        m_out = jnp.where(lane == h, m_t, m_out)
        hn = _rms(hh, ng_ref[:, h * MDV:(h + 1) * MDV]) * jax.nn.sigmoid(o_ref[:, h * MDV:(h + 1) * MDV])
        hm_ref[:, h * MDV:(h + 1) * MDV] = hn.astype(hm_ref.dtype)
    m_ref[...] = m_out


def _mlstm_sample(proj, off, bs, b_i, b_f, ng, c0, n0, m0):
    tb = 8
    mq_b, mk_b = off["mq"] // MQK_W, off["mk"] // MQK_W
    mv_b, mo_b, gt_b = off["mv"] // MV_W, off["mo"] // MV_W, off["gt"] // LANES
    smem = pl.BlockSpec(memory_space=pltpu.SMEM)
    return pl.pallas_call(
        _mlstm_sample_kernel,
        grid=(bs // tb,),
        in_specs=[smem, smem,
                  pl.BlockSpec((tb, MQK_W), lambda i: (i, mq_b)),
                  pl.BlockSpec((tb, MQK_W), lambda i: (i, mk_b)),
                  pl.BlockSpec((tb, MV_W), lambda i: (i, mv_b)),
                  pl.BlockSpec((tb, MV_W), lambda i: (i, mo_b)),
                  pl.BlockSpec((tb, LANES), lambda i: (i, gt_b)),
                  pl.BlockSpec((1, MV_W), lambda i: (0, 0)),
                  pl.BlockSpec((tb, MH, MDK, MDV), lambda i: (i, 0, 0, 0)),
                  pl.BlockSpec((tb, MH, MDK), lambda i: (i, 0, 0)),
                  pl.BlockSpec((tb, MH), lambda i: (i, 0))],
        out_specs=[pl.BlockSpec((tb, MV_W), lambda i: (i, 0)),
                   pl.BlockSpec((tb, MH, MDK, MDV), lambda i: (i, 0, 0, 0)),
                   pl.BlockSpec((tb, MH, MDK), lambda i: (i, 0, 0)),
                   pl.BlockSpec((tb, LANES), lambda i: (i, 0))],
        out_shape=[jax.ShapeDtypeStruct((bs, MV_W), BF16),
                   jax.ShapeDtypeStruct((bs, MH, MDK, MDV), F32),
                   jax.ShapeDtypeStruct((bs, MH, MDK), F32),
                   jax.ShapeDtypeStruct((bs, LANES), F32)],
        compiler_params=_cparams(("arbitrary",)),
        name="mlstm_sample",
    )(b_i, b_f, proj, proj, proj, proj, proj, ng, c0, n0, m0)


def _postmix_kernel(n_exp, n_main, *refs):
    if n_main is None:
        _postmix_tile(n_exp, *refs)
        return
    tile_refs, u2s_ref, out_refs = refs[:15], refs[15], refs[16:]
    i = pl.program_id(0)

    @pl.when(i < n_main)
    def _():
        _postmix_tile(n_exp, *tile_refs, *out_refs)

    @pl.when(i == n_main)
    def _():
        out_refs[1][0:u2s_ref.shape[0], :] = u2s_ref[...]


def _postmix_tile(n_exp, ha_ref, hm_ref, ga_ref, gm_ref, x_ref, g1_ref, sh2_ref, sc2_ref,
                  wa_ref, wm_ref, wo_ref, gf_ref, wrh_ref, wrl_ref, br_ref, x1_ref, u2_ref, ti_ref, tg_ref):
    a = jnp.dot(ha_ref[...], wa_ref[...], preferred_element_type=F32)
    m = jnp.dot(hm_ref[...], wm_ref[...], preferred_element_type=F32)
    merged = jax.nn.sigmoid(ga_ref[...]) * a + jax.nn.sigmoid(gm_ref[...]) * m
    y = jnp.dot(merged.astype(BF16), wo_ref[...], preferred_element_type=F32)
    x1 = x_ref[...] + g1_ref[...] * y
    x1_ref[...] = x1
    u2 = _rms(x1, gf_ref[...]) * (1.0 + sc2_ref[...]) + sh2_ref[...]
    u2_ref[...] = u2
    u2h = u2.astype(BF16)
    u2l = (u2 - u2h.astype(F32)).astype(BF16)
    logits = (jnp.dot(u2h, wrh_ref[...], preferred_element_type=F32)
              + (jnp.dot(u2l, wrh_ref[...], preferred_element_type=F32)
                 + jnp.dot(u2h, wrl_ref[...], preferred_element_type=F32))) + br_ref[...]
    lane = lax.broadcasted_iota(jnp.int32, logits.shape, 1)
    lanef = lane.astype(F32)
    work = jnp.where(lane < n_exp, logits, -jnp.inf)
    vals, idxs = [], []
    for _ in range(TOP_K):
        mx = jnp.max(work, axis=-1, keepdims=True)
        am = jnp.min(jnp.where(work == mx, lanef, float(LANES)), axis=-1, keepdims=True)
        vals.append(mx)
        idxs.append(am)
        work = jnp.where(lanef == am, -jnp.inf, work)
    es = [jnp.exp(v - vals[0]) for v in vals]
    tot = es[0] + es[1] + es[2] + es[3]
    ti = jnp.zeros(logits.shape, F32)
    tg = jnp.zeros(logits.shape, F32)
    for kk in range(TOP_K):
        ti = jnp.where(lane == kk, idxs[kk], ti)
        tg = jnp.where(lane == kk, es[kk] / tot, tg)
    ti_ref[...] = ti.astype(jnp.int32)
    tg_ref[...] = tg


def _postmix(ha, hm, proj, off, x2d, mod_block, mod_idx, mod, wa, wm, wo, gf, wr, br, n_exp, tm, name, u2_tail=None):
    n, d = x2d.shape
    nt = n // tm
    ga_b, gm_b = off["ga"] // d, off["gm"] // d
    const = lambda shape: pl.BlockSpec(shape, lambda i: (0,) * len(shape), pipeline_mode=pl.Buffered(1))
    ci = (lambda i: i) if u2_tail is None else (lambda i: jnp.minimum(i, nt - 1))
    row = lambda w, cb=0: pl.BlockSpec((tm, w), lambda i: (ci(i), cb))
    mspec = lambda k: pl.BlockSpec(mod_block, lambda i: mod_idx(ci(i), k))
    in_specs = [row(AQ_W), row(MV_W), row(d, ga_b), row(d, gm_b), row(d),
                mspec(2), mspec(3), mspec(4),
                const((AQ_W, d)), const((MV_W, d)), const((d, d)),
                const((1, d)), const((d, LANES)), const((d, LANES)), const((1, LANES))]
    args = [ha, hm, proj, proj, x2d, mod, mod, mod, wa, wm, wo, gf, wr[0], wr[1], br]
    n_u2 = n
    if u2_tail is not None:
        assert u2_tail.shape[0] <= tm
        in_specs.append(const(u2_tail.shape))
        args.append(u2_tail)
        n_u2 = n + u2_tail.shape[0]
    return pl.pallas_call(
        functools.partial(_postmix_kernel, n_exp, None if u2_tail is None else nt),
        grid=(nt if u2_tail is None else nt + 1,),
        in_specs=in_specs,
        out_specs=[row(d), pl.BlockSpec((tm, d), lambda i: (i, 0)), row(LANES), row(LANES)],
        out_shape=[jax.ShapeDtypeStruct((n, d), F32), jax.ShapeDtypeStruct((n_u2, d), F32),
                   jax.ShapeDtypeStruct((n, LANES), jnp.int32), jax.ShapeDtypeStruct((n, LANES), F32)],
        compiler_params=_cparams(("arbitrary",)),
        name=name,
    )(*args)


def _moe_kernel(n_s, n_c, se_ref, sf_ref, sn_ref, sr_ref, nu_ref, tokn_ref, u2_ref, wu_ref, wd_ref, bu_ref,
                bd_ref, yb_ref, xg, xb, acc, wub, wdp, wdb, act_carry, zbuf, gsem, wsem, zsem):
    s = pl.program_id(0)
    c = pl.program_id(1)
    sub = MOE_SUB
    ns = sn_ref[s]
    ns_prev = sn_ref[jnp.maximum(s - 1, 0)]
    fb = sf_ref[s]
    grp = MOE_GATHER_GROUP
    groups = (sr_ref[s] + grp - 1) // grp
    groups_next = jnp.where(s + 1 < n_s, (sr_ref[jnp.minimum(s + 1, n_s - 1)] + grp - 1) // grp, 0)

    def gather_group(g, carry):
        base = pl.multiple_of(g * grp, grp)
        for i in range(grp):
            j = base + i
            pltpu.make_async_copy(u2_ref.at[pl.ds(tokn_ref[0, j], 1)], xg.at[pl.ds(j, 1)], gsem).start()
        return carry

    def wb_copy(m, blk):
        return pltpu.make_async_copy(acc.at[pl.ds(pl.multiple_of(m * sub, sub), sub)],
                                     yb_ref.at[pl.ds(pl.multiple_of(blk * sub, sub), sub)], wsem.at[m])

    @pl.when((s == 0) & (c == 0))
    def _():
        xg[...] = jnp.zeros_like(xg)
        wdb[...] = jnp.zeros_like(wdb)
        act_carry[...] = jnp.zeros_like(act_carry)

    @pl.when(c == 0)
    def _():
        def wait_group(i, carry):
            pltpu.make_async_copy(u2_ref.at[pl.ds(0, grp)], xg.at[pl.ds(0, grp)], gsem).wait()
            return carry
        lax.fori_loop(0, jnp.where(ns_prev > 0, xg.shape[0] // grp, groups), wait_group, 0)
        for m in range(MOE_NSUB):
            @pl.when((s > 0) & (m < ns_prev))
            def _():
                wb_copy(m, 0).wait()

        def prep(m, carry):
            r0 = pl.multiple_of(m * sub, sub)
            xb[pl.ds(r0, sub), :] = xg[pl.ds(r0, sub), :].astype(BF16)
            acc[pl.ds(r0, sub), :] = jnp.broadcast_to(bd_ref[...], (sub, acc.shape[1]))
            return carry
        lax.fori_loop(0, ns, prep, 0)

    def down(m, act):
        r0 = pl.multiple_of(m * sub, sub)
        pw = 4 * LANES
        for p in range(wdb.shape[1] // pw):
            cols = slice(p * pw, (p + 1) * pw)
            acc[pl.ds(r0, sub), cols] += jnp.dot(act, wdb[:, cols], preferred_element_type=F32)

    @pl.when(ns > 0)
    def _():
        down(ns - 1, act_carry[...])
        share = xg.shape[0] // grp // n_c
        for g in range(share):
            gather_group(c * share + g, 0)
        wub[...] = wu_ref[...].astype(BF16)
        half = LANES // 2
        for cb in range(wd_ref.shape[1] // LANES):
            cols = slice(cb * LANES, (cb + 1) * LANES)
            for g in range(wd_ref.shape[0] // LANES):
                wdp[cb, pl.ds(g * LANES, half, stride=2), :] = wd_ref[g * LANES:g * LANES + half, cols]
                wdp[cb, pl.ds(g * LANES + 1, half, stride=2), :] = wd_ref[g * LANES + half:(g + 1) * LANES, cols]
            wdb[:, cols] = wdp[cb].astype(BF16)

    def block(m, act_prev):
        r0 = pl.multiple_of(m * sub, sub)
        x = xb[pl.ds(r0, sub), :]
        gw = 2 * LANES
        hs = [jnp.dot(x, wub[:, g * gw:(g + 1) * gw], preferred_element_type=F32) + bu_ref[:, g * gw:(g + 1) * gw]
              for g in range(wub.shape[1] // gw)]
        if act_prev is not None:
            down(jnp.maximum(m - 1, 0), act_prev)
        even = lax.broadcasted_iota(jnp.int32, (sub, LANES), 1) % 2 == 0
        parts = []
        for h in hs:
            h0 = h[:, :LANES]
            h1 = h[:, LANES:]
            glu = jnp.where(even, h0, pltpu.roll(h1, 1, 1))
            lin = jnp.where(even, pltpu.roll(h0, LANES - 1, 1), h1)
            glu = jnp.minimum(glu, SWIGLU_LIMIT)
            lin = jnp.clip(lin, -SWIGLU_LIMIT, SWIGLU_LIMIT)
            parts.append(glu * jax.nn.sigmoid(SWIGLU_ALPHA * glu) * (lin + 1.0))
        return jnp.concatenate(parts, axis=1).astype(BF16)

    def wb_after(m):
        @pl.when((c == n_c - 1) & (m > 0))
        def _():
            wb_copy(m - 1, fb + m - 1).start()

    def pair(p, act_prev):
        act = block(2 * p + 1, block(2 * p, act_prev))
        wb_after(2 * p)
        wb_after(2 * p + 1)
        return act

    def single(m, act_prev):
        act = block(m, act_prev)
        wb_after(m)
        return act

    def first_pair():
        act = block(1, block(0, None))
        wb_after(1)
        return act

    act_last = lax.cond(ns >= 2, first_pair, lambda: jnp.zeros(act_carry.shape, BF16))
    act_last = lax.fori_loop(1, ns // 2, pair, act_last)
    act_last = lax.fori_loop(2 * (ns // 2), ns, single, act_last)

    @pl.when((ns > 0) & (c < n_c - 1))
    def _():
        act_carry[...] = act_last

    @pl.when((ns > 0) & (c == n_c - 1))
    def _():
        down(ns - 1, act_last)
        wb_copy(ns - 1, fb + ns - 1).start()
        act_carry[...] = jnp.zeros_like(act_carry)

    per_step = (groups_next + n_c - 1) // n_c
    lax.fori_loop(jnp.where(ns > 0, groups_next, jnp.minimum(c * per_step, groups_next)),
                  jnp.minimum((c + 1) * per_step, groups_next), gather_group, 0)

    @pl.when((s == n_s - 1) & (c == n_c - 1))
    def _():
        n_blocks = yb_ref.shape[0] // sub
        zbuf[...] = jnp.zeros_like(zbuf)

        def zstart(b, carry):
            pltpu.make_async_copy(zbuf, yb_ref.at[pl.ds(pl.multiple_of(b * sub, sub), sub)], zsem).start()
            return carry

        def zwait(b, carry):
            pltpu.make_async_copy(zbuf, yb_ref.at[pl.ds(0, sub)], zsem).wait()
            return carry
        lax.fori_loop(nu_ref[0], n_blocks, zstart, 0)
        lax.fori_loop(nu_ref[0], n_blocks, zwait, 0)


def _moe(u2, tabs, n_blocks, w_up, b_up, w_down, b_down):
    sb_e, sb_fb, sb_ns, sb_rows, n_used, sb_tok = tabs
    n_s = sb_e.shape[0]
    n_exp, d, de2 = w_up.shape
    de = de2 // 2
    hc = min(MOE_HC, de)
    n_c = de // hc
    rmax = MOE_NSUB * MOE_SUB
    assert rmax % MOE_GATHER_GROUP == 0 and hc % LANES == 0

    def chunk(s, c, sn):
        return jnp.where(sn[s] > 0, c, jnp.where(s == 0, 0, n_c - 1))

    grid_spec = pltpu.PrefetchScalarGridSpec(
        num_scalar_prefetch=5,
        grid=(n_s, n_c),
        in_specs=[
            pl.BlockSpec((None, 1, rmax), lambda s, c, se, sf, sn, sr, nu: (s + 1, 0, 0), memory_space=pltpu.SMEM),
            pl.BlockSpec(memory_space=pl.ANY),
            pl.BlockSpec((None, d, 2 * hc), lambda s, c, se, sf, sn, sr, nu: (se[s], 0, chunk(s, c, sn))),
            pl.BlockSpec((None, hc, d), lambda s, c, se, sf, sn, sr, nu: (se[s], chunk(s, c, sn), 0)),
            pl.BlockSpec((None, 1, 2 * hc), lambda s, c, se, sf, sn, sr, nu: (se[s], 0, chunk(s, c, sn))),
            pl.BlockSpec((None, 1, d), lambda s, c, se, sf, sn, sr, nu: (se[s], 0, 0)),
        ],
        out_specs=pl.BlockSpec(memory_space=pl.ANY),
        scratch_shapes=[pltpu.VMEM((rmax, d), F32), pltpu.VMEM((rmax, d), BF16), pltpu.VMEM((rmax, d), F32),
                        pltpu.VMEM((d, 2 * hc), BF16), pltpu.VMEM((d // LANES, hc, LANES), F32),
                        pltpu.VMEM((hc, d), BF16), pltpu.VMEM((MOE_SUB, hc), BF16),
                        pltpu.VMEM((MOE_SUB, d), F32),
                        pltpu.SemaphoreType.DMA, pltpu.SemaphoreType.DMA((MOE_NSUB,)), pltpu.SemaphoreType.DMA],
    )
    return pl.pallas_call(
        functools.partial(_moe_kernel, n_s, n_c),
        grid_spec=grid_spec,
        out_shape=jax.ShapeDtypeStruct((n_blocks * MOE_SUB, d), F32),
        compiler_params=_cparams(("arbitrary", "arbitrary")),
        name="moe_experts",
    )(sb_e, sb_fb, sb_ns, sb_rows, n_used, sb_tok, u2, w_up, w_down, b_up.reshape(n_exp, 1, de2),
      b_down.reshape(n_exp, 1, d))


def _combine_kernel(pc_ref, pn_ref, yb_ref, x1_ref, g2_ref, tg_ref, o_ref, buf, sem):
    s = pl.program_id(0)
    ns = pl.num_programs(0)
    tm = x1_ref.shape[0]

    rows = 8

    def issue_token(p_ref, slot, t):
        for kk in range(TOP_K):
            pltpu.make_async_copy(yb_ref.at[pl.ds(p_ref[0, t * TOP_K + kk], 1)],
                                  buf.at[slot, kk, pl.ds(t, 1)], sem.at[slot]).start()

    def issue(p_ref, slot):
        def body(t, carry):
            issue_token(p_ref, slot, t)
            return carry
        lax.fori_loop(0, tm, body, 0)

    def drain(slot):
        for kk in range(TOP_K):
            pltpu.make_async_copy(yb_ref.at[pl.ds(0, tm)], buf.at[slot, kk], sem.at[slot]).wait()

    slot = s % 2

    def out_rows(i):
        rs = pl.ds(pl.multiple_of(i * rows, rows), rows)
        tg = tg_ref[rs, :]
        acc = tg[:, 0:1] * buf[slot, 0, rs, :]
        for kk in range(1, TOP_K):
            acc = acc + tg[:, kk:kk + 1] * buf[slot, kk, rs, :]
        g2 = g2_ref[...] if g2_ref.shape[0] == 1 else g2_ref[rs, :]
        o_ref[rs, :] = x1_ref[rs, :] + g2 * acc

    @pl.when(s == 0)
    def _():
        issue(pc_ref, 0)

    drain(slot)

    @pl.when(s + 1 < ns)
    def _():
        def body(i, carry):
            for j in range(rows):
                issue_token(pn_ref, 1 - slot, i * rows + j)
            out_rows(i)
            return carry
        lax.fori_loop(0, tm // rows, body, 0)

    @pl.when(s + 1 == ns)
    def _():
        def body(i, carry):
            out_rows(i)
            return carry
        lax.fori_loop(0, tm // rows, body, 0)


def _combine(pos, yb, x1, g2_spec, g2_arg, tg, tm, name):
    n, d = x1.shape
    ns = n // tm
    pos3 = pos.reshape(ns, 1, tm * TOP_K)
    return pl.pallas_call(
        _combine_kernel,
        grid=(ns,),
        in_specs=[pl.BlockSpec((None, 1, tm * TOP_K), lambda s: (s, 0, 0), memory_space=pltpu.SMEM),
                  pl.BlockSpec((None, 1, tm * TOP_K), lambda s: (jnp.minimum(s + 1, ns - 1), 0, 0),
                               memory_space=pltpu.SMEM),
                  pl.BlockSpec(memory_space=pl.ANY),
                  pl.BlockSpec((tm, d), lambda s: (s, 0)),
                  g2_spec,
                  pl.BlockSpec((tm, LANES), lambda s: (s, 0))],
        out_specs=pl.BlockSpec((tm, d), lambda s: (s, 0)),
        out_shape=jax.ShapeDtypeStruct((n, d), F32),
        scratch_shapes=[pltpu.VMEM((2, TOP_K, tm, d), F32), pltpu.SemaphoreType.DMA((2,))],
        compiler_params=_cparams(("arbitrary",)),
        name=name,
    )(pos3, pos3, yb, x1, g2_arg, tg)


def _routing(top_e, n_exp, n_blocks):
    sub, nsub = MOE_SUB, MOE_NSUB
    rmax = sub * nsub
    i32 = jnp.int32
    nk = top_e.size
    flat_e = top_e.reshape(nk)
    oh = (flat_e[:, None] == jnp.arange(n_exp, dtype=i32)[None, :]).astype(i32)
    csum = jnp.cumsum(oh, axis=0)
    rank = jnp.sum((csum - oh) * oh, axis=1)
    counts = csum[-1]
    nblk = (counts + sub - 1) // sub
    blk_end = jnp.cumsum(nblk)
    blk_start = blk_end - nblk
    pos = jnp.sum(oh * blk_start[None, :], axis=1) * sub + rank
    n_used = blk_end[-1]
    nsb = (nblk + nsub - 1) // nsub
    sb_end = jnp.cumsum(nsb)
    sb_start = sb_end - nsb
    n_sb = sb_end[-1]
    n_real = n_exp + n_blocks // nsub + 1
    sidx = jnp.arange(n_real, dtype=i32)
    e_of = jnp.minimum(jnp.sum((sidx[:, None] >= sb_end[None, :]).astype(i32), axis=1), n_exp - 1)
    oh_s = (e_of[:, None] == jnp.arange(n_exp, dtype=i32)[None, :]).astype(i32)
    k_in = sidx - jnp.sum(oh_s * sb_start[None, :], axis=1)
    fb = jnp.sum(oh_s * blk_start[None, :], axis=1) + nsub * k_in
    ns = jnp.clip(jnp.sum(oh_s * nblk[None, :], axis=1) - nsub * k_in, 0, nsub) * (sidx < n_sb)
    last_e = jnp.sum(jnp.where(sidx == n_sb - 1, e_of, 0))
    e_of = jnp.where(sidx < n_sb, e_of, last_e)
    zero = jnp.zeros((1,), i32)
    sb_e = jnp.concatenate([e_of[:1], e_of, last_e.reshape(1)]).astype(i32)
    sb_fb = jnp.concatenate([zero, fb, zero]).astype(i32)
    sb_ns = jnp.concatenate([zero, ns, zero]).astype(i32)
    rows = jnp.clip(jnp.sum(oh_s * counts[None, :], axis=1) - rmax * k_in, 0, rmax) * (sidx < n_sb)
    sb_rows = jnp.concatenate([zero, rows, zero]).astype(i32)
    slot = 1 + jnp.sum(oh * sb_start[None, :], axis=1) + rank // rmax
    tok = (jnp.arange(nk, dtype=i32) // TOP_K).astype(i32)
    flat = (slot * rmax + rank % rmax).astype(i32)
    sb_tok = jnp.zeros(((n_real + 3) * rmax,), i32).at[flat].set(tok, unique_indices=True)
    tabs = (sb_e, sb_fb, sb_ns, sb_rows, n_used.reshape(1).astype(i32), sb_tok.reshape(n_real + 3, 1, rmax))
    return pos.astype(i32), tabs


def kernel(x_prompt, x_sample, cache_k, cache_v, state_C, state_n, state_m, c_prompt, c_sample, w_ada, b_ada,
           g_mix, w_in, b_igate, b_fgate, q_norm_g, k_norm_g, attn_sinks, mlstm_norm_g, w_attn_up, w_mlstm_up,
           w_out, g_ffn, w_router, b_router, w_up, b_up, w_down, b_down):
    bp, sp, d = x_prompt.shape
    bs = x_sample.shape[0]
    assert x_sample.shape[1] == 1 and w_ada.shape[0] == 1
    n_p = bp * sp
    win = cache_k.shape[2]
    n_exp = w_router.shape[2]
    de = w_down.shape[2]
    off, _ = _col_layout(d)

    w_r = _winprep(jnp.transpose(w_in[0]), d)
    wa =w_attn_up[0].astype(BF16)
    wm = w_mlstm_up[0].astype(BF16)
    wo = w_out[0].astype(BF16)
    wr32 = jnp.pad(w_router[0], ((0, 0), (0, LANES - n_exp)))
    wrh = wr32.astype(BF16)
    wr = (wrh, (wr32 - wrh.astype(F32)).astype(BF16))
    br = jnp.pad(b_router[0], (0, LANES - n_exp)).reshape(1, LANES)
    gmix = g_mix[0].reshape(1, d)
    gffn = g_ffn[0].reshape(1, d)
    qg = q_norm_g[0].reshape(1, HD)
    kg = k_norm_g[0].reshape(1, HD)
    ng = mlstm_norm_g[0].reshape(1, MV_W)
    sinks = attn_sinks[0]
    hs = jnp.stack([jnp.broadcast_to(jnp.asarray(_SLOPES, F32)[:, None], (ATTN_HEADS, LANES)),
                    jnp.broadcast_to(sinks[:, None], (ATTN_HEADS, LANES))])

    mod = _adaln(jnp.concatenate([c_prompt, c_sample], axis=0), w_ada[0], b_ada[0])
    mod_p = mod[:bp].reshape(bp, 1, N_MOD * d)
    mod_s = mod[bp:]

    xs = x_sample.reshape(bs, d)
    sspec2 = lambda kk: pl.BlockSpec((bs, d), lambda i, j: (0, kk))
    sspec1 = lambda kk: pl.BlockSpec((bs, d), lambda i: (0, kk))
    proj_s = _inproj(xs, gmix, (sspec2(0), sspec2(1)), (mod_s, mod_s), w_r, bs, "inproj_sample")
    q3 = proj_s[:, off["aq"]:off["aq"] + AQ_W].reshape(bs, ATTN_HEADS, HD)
    kn3 = proj_s[:, off["ak"]:off["ak"] + AKV_W].reshape(bs, 1, AKV_W)
    vn3 = proj_s[:, off["av"]:off["av"] + AKV_W].reshape(bs, 1, AKV_W)
    feature_major = lambda cache: jnp.swapaxes(cache[0].reshape(bs, win, AKV_W), 1, 2)
    ha_s3, kt_s, vt_s = _attn_sample(q3, kn3, vn3, feature_major(cache_k), feature_major(cache_v), qg, kg, hs)
    k_s, v_s = jnp.swapaxes(kt_s, 1, 2), jnp.swapaxes(vt_s, 1, 2)
    hm_s, c_s, n_state_s, m_s = _mlstm_sample(proj_s, off, bs, b_igate[0], b_fgate[0], ng,
                                              state_C[0], state_n[0], state_m[0])
    x1_s, u2_s, ti_s, tg_s = _postmix(
        ha_s3.reshape(bs, AQ_W), hm_s, proj_s, off, xs, (bs, d), lambda i, k: (0, k), mod_s,
        wa, wm, wo, gffn, wr, br, n_exp, bs, "postmix_sample")

    xp = x_prompt.reshape(n_p, d)
    tm_in = min(1024, sp)
    tps = sp // tm_in
    pspec = lambda t, kk, nargs: pl.BlockSpec(
        (None, 1, d), (lambda i, j: (i // t, 0, kk)) if nargs == 2 else (lambda i: (i // t, 0, kk)))
    proj_p = _inproj(xp, gmix, (pspec(tps, 0, 2), pspec(tps, 1, 2)), (mod_p, mod_p), w_r, tm_in, "inproj_prompt")
    ha_p, kn_p = _attn_prompt(proj_p, off, bp, sp, sinks, qg, kg)
    hm_p, c_p, n_state_p, m_p = _mlstm_prompt(proj_p, off, bp, sp, b_igate[0], b_fgate[0], ng)
    tm_pm = min(256, sp)
    tpp = sp // tm_pm
    x1_p, u2_all, ti_p, tg_p = _postmix(
        ha_p, hm_p, proj_p, off, xp, (None, 1, d), lambda i, k: (i // tpp, 0, k), mod_p,
        wa, wm, wo, gffn, wr, br, n_exp, tm_pm, "postmix_prompt", u2_tail=u2_s)

    nk = (n_p + bs) * TOP_K
    n_blocks = -(-nk // MOE_SUB) + n_exp
    top_e = jnp.concatenate([ti_p[:, :TOP_K], ti_s[:, :TOP_K]], axis=0)
    pos, tabs = _routing(top_e, n_exp, n_blocks)
    yb = _moe(u2_all, tabs, n_blocks, w_up[0], b_up[0], w_down[0], b_down[0])
    y_p = _combine(pos[:n_p * TOP_K], yb, x1_p, pspec(tpp, 5, 1), mod_p, tg_p, tm_pm, "combine_prompt")
    y_s = _combine(pos[n_p * TOP_K:], yb, x1_s, sspec1(5), mod_s, tg_s, bs, "combine_sample")

    kvshape = (1, bp, win, ATTN_KV, HD)
    k_p = kn_p.reshape(bp, sp, AKV_W)[:, sp - win:].reshape(kvshape)
    v_p = proj_p[:, off["av"]:off["av"] + AKV_W].reshape(bp, sp, AKV_W)[:, sp - win:].reshape(kvshape)
    return (y_p.reshape(bp, sp, d), y_s.reshape(bs, 1, d),
            k_p, v_p, c_p[None], n_state_p[None], m_p[:, 0, :MH][None],
            k_s.reshape(1, bs, win, ATTN_KV, HD), v_s.reshape(1, bs, win, ATTN_KV, HD),
            c_s[None], n_state_s[None], m_s[:, :MH][None])
```
